```python
import functools
import jax, jax.numpy as jnp
from jax import lax
import numpy as np

D_MODEL = 1024
BATCH = 8
SEQ = 2048
DEPTH = 2
DEC_BATCH = 128
DEC_SEQ = 1
PAST_LEN = 16384
PAGE_SIZE = 128

HEAD_DIM = 64
D_A = 3 * D_MODEL // 8
D_B = D_MODEL // 4
D_C = D_MODEL - D_A - D_B
N_HEADS_B = D_B // HEAD_DIM
CONV_A = 3
CONV_C = 31
CHUNK = 128
D_IN = 3 * D_A + 2 * D_B + 2 * D_C
D_PLE = 256
D_FF = 11 * D_MODEL // 4
N_EXPERTS = 8
TOP_K = 2
D_EXPERT = 2 * D_MODEL
N_DENSE = (DEPTH + 1) // 2
N_MOE = DEPTH // 2
EPS = 1e-6

kernel_name = 'hybrid_conv_chunkmlp_conformer_decoder_step'


def rmsnorm(x, g):
    xf = x.astype(jnp.float32)
    y = xf * lax.rsqrt(jnp.mean(xf * xf, axis=-1, keepdims=True) + EPS)
    return (y * g.astype(jnp.float32)).astype(x.dtype)


def layernorm(x, g, b):
    xf = x.astype(jnp.float32)
    mu = jnp.mean(xf, axis=-1, keepdims=True)
    xc = xf - mu
    var = jnp.mean(xc * xc, axis=-1, keepdims=True)
    y = xc * lax.rsqrt(var + EPS) * g.astype(jnp.float32) + b.astype(jnp.float32)
    return y.astype(x.dtype)


def causal_dwconv(x, prev, w):
    xp = jnp.concatenate([prev.astype(x.dtype), x], axis=1)
    y = lax.conv_general_dilated(
        xp, w[:, None, :].astype(x.dtype), window_strides=(1,), padding='VALID',
        dimension_numbers=('NWC', 'WIO', 'NWC'), feature_group_count=x.shape[-1])
    return y, xp[:, -(w.shape[0] - 1):]


def chunk_mix(v, w_s, b_s):
    n, length, _ = v.shape
    n_chunks = -(-length // CHUNK)
    pad = n_chunks * CHUNK - length
    vp = jnp.pad(v, ((0, 0), (0, pad), (0, 0))).reshape(n, n_chunks, CHUNK, N_HEADS_B, HEAD_DIM)
    mask = jnp.tril(jnp.ones((CHUNK, CHUNK), dtype=bool))
    wm = jnp.where(mask[None], w_s, jnp.zeros_like(w_s)).astype(v.dtype)
    s = jnp.einsum('hts,ncshd->ncthd', wm, vp) + b_s.T[None, None, :, :, None].astype(v.dtype)
    return s.reshape(n, n_chunks * CHUNK, D_B)[:, :length]


def swiglu(x, w_gate, w_up, w_down):
    return (jax.nn.silu(x @ w_gate) * (x @ w_up)) @ w_down


def moe_swiglu(x, w_router, w_gate, w_up, w_down):
    logits = (x @ w_router).astype(jnp.float32)
    top_v, top_i = lax.top_k(logits, TOP_K)
    top_w = jax.nn.softmax(top_v, axis=-1)
    gates = jnp.einsum('...k,...ke->...e', top_w,
                       jax.nn.one_hot(top_i, N_EXPERTS, dtype=jnp.float32)).astype(x.dtype)
    out = jnp.zeros_like(x)
    for e in range(N_EXPERTS):
        out = out + gates[..., e:e + 1] * swiglu(x, w_gate[e], w_up[e], w_down[e])
    return out


def hybrid_layer(h, p_i, prev_a, prev_c, g_mix, w_in, w_conv_a, w_s, b_s, g_ln_b, b_ln_b,
                 w_conv_c, b_conv_c, g_ln_c, b_ln_c, g_out, w_o, g_ffn, ffn,
                 g_ple, w_ple_gate, w_ple_proj):
    n = rmsnorm(h, g_mix)
    z = n @ w_in
    o1, o2, o3 = D_A, 2 * D_A, 3 * D_A
    o4 = o3 + 2 * D_B
    gate_b, gate_c, xa = z[..., :o1], z[..., o1:o2], z[..., o2:o3]
    conv_a, new_a = causal_dwconv(gate_c * xa, prev_a, w_conv_a)
    y_a = gate_b * conv_a
    zb = jax.nn.gelu(z[..., o3:o4])
    u = zb[..., :D_B]
    v = layernorm(zb[..., D_B:], g_ln_b, b_ln_b)
    y_b = u * chunk_mix(v, w_s, b_s)
    zc = z[..., o4:]
    glu = zc[..., :D_C] * jax.nn.sigmoid(zc[..., D_C:])
    conv_c, new_c = causal_dwconv(glu, prev_c, w_conv_c)
    y_c = jax.nn.silu(layernorm(conv_c + b_conv_c, g_ln_c, b_ln_c))
    y = jnp.concatenate([rmsnorm(y_a, g_out[:D_A]),
                         rmsnorm(y_b, g_out[D_A:D_A + D_B]),
                         rmsnorm(y_c, g_out[D_A + D_B:])], axis=-1)
    h = h + y @ w_o
    h = h + ffn(rmsnorm(h, g_ffn))
    h = h + jax.nn.sigmoid(rmsnorm(h, g_ple) @ w_ple_gate) * (p_i @ w_ple_proj)
    return h, new_a, new_c, v


def setup_inputs(seed: int = 0) -> dict:
    key = jax.random.key(seed)
    ks = iter(jax.random.split(key, 40))
    f32 = jnp.float32

    def nrm(shape, scale):
        return jax.random.normal(next(ks), shape, f32) * scale

    def gain(shape):
        return 1.0 + nrm(shape, 0.05)

    return {
        'x_prompt': nrm((BATCH, SEQ, D_MODEL), 1.0),
        'x_sample': nrm((DEC_BATCH, DEC_SEQ, D_MODEL), 1.0),
        'state_conv_a': nrm((DEPTH, DEC_BATCH, CONV_A - 1, D_A), 1.0),
        'state_conv_c': nrm((DEPTH, DEC_BATCH, CONV_C - 1, D_C), 1.0),
        'p_prompt': nrm((DEPTH, BATCH, SEQ, D_PLE), 1.0),
        'p_sample': nrm((DEPTH, DEC_BATCH, DEC_SEQ, D_PLE), 1.0),
        'g_mix': gain((DEPTH, D_MODEL)),
        'w_in': nrm((DEPTH, D_MODEL, D_IN), D_MODEL ** -0.5),
        'w_conv_a': nrm((DEPTH, CONV_A, D_A), CONV_A ** -0.5),
        'w_s': nrm((DEPTH, N_HEADS_B, CHUNK, CHUNK), CHUNK ** -0.5),
        'b_s': 1.0 + nrm((DEPTH, N_HEADS_B, CHUNK), 0.1),
        'g_ln_b': gain((DEPTH, D_B)),
        'b_ln_b': nrm((DEPTH, D_B), 0.02),
        'w_conv_c': nrm((DEPTH, CONV_C, D_C), CONV_C ** -0.5),
        'b_conv_c': nrm((DEPTH, D_C), 0.02),
        'g_ln_c': gain((DEPTH, D_C)),
        'b_ln_c': nrm((DEPTH, D_C), 0.02),
        'g_out': gain((DEPTH, D_MODEL)),
        'w_o': nrm((DEPTH, D_MODEL, D_MODEL), D_MODEL ** -0.5),
        'g_ffn': gain((DEPTH, D_MODEL)),
        'w_ff_gate': nrm((N_DENSE, D_MODEL, D_FF), D_MODEL ** -0.5),
        'w_ff_up': nrm((N_DENSE, D_MODEL, D_FF), D_MODEL ** -0.5),
        'w_ff_down': nrm((N_DENSE, D_FF, D_MODEL), D_FF ** -0.5),
        'w_router': nrm((N_MOE, D_MODEL, N_EXPERTS), D_MODEL ** -0.5),
        'w_ex_gate': nrm((N_MOE, N_EXPERTS, D_MODEL, D_EXPERT), D_MODEL ** -0.5),
        'w_ex_up': nrm((N_MOE, N_EXPERTS, D_MODEL, D_EXPERT), D_MODEL ** -0.5),
        'w_ex_down': nrm((N_MOE, N_EXPERTS, D_EXPERT, D_MODEL), D_EXPERT ** -0.5),
        'g_ple': gain((DEPTH, D_MODEL)),
        'w_ple_gate': nrm((DEPTH, D_MODEL, D_MODEL), D_MODEL ** -0.5),
        'w_ple_proj': nrm((DEPTH, D_PLE, D_MODEL), D_PLE ** -0.5),
        'g_final': gain((D_MODEL,)),
    }


def reference(x_prompt, x_sample, state_conv_a, state_conv_c, p_prompt, p_sample,
              g_mix, w_in, w_conv_a, w_s, b_s, g_ln_b, b_ln_b, w_conv_c, b_conv_c,
              g_ln_c, b_ln_c, g_out, w_o, g_ffn, w_ff_gate, w_ff_up, w_ff_down,
              w_router, w_ex_gate, w_ex_up, w_ex_down, g_ple, w_ple_gate, w_ple_proj,
              g_final):
    hp, hs = x_prompt, x_sample
    na_p, na_s, nc_p, nc_s, nv_s = [], [], [], [], []
    for i in range(DEPTH):
        j = i // 2
        if i % 2 == 0:
            ffn = functools.partial(swiglu, w_gate=w_ff_gate[j], w_up=w_ff_up[j],
                                    w_down=w_ff_down[j])
        else:
            ffn = functools.partial(moe_swiglu, w_router=w_router[j], w_gate=w_ex_gate[j],
                                    w_up=w_ex_up[j], w_down=w_ex_down[j])

        def run(h, p_i, prev_a, prev_c):
            return hybrid_layer(h, p_i, prev_a, prev_c, g_mix[i], w_in[i], w_conv_a[i],
                                w_s[i], b_s[i], g_ln_b[i], b_ln_b[i], w_conv_c[i],
                                b_conv_c[i], g_ln_c[i], b_ln_c[i], g_out[i], w_o[i],
                                g_ffn[i], ffn, g_ple[i], w_ple_gate[i], w_ple_proj[i])

        zeros_a = jnp.zeros((hp.shape[0], CONV_A - 1, D_A), hp.dtype)
        zeros_c = jnp.zeros((hp.shape[0], CONV_C - 1, D_C), hp.dtype)
        hp, a_p, c_p, _ = run(hp, p_prompt[i], zeros_a, zeros_c)
        hs, a_s, c_s, v_s = run(hs, p_sample[i], state_conv_a[i], state_conv_c[i])
        na_p.append(a_p)
        na_s.append(a_s)
        nc_p.append(c_p)
        nc_s.append(c_s)
        nv_s.append(v_s)
    y_prompt = rmsnorm(hp, g_final)
    y_sample = rmsnorm(hs, g_final)
    return (y_prompt, y_sample, jnp.stack(na_p), jnp.stack(na_s), jnp.stack(nc_p),
            jnp.stack(nc_s), jnp.stack(nv_s))
```

```python
import functools

import jax
import jax.numpy as jnp
from jax import lax
from jax.experimental import pallas as pl
from jax.experimental.pallas import tpu as pltpu

D_MODEL = 1024
HEAD_DIM = 64
D_A = 384
D_B = 256
D_C = 384
N_HEADS_B = D_B // HEAD_DIM
CONV_A = 3
CONV_C = 31
CHUNK = 128
D_IN = 3 * D_A + 2 * D_B + 2 * D_C
D_PLE = 256
N_EXPERTS = 8
EPS = 1e-6

O1, O2, O3 = D_A, 2 * D_A, 3 * D_A
O4 = O3 + 2 * D_B

F32 = jnp.float32
BF16 = jnp.bfloat16

VMEM_LIMIT_BYTES = 56 * 1024 * 1024

MIX_ROWS = 512
CONV_ROWS = 64
A_HALO = 8
C_HALO = 32


def _rms(x, g):
    return x * lax.rsqrt(jnp.mean(x * x, axis=-1, keepdims=True) + EPS) * g


def _ln(x, g, b):
    mu = jnp.mean(x, axis=-1, keepdims=True)
    xc = x - mu
    var = jnp.mean(xc * xc, axis=-1, keepdims=True)
    return xc * lax.rsqrt(var + EPS) * g + b


def _dot(a, b):
    return jnp.dot(a.astype(BF16), b, preferred_element_type=F32)


def _mixer_tail(h, y_a, y_b, y_c, gout_ref, wo_ref):
    y = jnp.concatenate([_rms(y_a, gout_ref[:, :D_A]),
                         _rms(y_b, gout_ref[:, D_A:D_A + D_B]),
                         _rms(y_c, gout_ref[:, D_A + D_B:])], axis=-1)
    return h + _dot(y, wo_ref[...])


def _mix_seq_kernel(h_ref, gmix_ref, win_ref, wca_ref, wscat_ref, bsfull_ref, glnb_ref,
                    blnb_ref, wcc_ref, bcc_ref, glnc_ref, blnc_ref, gout_ref, wo_ref,
                    hout_ref, newa_ref, newc_ref, qext_ref, gext_ref, conv_ref):
    t = pl.program_id(1)
    tm = h_ref.shape[0]

    @pl.when(t == 0)
    def _():
        qext_ref[0:A_HALO, :] = jnp.zeros((A_HALO, D_A), F32)
        gext_ref[0:C_HALO, :] = jnp.zeros((C_HALO, D_C), F32)

    h = h_ref[...]
    z = _dot(_rms(h, gmix_ref[...]), win_ref[...])

    gate_b = z[:, :O1]
    qext_ref[A_HALO:A_HALO + tm, :] = z[:, O1:O2] * z[:, O2:O3]
    conv_a = jnp.zeros((tm, D_A), F32)
    for k in range(CONV_A):
        off = A_HALO - (CONV_A - 1) + k
        conv_a = conv_a + wca_ref[k:k + 1, :] * qext_ref[off:off + tm, :]
    y_a = gate_b * conv_a
    last_q = qext_ref[A_HALO + tm - (CONV_A - 1):A_HALO + tm, :]
    newa_ref[...] = last_q
    qext_ref[A_HALO - (CONV_A - 1):A_HALO, :] = last_q

    zb = jax.nn.gelu(z[:, O3:O4])
    u = zb[:, :D_B]
    v = _ln(zb[:, D_B:], glnb_ref[...], blnb_ref[...])
    row = lax.broadcasted_iota(jnp.int32, (CHUNK, N_HEADS_B * CHUNK), 0)
    col = lax.broadcasted_iota(jnp.int32, (CHUNK, N_HEADS_B * CHUNK), 1)
    w_tril = jnp.where((col % CHUNK) <= row, wscat_ref[...], 0.0).astype(BF16)
    lane_head = lax.broadcasted_iota(jnp.int32, (CHUNK, D_B), 1) // HEAD_DIM
    s_chunks = []
    for c in range(tm // CHUNK):
        vc = v[c * CHUNK:(c + 1) * CHUNK, :]
        vstack = jnp.concatenate(
            [jnp.where(lane_head == hd, vc, 0.0) for hd in range(N_HEADS_B)], axis=0)
        s_chunks.append(_dot(w_tril, vstack.astype(BF16)) + bsfull_ref[...])
    y_b = u * jnp.concatenate(s_chunks, axis=0)

    gext_ref[C_HALO:C_HALO + tm, :] = z[:, O4:O4 + D_C] * jax.nn.sigmoid(z[:, O4 + D_C:])
    base = C_HALO - (CONV_C - 1)
    for r0 in range(0, tm, CONV_ROWS):
        acc = jnp.zeros((CONV_ROWS, D_C), F32)
        for k in range(CONV_C):
            acc = acc + wcc_ref[k:k + 1, :] * gext_ref[base + r0 + k:base + r0 + k + CONV_ROWS, :]
        conv_ref[r0:r0 + CONV_ROWS, :] = acc
    y_c = jax.nn.silu(_ln(conv_ref[...] + bcc_ref[...], glnc_ref[...], blnc_ref[...]))
    last_g = gext_ref[C_HALO + tm - (CONV_C - 1):C_HALO + tm, :]
    newc_ref[...] = last_g
    gext_ref[C_HALO - (CONV_C - 1):C_HALO, :] = last_g

    hout_ref[...] = _mixer_tail(h, y_a, y_b, y_c, gout_ref, wo_ref)


def _const_spec(shape):
    return pl.BlockSpec(shape, lambda *_: (0,) * len(shape))


def _mix_seq(h, lw):
    n_seq, seq, _ = h.shape
    tm = MIX_ROWS
    small = [lw['g_mix'], lw['w_in'], lw['w_conv_a'], lw['w_s_cat'], lw['b_s_full'],
             lw['g_ln_b'], lw['b_ln_b'], lw['w_conv_c'], lw['b_conv_c'], lw['g_ln_c'],
             lw['b_ln_c'], lw['g_out'], lw['w_o']]
    return pl.pallas_call(
        _mix_seq_kernel,
        grid=(n_seq, seq // tm),
        in_specs=[pl.BlockSpec((None, tm, D_MODEL), lambda s, t: (s, t, 0))]
        + [_const_spec(w.shape) for w in small],
        out_specs=[pl.BlockSpec((None, tm, D_MODEL), lambda s, t: (s, t, 0)),
                   pl.BlockSpec((None, CONV_A - 1, D_A), lambda s, t: (s, 0, 0)),
                   pl.BlockSpec((None, CONV_C - 1, D_C), lambda s, t: (s, 0, 0))],
        out_shape=[jax.ShapeDtypeStruct(h.shape, F32),
                   jax.ShapeDtypeStruct((n_seq, CONV_A - 1, D_A), F32),
                   jax.ShapeDtypeStruct((n_seq, CONV_C - 1, D_C), F32)],
        scratch_shapes=[pltpu.VMEM((A_HALO + tm, D_A), F32),
                        pltpu.VMEM((C_HALO + tm, D_C), F32),
                        pltpu.VMEM((tm, D_C), F32)],
        compiler_params=pltpu.CompilerParams(
            dimension_semantics=("arbitrary", "arbitrary"),
            vmem_limit_bytes=VMEM_LIMIT_BYTES),
        name="mix_seq",
    )(h, *small)


def _mix_row_kernel(h_ref, sa_ref, sc_ref, gmix_ref, win_ref, wca_ref, wsd_ref, bs0_ref,
                    glnb_ref, blnb_ref, wcc_ref, bcc_ref, glnc_ref, blnc_ref, gout_ref,
                    wo_ref, hout_ref, newa_ref, newc_ref, v_ref):
    h = h_ref[...]
    z = _dot(_rms(h, gmix_ref[...]), win_ref[...])

    q = z[:, O1:O2] * z[:, O2:O3]
    conv_a = wca_ref[CONV_A - 1:CONV_A, :] * q
    for k in range(CONV_A - 1):
        conv_a = conv_a + wca_ref[k:k + 1, :] * sa_ref[k]
    y_a = z[:, :O1] * conv_a
    for k in range(CONV_A - 2):
        newa_ref[k] = sa_ref[k + 1]
    newa_ref[CONV_A - 2] = q

    zb = jax.nn.gelu(z[:, O3:O4])
    v = _ln(zb[:, D_B:], glnb_ref[...], blnb_ref[...])
    v_ref[...] = v
    y_b = zb[:, :D_B] * (wsd_ref[...] * v + bs0_ref[...])

    glu = z[:, O4:O4 + D_C] * jax.nn.sigmoid(z[:, O4 + D_C:])
    conv_c = wcc_ref[CONV_C - 1:CONV_C, :] * glu
    for k in range(CONV_C - 1):
        conv_c = conv_c + wcc_ref[k:k + 1, :] * sc_ref[k]
    y_c = jax.nn.silu(_ln(conv_c + bcc_ref[...], glnc_ref[...], blnc_ref[...]))
    for k in range(CONV_C - 2):
        newc_ref[k] = sc_ref[k + 1]
    newc_ref[CONV_C - 2] = glu

    hout_ref[...] = _mixer_tail(h, y_a, y_b, y_c, gout_ref, wo_ref)


def _mix_row(h, state_a, state_c, lw):
    n = h.shape[0]
    args = [h, state_a, state_c, lw['g_mix'], lw['w_in'], lw['w_conv_a'], lw['w_s_diag0'],
            lw['b_s_row0'], lw['g_ln_b'], lw['b_ln_b'], lw['w_conv_c'], lw['b_conv_c'],
            lw['g_ln_c'], lw['b_ln_c'], lw['g_out'], lw['w_o']]
    return pl.pallas_call(
        _mix_row_kernel,
        grid=(1,),
        in_specs=[_const_spec(a.shape) for a in args],
        out_specs=[_const_spec((n, D_MODEL)), _const_spec((CONV_A - 1, n, D_A)),
                   _const_spec((CONV_C - 1, n, D_C)), _const_spec((n, D_B))],
        out_shape=[jax.ShapeDtypeStruct((n, D_MODEL), F32),
                   jax.ShapeDtypeStruct((CONV_A - 1, n, D_A), F32),
                   jax.ShapeDtypeStruct((CONV_C - 1, n, D_C), F32),
                   jax.ShapeDtypeStruct((n, D_B), F32)],
        compiler_params=pltpu.CompilerParams(
            dimension_semantics=("arbitrary",), vmem_limit_bytes=VMEM_LIMIT_BYTES),
        name="mix_row",
    )(*args)


def _ple(h, p, gple_ref, wpg_ref, wpp_ref):
    gate = jax.nn.sigmoid(_dot(_rms(h, gple_ref[...]), wpg_ref[...]))
    return h + gate * _dot(p, wpp_ref[...])


def _ffn_dense_kernel(h_ref, p_ref, gffn_ref, wg_ref, wu_ref, wd_ref, gple_ref, wpg_ref,
                      wpp_ref, out_ref):
    h = h_ref[...]
    xn = _rms(h, gffn_ref[...]).astype(BF16)
    a = jax.nn.silu(_dot(xn, wg_ref[...])) * _dot(xn, wu_ref[...])
    h = h + _dot(a, wd_ref[...])
    out_ref[...] = _ple(h, p_ref[...], gple_ref, wpg_ref, wpp_ref)


def _ffn_dense(h, p, lw, tm):
    n_tok = h.shape[0]
    weights = [lw['g_ffn'], lw['w_ff_gate'], lw['w_ff_up'], lw['w_ff_down'], lw['g_ple'],
               lw['w_ple_gate'], lw['w_ple_proj']]
    return pl.pallas_call(
        _ffn_dense_kernel,
        grid=(pl.cdiv(n_tok, tm),),
        in_specs=[pl.BlockSpec((tm, D_MODEL), lambda i: (i, 0)),
                  pl.BlockSpec((tm, D_PLE), lambda i: (i, 0))]
        + [_const_spec(w.shape) for w in weights],
        out_specs=pl.BlockSpec((tm, D_MODEL), lambda i: (i, 0)),
        out_shape=jax.ShapeDtypeStruct(h.shape, F32),
        compiler_params=pltpu.CompilerParams(
            dimension_semantics=("arbitrary",), vmem_limit_bytes=VMEM_LIMIT_BYTES),
        name="ffn_dense",
    )(h, p, *weights)


def _moe_kernel(h_ref, gffn_ref, wr_ref, wg_ref, wu_ref, wd_ref, out_ref, xn_ref, gates_ref):
    e = pl.program_id(1)

    @pl.when(e == 0)
    def _():
        h = h_ref[...]
        xn = _rms(h, gffn_ref[...])
        xn_ref[...] = xn.astype(BF16)
        logits = jnp.dot(xn, wr_ref[...], preferred_element_type=F32,
                         precision=lax.Precision.HIGHEST)
        lane = lax.broadcasted_iota(jnp.int32, logits.shape, 1)
        m1 = jnp.max(logits, axis=-1, keepdims=True)
        i1 = jnp.min(jnp.where(logits == m1, lane, N_EXPERTS), axis=-1, keepdims=True)
        rest = jnp.where(lane == i1, -jnp.inf, logits)
        m2 = jnp.max(rest, axis=-1, keepdims=True)
        i2 = jnp.min(jnp.where(rest == m2, lane, N_EXPERTS), axis=-1, keepdims=True)
        e2 = jnp.exp(m2 - m1)
        denom = 1.0 + e2
        gates_ref[...] = (jnp.where(lane == i1, 1.0 / denom, 0.0)
                          + jnp.where(lane == i2, e2 / denom, 0.0))
        out_ref[...] = h

    xn = xn_ref[...]
    a = jax.nn.silu(_dot(xn, wg_ref[...])) * _dot(xn, wu_ref[...])
    y = _dot(a, wd_ref[...])
    lane = lax.broadcasted_iota(jnp.int32, gates_ref.shape, 1)
    g = jnp.sum(jnp.where(lane == e, gates_ref[...], 0.0), axis=-1, keepdims=True)
    out_ref[...] += g * y


def _moe(h, lw, tm):
    n_tok = h.shape[0]
    d_exp = lw['w_ex_gate'].shape[-1]
    return pl.pallas_call(
        _moe_kernel,
        grid=(pl.cdiv(n_tok, tm), N_EXPERTS),
        in_specs=[pl.BlockSpec((tm, D_MODEL), lambda i, e: (i, 0)),
                  _const_spec(lw['g_ffn'].shape),
                  _const_spec(lw['w_router'].shape),
                  pl.BlockSpec((None, D_MODEL, d_exp), lambda i, e: (e, 0, 0)),
                  pl.BlockSpec((None, D_MODEL, d_exp), lambda i, e: (e, 0, 0)),
                  pl.BlockSpec((None, d_exp, D_MODEL), lambda i, e: (e, 0, 0))],
        out_specs=pl.BlockSpec((tm, D_MODEL), lambda i, e: (i, 0)),
        out_shape=jax.ShapeDtypeStruct(h.shape, F32),
        scratch_shapes=[pltpu.VMEM((tm, D_MODEL), BF16),
                        pltpu.VMEM((tm, N_EXPERTS), F32)],
        compiler_params=pltpu.CompilerParams(
            dimension_semantics=("arbitrary", "arbitrary"),
            vmem_limit_bytes=VMEM_LIMIT_BYTES),
        name="moe_dense",
    )(h, lw['g_ffn'], lw['w_router'], lw['w_ex_gate'], lw['w_ex_up'], lw['w_ex_down'])


def _ple_kernel(h_ref, p_ref, gple_ref, wpg_ref, wpp_ref, out_ref):
    out_ref[...] = _ple(h_ref[...], p_ref[...], gple_ref, wpg_ref, wpp_ref)


def _ple_call(h, p, lw, tm):
    n_tok = h.shape[0]
    weights = [lw['g_ple'], lw['w_ple_gate'], lw['w_ple_proj']]
    return pl.pallas_call(
        _ple_kernel,
        grid=(pl.cdiv(n_tok, tm),),
        in_specs=[pl.BlockSpec((tm, D_MODEL), lambda i: (i, 0)),
                  pl.BlockSpec((tm, D_PLE), lambda i: (i, 0))]
        + [_const_spec(w.shape) for w in weights],
        out_specs=pl.BlockSpec((tm, D_MODEL), lambda i: (i, 0)),
        out_shape=jax.ShapeDtypeStruct(h.shape, F32),
        compiler_params=pltpu.CompilerParams(
            dimension_semantics=("arbitrary",), vmem_limit_bytes=VMEM_LIMIT_BYTES),
        name="ple",
    )(h, p, *weights)


def _final_norm_kernel(h_ref, g_ref, out_ref):
    out_ref[...] = _rms(h_ref[...], g_ref[...])


def _final_norm(h, g, tm):
    n_tok = h.shape[0]
    return pl.pallas_call(
        _final_norm_kernel,
        grid=(pl.cdiv(n_tok, tm),),
        in_specs=[pl.BlockSpec((tm, D_MODEL), lambda i: (i, 0)), _const_spec(g.shape)],
        out_specs=pl.BlockSpec((tm, D_MODEL), lambda i: (i, 0)),
        out_shape=jax.ShapeDtypeStruct(h.shape, F32),
        name="final_norm",
    )(h, g)


def _row(x):
    return x.reshape(1, -1)


def kernel(x_prompt, x_sample, state_conv_a, state_conv_c, p_prompt, p_sample, g_mix, w_in, w_conv_a, w_s, b_s, g_ln_b, b_ln_b, w_conv_c, b_conv_c, g_ln_c, b_ln_c, g_out, w_o, g_ffn, w_ff_gate, w_ff_up, w_ff_down, w_router, w_ex_gate, w_ex_up, w_ex_down, g_ple, w_ple_gate, w_ple_proj, g_final):
    depth = g_mix.shape[0]
    n_seq, seq, _ = x_prompt.shape
    n_dec = x_sample.shape[0]
    assert x_sample.shape[1] == 1 and seq % MIX_ROWS == 0

    hp = x_prompt
    hs = x_sample.reshape(n_dec, D_MODEL)
    outs = {k: [] for k in ('a_p', 'a_s', 'c_p', 'c_s', 'v_s')}
    for i in range(depth):
        j = i // 2
        lw = {
            'g_mix': _row(g_mix[i]), 'w_in': w_in[i].astype(BF16), 'w_conv_a': w_conv_a[i],
            'w_s_cat': jnp.transpose(w_s[i], (1, 0, 2)).reshape(CHUNK, N_HEADS_B * CHUNK),
            'b_s_full': jnp.repeat(b_s[i].T, HEAD_DIM, axis=1),
            'w_s_diag0': _row(jnp.repeat(w_s[i, :, 0, 0], HEAD_DIM)),
            'b_s_row0': _row(jnp.repeat(b_s[i, :, 0], HEAD_DIM)),
            'g_ln_b': _row(g_ln_b[i]), 'b_ln_b': _row(b_ln_b[i]),
            'w_conv_c': w_conv_c[i], 'b_conv_c': _row(b_conv_c[i]),
            'g_ln_c': _row(g_ln_c[i]), 'b_ln_c': _row(b_ln_c[i]),
            'g_out': _row(g_out[i]), 'w_o': w_o[i].astype(BF16),
            'g_ffn': _row(g_ffn[i]), 'g_ple': _row(g_ple[i]),
            'w_ple_gate': w_ple_gate[i].astype(BF16), 'w_ple_proj': w_ple_proj[i].astype(BF16),
        }
        hp, a_p, c_p = _mix_seq(hp, lw)
        hs, a_s, c_s, v_s = _mix_row(hs, jnp.swapaxes(state_conv_a[i], 0, 1),
                                     jnp.swapaxes(state_conv_c[i], 0, 1), lw)
        outs['a_p'].append(a_p)
        outs['c_p'].append(c_p)
        outs['a_s'].append(jnp.swapaxes(a_s, 0, 1))
        outs['c_s'].append(jnp.swapaxes(c_s, 0, 1))
        outs['v_s'].append(v_s.reshape(n_dec, 1, D_B))

        hp2 = hp.reshape(n_seq * seq, D_MODEL)
        pp = p_prompt[i].reshape(n_seq * seq, D_PLE)
        ps = p_sample[i].reshape(n_dec, D_PLE)
        if i % 2 == 0:
            lw.update({'w_ff_gate': w_ff_gate[j].astype(BF16), 'w_ff_up': w_ff_up[j].astype(BF16),
                       'w_ff_down': w_ff_down[j].astype(BF16)})
            hp2 = _ffn_dense(hp2, pp, lw, 512)
            hs = _ffn_dense(hs, ps, lw, n_dec)
        else:
            lw.update({'w_router': w_router[j], 'w_ex_gate': w_ex_gate[j].astype(BF16),
                       'w_ex_up': w_ex_up[j].astype(BF16),
                       'w_ex_down': w_ex_down[j].astype(BF16)})
            hp2 = _ple_call(_moe(hp2, lw, 512), pp, lw, 512)
            hs = _ple_call(_moe(hs, lw, n_dec), ps, lw, n_dec)
        hp = hp2.reshape(n_seq, seq, D_MODEL)

    gf = _row(g_final)
    y_prompt = _final_norm(hp.reshape(n_seq * seq, D_MODEL), gf, 512).reshape(x_prompt.shape)
    y_sample = _final_norm(hs, gf, n_dec).reshape(x_sample.shape)
    return (y_prompt, y_sample, jnp.stack(outs['a_p']), jnp.stack(outs['a_s']),
            jnp.stack(outs['c_p']), jnp.stack(outs['c_s']), jnp.stack(outs['v_s']))
```

```python
import functools

import jax
import jax.numpy as jnp
from jax import lax
from jax.experimental import pallas as pl
from jax.experimental.pallas import tpu as pltpu
from jax.experimental.pallas import tpu_sc as plsc

D_MODEL = 1024
HEAD_DIM = 64
D_A = 384
D_B = 256
D_C = 384
N_HEADS_B = D_B // HEAD_DIM
CONV_A = 3
CONV_C = 31
CHUNK = 128
D_IN = 3 * D_A + 2 * D_B + 2 * D_C
D_PLE = 256
N_EXPERTS = 8
TOP_K = 2
EPS = 1e-6

O1, O2, O3 = D_A, 2 * D_A, 3 * D_A
O4 = O3 + 2 * D_B

F32 = jnp.float32
BF16 = jnp.bfloat16
I32 = jnp.int32

VMEM_LIMIT_BYTES = 56 * 1024 * 1024

TOK_ROWS = 512
CONV_ROWS = 64
A_HALO = 8
C_HALO = 32
EXPERT_ROWS = 512

SC_CORES = 2
SC_WORKERS = 32
SC_WINDOW = 48


def _rms(x, g):
    return x * lax.rsqrt(jnp.mean(x * x, axis=-1, keepdims=True) + EPS) * g


def _ln(x, g, b):
    mu = jnp.mean(x, axis=-1, keepdims=True)
    xc = x - mu
    var = jnp.mean(xc * xc, axis=-1, keepdims=True)
    return xc * lax.rsqrt(var + EPS) * g + b


def _dot(a, b):
    return jnp.dot(a.astype(BF16), b, preferred_element_type=F32)


def _const_spec(shape):
    return pl.BlockSpec(shape, lambda *_: (0,) * len(shape))


def _tc_params():
    return pltpu.CompilerParams(dimension_semantics=("arbitrary",),
                                vmem_limit_bytes=VMEM_LIMIT_BYTES)


def _mixer_tail(h, y_a, y_b, y_c, gout_ref, wo_ref):
    y = jnp.concatenate([_rms(y_a, gout_ref[:, :D_A]),
                         _rms(y_b, gout_ref[:, D_A:D_A + D_B]),
                         _rms(y_c, gout_ref[:, D_A + D_B:])], axis=-1)
    return h + _dot(y, wo_ref[...])


def _mixer_kernel(hp_ref, hs_ref, sa_ref, sc_ref, gmix_ref, win_ref, wca_ref, wscat_ref,
                  bsfull_ref, glnb_ref, blnb_ref, wcc_ref, bcc_ref, glnc_ref, blnc_ref,
                  gout_ref, wo_ref,
                  hout_ref, newa_p_ref, newc_p_ref, newa_s_ref, newc_s_ref, v_ref,
                  qext_ref, gext_ref, conv_ref, *, tiles_per_seq):
    g = pl.program_id(0)
    n_seq_tiles = pl.num_programs(0) - 1
    tm = hp_ref.shape[0]
    n_dec = hs_ref.shape[0]

    @pl.when(g < n_seq_tiles)
    def _sequence_tile():
        @pl.when(g % tiles_per_seq == 0)
        def _():
            qext_ref[0:A_HALO, :] = jnp.zeros((A_HALO, D_A), F32)
            gext_ref[0:C_HALO, :] = jnp.zeros((C_HALO, D_C), F32)

        h = hp_ref[...]
        z = _dot(_rms(h, gmix_ref[...]), win_ref[...])

        qext_ref[A_HALO:A_HALO + tm, :] = z[:, O1:O2] * z[:, O2:O3]
        conv_a = jnp.zeros((tm, D_A), F32)
        for k in range(CONV_A):
            off = A_HALO - (CONV_A - 1) + k
            conv_a = conv_a + wca_ref[k:k + 1, :] * qext_ref[off:off + tm, :]
        y_a = z[:, :O1] * conv_a
        last_q = qext_ref[A_HALO + tm - (CONV_A - 1):A_HALO + tm, :]
        newa_p_ref[...] = last_q
        qext_ref[A_HALO - (CONV_A - 1):A_HALO, :] = last_q

        zb = jax.nn.gelu(z[:, O3:O4])
        v = _ln(zb[:, D_B:], glnb_ref[...], blnb_ref[...])
        row = lax.broadcasted_iota(I32, (CHUNK, N_HEADS_B * CHUNK), 0)
        col = lax.broadcasted_iota(I32, (CHUNK, N_HEADS_B * CHUNK), 1)
        w_tril = jnp.where((col % CHUNK) <= row, wscat_ref[...], 0.0).astype(BF16)
        lane_head = lax.broadcasted_iota(I32, (CHUNK, D_B), 1) // HEAD_DIM
        s_chunks = []
        for c in range(tm // CHUNK):
            vc = v[c * CHUNK:(c + 1) * CHUNK, :]
            vstack = jnp.concatenate(
                [jnp.where(lane_head == hd, vc, 0.0) for hd in range(N_HEADS_B)], axis=0)
            s_chunks.append(_dot(w_tril, vstack.astype(BF16)) + bsfull_ref[...])
        y_b = zb[:, :D_B] * jnp.concatenate(s_chunks, axis=0)

        gext_ref[C_HALO:C_HALO + tm, :] = (z[:, O4:O4 + D_C]
                                           * jax.nn.sigmoid(z[:, O4 + D_C:]))
        base = C_HALO - (CONV_C - 1)
        for r0 in range(0, tm, CONV_ROWS):
            acc = jnp.zeros((CONV_ROWS, D_C), F32)
            for k in range(CONV_C):
                lo = base + r0 + k
                acc = acc + wcc_ref[k:k + 1, :] * gext_ref[lo:lo + CONV_ROWS, :]
            conv_ref[r0:r0 + CONV_ROWS, :] = acc
        y_c = jax.nn.silu(_ln(conv_ref[...] + bcc_ref[...], glnc_ref[...], blnc_ref[...]))
        last_g = gext_ref[C_HALO + tm - (CONV_C - 1):C_HALO + tm, :]
        newc_p_ref[...] = last_g
        gext_ref[C_HALO - (CONV_C - 1):C_HALO, :] = last_g

        hout_ref[...] = _mixer_tail(h, y_a, y_b, y_c, gout_ref, wo_ref)

    @pl.when(g == n_seq_tiles)
    def _sample_rows():
        h = hs_ref[...]
        z = _dot(_rms(h, gmix_ref[...]), win_ref[...])

        q = z[:, O1:O2] * z[:, O2:O3]
        conv_a = wca_ref[CONV_A - 1:CONV_A, :] * q
        for k in range(CONV_A - 1):
            conv_a = conv_a + wca_ref[k:k + 1, :] * sa_ref[k]
        y_a = z[:, :O1] * conv_a
        for k in range(CONV_A - 2):
            newa_s_ref[k] = sa_ref[k + 1]
        newa_s_ref[CONV_A - 2] = q

        zb = jax.nn.gelu(z[:, O3:O4])
        v = _ln(zb[:, D_B:], glnb_ref[...], blnb_ref[...])
        v_ref[...] = v
        w_diag0 = jnp.concatenate(
            [jnp.broadcast_to(wscat_ref[0:1, hd * CHUNK:hd * CHUNK + 1], (1, HEAD_DIM))
             for hd in range(N_HEADS_B)], axis=-1)
        y_b = zb[:, :D_B] * (w_diag0 * v + bsfull_ref[0:1, :])

        glu = z[:, O4:O4 + D_C] * jax.nn.sigmoid(z[:, O4 + D_C:])
        conv_c = wcc_ref[CONV_C - 1:CONV_C, :] * glu
        for k in range(CONV_C - 1):
            conv_c = conv_c + wcc_ref[k:k + 1, :] * sc_ref[k]
        y_c = jax.nn.silu(_ln(conv_c + bcc_ref[...], glnc_ref[...], blnc_ref[...]))
        for k in range(CONV_C - 2):
            newc_s_ref[k] = sc_ref[k + 1]
        newc_s_ref[CONV_C - 2] = glu

        hout_ref[0:n_dec, :] = _mixer_tail(h, y_a, y_b, y_c, gout_ref, wo_ref)


def _mixer(hp, hs, hs_block, state_a, state_c, lw, n_seq, seq):
    tm = TOK_ROWS
    n_dec = state_a.shape[1]
    tiles_per_seq = seq // tm
    n_seq_tiles = n_seq * tiles_per_seq
    n_tok = n_seq * seq + n_dec
    weights = [lw['g_mix'], lw['w_in'], lw['w_conv_a'], lw['w_s_cat'], lw['b_s_full'],
               lw['g_ln_b'], lw['b_ln_b'], lw['w_conv_c'], lw['b_conv_c'], lw['g_ln_c'],
               lw['b_ln_c'], lw['g_out'], lw['w_o']]
    seq_of = lambda g: jnp.minimum(g // tiles_per_seq, n_seq - 1)
    return pl.pallas_call(
        functools.partial(_mixer_kernel, tiles_per_seq=tiles_per_seq),
        grid=(n_seq_tiles + 1,),
        in_specs=[pl.BlockSpec((tm, D_MODEL), lambda g: (jnp.minimum(g, n_seq_tiles - 1), 0)),
                  pl.BlockSpec((n_dec, D_MODEL), lambda g: (hs_block, 0)),
                  _const_spec(state_a.shape), _const_spec(state_c.shape)]
        + [_const_spec(w.shape) for w in weights],
        out_specs=[pl.BlockSpec((tm, D_MODEL), lambda g: (g, 0)),
                   pl.BlockSpec((None, CONV_A - 1, D_A), lambda g: (seq_of(g), 0, 0)),
                   pl.BlockSpec((None, CONV_C - 1, D_C), lambda g: (seq_of(g), 0, 0)),
                   _const_spec((CONV_A - 1, n_dec, D_A)),
                   _const_spec((CONV_C - 1, n_dec, D_C)),
                   _const_spec((n_dec, D_B))],
        out_shape=[jax.ShapeDtypeStruct((n_tok, D_MODEL), F32),
                   jax.ShapeDtypeStruct((n_seq, CONV_A - 1, D_A), F32),
                   jax.ShapeDtypeStruct((n_seq, CONV_C - 1, D_C), F32),
                   jax.ShapeDtypeStruct((CONV_A - 1, n_dec, D_A), F32),
                   jax.ShapeDtypeStruct((CONV_C - 1, n_dec, D_C), F32),
                   jax.ShapeDtypeStruct((n_dec, D_B), F32)],
        scratch_shapes=[pltpu.VMEM((A_HALO + tm, D_A), F32),
                        pltpu.VMEM((C_HALO + tm, D_C), F32),
                        pltpu.VMEM((tm, D_C), F32)],
        compiler_params=_tc_params(),
        name="mixer",
    )(hp, hs, state_a, state_c, *weights)


def _ple(h, p, gple_ref, wpg_ref, wpp_ref):
    gate = jax.nn.sigmoid(_dot(_rms(h, gple_ref[...]), wpg_ref[...]))
    return h + gate * _dot(p, wpp_ref[...])


def _tile_embedding(pp_ref, ps_ref, pbuf_ref):
    pbuf_ref[...] = pp_ref[...]

    @pl.when(pl.program_id(0) == pl.num_programs(0) - 1)
    def _():
        pbuf_ref[0:ps_ref.shape[0], :] = ps_ref[...]

    return pbuf_ref[...]


def _embedding_specs(n_tiles, tm, n_dec):
    return [pl.BlockSpec((tm, D_PLE), lambda g: (jnp.minimum(g, n_tiles - 2), 0)),
            _const_spec((n_dec, D_PLE))]


def _ffn_dense_kernel(h_ref, pp_ref, ps_ref, gffn_ref, wg_ref, wu_ref, wd_ref, gple_ref,
                      wpg_ref, wpp_ref, out_ref, pbuf_ref):
    h = h_ref[...]
    xn = _rms(h, gffn_ref[...]).astype(BF16)
    a = jax.nn.silu(_dot(xn, wg_ref[...])) * _dot(xn, wu_ref[...])
    h = h + _dot(a, wd_ref[...])
    p = _tile_embedding(pp_ref, ps_ref, pbuf_ref)
    out_ref[...] = _ple(h, p, gple_ref, wpg_ref, wpp_ref)


def _ffn_dense(h, pp, ps, lw):
    n_tok = h.shape[0]
    tm = TOK_ROWS
    n_tiles = pl.cdiv(n_tok, tm)
    weights = [lw['g_ffn'], lw['w_ff_gate'], lw['w_ff_up'], lw['w_ff_down'], lw['g_ple'],
               lw['w_ple_gate'], lw['w_ple_proj']]
    return pl.pallas_call(
        _ffn_dense_kernel,
        grid=(n_tiles,),
        in_specs=[pl.BlockSpec((tm, D_MODEL), lambda g: (g, 0))]
        + _embedding_specs(n_tiles, tm, ps.shape[0])
        + [_const_spec(w.shape) for w in weights],
        out_specs=pl.BlockSpec((tm, D_MODEL), lambda g: (g, 0)),
        out_shape=jax.ShapeDtypeStruct(h.shape, F32),
        scratch_shapes=[pltpu.VMEM((tm, D_PLE), F32)],
        compiler_params=_tc_params(),
        name="ffn_dense",
    )(h, pp, ps, *weights)


def _router_kernel(h_ref, gffn_ref, wr_ref, mi_ref, mf_ref, cnt_ref, carry_ref, *, n_tok):
    g = pl.program_id(0)
    tm = h_ref.shape[0]

    @pl.when(g == 0)
    def _():
        carry_ref[...] = jnp.zeros(carry_ref.shape, F32)

    valid = (g * tm + lax.broadcasted_iota(I32, (tm, 1), 0)) < n_tok
    xn = _rms(jnp.where(valid, h_ref[...], 0.0), gffn_ref[...])
    logits = jnp.dot(xn, wr_ref[...], preferred_element_type=F32,
                     precision=lax.Precision.HIGHEST)
    lane = lax.broadcasted_iota(I32, logits.shape, 1)
    m1 = jnp.max(logits, axis=-1, keepdims=True)
    i1 = jnp.min(jnp.where(logits == m1, lane, N_EXPERTS), axis=-1, keepdims=True)
    rest = jnp.where(lane == i1, -jnp.inf, logits)
    m2 = jnp.max(rest, axis=-1, keepdims=True)
    i2 = jnp.min(jnp.where(rest == m2, lane, N_EXPERTS), axis=-1, keepdims=True)
    e2 = jnp.exp(m2 - m1)
    denom = 1.0 + e2
    w1 = 1.0 / denom
    w2 = e2 / denom

    oh1 = jnp.where((lane == i1) & valid, 1.0, 0.0)
    oh2 = jnp.where((lane == i2) & valid, 1.0, 0.0)
    member = oh1 + oh2
    r = lax.broadcasted_iota(I32, (tm, tm), 0)
    c = lax.broadcasted_iota(I32, (tm, tm), 1)
    before = jnp.where(c < r, 1.0, 0.0).astype(BF16)
    pos = _dot(before, member.astype(BF16)) + carry_ref[...]
    pos1 = jnp.sum(oh1 * pos, axis=-1, keepdims=True).astype(I32)
    pos2 = jnp.sum(oh2 * pos, axis=-1, keepdims=True).astype(I32)
    carry_ref[...] = carry_ref[...] + jnp.sum(member, axis=0, keepdims=True)
    cnt_ref[...] = carry_ref[...]

    mi_ref[...] = jnp.where(lane == 0, i1, jnp.where(lane == 1, i2,
                            jnp.where(lane == 2, pos1, jnp.where(lane == 3, pos2, 0))))
    mf_ref[...] = jnp.where(lane == 0, w1, jnp.where(lane == 1, w2, 0.0))


def _router(h, lw):
    n_tok = h.shape[0]
    tm = TOK_ROWS
    return pl.pallas_call(
        functools.partial(_router_kernel, n_tok=n_tok),
        grid=(pl.cdiv(n_tok, tm),),
        in_specs=[pl.BlockSpec((tm, D_MODEL), lambda g: (g, 0)),
                  _const_spec(lw['g_ffn'].shape), _const_spec(lw['w_router'].shape)],
        out_specs=[pl.BlockSpec((tm, N_EXPERTS), lambda g: (g, 0)),
                   pl.BlockSpec((tm, N_EXPERTS), lambda g: (g, 0)),
                   _const_spec((1, N_EXPERTS))],
        out_shape=[jax.ShapeDtypeStruct((n_tok, N_EXPERTS), I32),
                   jax.ShapeDtypeStruct((n_tok, N_EXPERTS), F32),
                   jax.ShapeDtypeStruct((1, N_EXPERTS), F32)],
        scratch_shapes=[pltpu.VMEM((1, N_EXPERTS), F32)],
        compiler_params=_tc_params(),
        name="router",
    )(h, lw['g_ffn'], lw['w_router'])


def _sc_chunk(n_rows):
    per_worker = pl.cdiv(n_rows, SC_WORKERS)
    return pl.cdiv(per_worker, SC_WINDOW) * SC_WINDOW


def _sc_worker_base(n_rows, chunk):
    wid = lax.axis_index("s") * SC_CORES + lax.axis_index("c")
    return jnp.minimum(wid * chunk, n_rows - chunk)


def _sc_scatter_rows(x, dest, n_out):
    n = x.shape[0]
    chunk = _sc_chunk(n)
    mesh = plsc.VectorSubcoreMesh(core_axis_name="c", subcore_axis_name="s")

    @functools.partial(
        pl.kernel, mesh=mesh,
        out_type=jax.ShapeDtypeStruct((n_out, D_MODEL), x.dtype),
        scratch_types=[pltpu.VMEM((SC_WINDOW,), I32) for _ in range(TOP_K)]
        + [pltpu.VMEM((SC_WINDOW, D_MODEL), x.dtype), pltpu.SemaphoreType.DMA],
        name="sc_scatter_rows",
    )
    def scatter(x_hbm, dest_hbm, out_hbm, idx0_v, idx1_v, rows_v, sem):
        base = _sc_worker_base(n, chunk)

        @pl.loop(0, chunk // SC_WINDOW)
        def _(j):
            off = pl.multiple_of(base + j * SC_WINDOW, 8)
            pltpu.sync_copy(dest_hbm.at[pl.ds(off, SC_WINDOW)], idx0_v)
            pltpu.sync_copy(dest_hbm.at[pl.ds(n + off, SC_WINDOW)], idx1_v)
            pltpu.sync_copy(x_hbm.at[pl.ds(off, SC_WINDOW)], rows_v)
            first = pltpu.async_copy(rows_v, out_hbm.at[idx0_v], sem)
            second = pltpu.async_copy(rows_v, out_hbm.at[idx1_v], sem)
            first.wait()
            second.wait()

    return scatter(x, dest)


def _sc_gather_rows(y, idx):
    n = idx.shape[0]
    chunk = _sc_chunk(n)
    mesh = plsc.VectorSubcoreMesh(core_axis_name="c", subcore_axis_name="s")

    @functools.partial(
        pl.kernel, mesh=mesh,
        out_type=jax.ShapeDtypeStruct((n, D_MODEL), y.dtype),
        scratch_types=[pltpu.VMEM((SC_WINDOW,), I32),
                       pltpu.VMEM((SC_WINDOW, D_MODEL), y.dtype), pltpu.SemaphoreType.DMA],
        name="sc_gather_rows",
    )
    def gather(y_hbm, idx_hbm, out_hbm, idx_v, rows_v, sem):
        base = _sc_worker_base(n, chunk)

        @pl.loop(0, chunk // SC_WINDOW)
        def _(j):
            off = pl.multiple_of(base + j * SC_WINDOW, 8)
            pltpu.sync_copy(idx_hbm.at[pl.ds(off, SC_WINDOW)], idx_v)
            pltpu.async_copy(y_hbm.at[idx_v], rows_v, sem).wait()
            pltpu.sync_copy(rows_v, out_hbm.at[pl.ds(off, SC_WINDOW)])

    return gather(y, idx)


def _expert_kernel(tile_expert_ref, n_valid_ref, xs_ref, gffn_ref, wg_ref, wu_ref, wd_ref,
                   y_ref):
    @pl.when(pl.program_id(0) < n_valid_ref[0])
    def _():
        xn = _rms(xs_ref[...], gffn_ref[...]).astype(BF16)
        a = jax.nn.silu(_dot(xn, wg_ref[...])) * _dot(xn, wu_ref[...])
        y_ref[...] = _dot(a, wd_ref[...])


def _experts(xs, tile_expert, n_valid, lw):
    n_slots = xs.shape[0]
    te = EXPERT_ROWS
    d_exp = lw['w_ex_gate'].shape[-1]
    row_block = lambda g, tex, nv: (jnp.minimum(g, nv[0] - 1), 0)
    w_block = lambda g, tex, nv: (tex[g], 0, 0)
    return pl.pallas_call(
        _expert_kernel,
        grid_spec=pltpu.PrefetchScalarGridSpec(
            num_scalar_prefetch=2,
            grid=(n_slots // te,),
            in_specs=[pl.BlockSpec((te, D_MODEL), row_block),
                      pl.BlockSpec(lw['g_ffn'].shape, lambda g, tex, nv: (0, 0)),
                      pl.BlockSpec((None, D_MODEL, d_exp), w_block),
                      pl.BlockSpec((None, D_MODEL, d_exp), w_block),
                      pl.BlockSpec((None, d_exp, D_MODEL), w_block)],
            out_specs=pl.BlockSpec((te, D_MODEL), row_block)),
        out_shape=jax.ShapeDtypeStruct((n_slots, D_MODEL), F32),
        compiler_params=_tc_params(),
        name="experts",
    )(tile_expert, n_valid, xs, lw['g_ffn'], lw['w_ex_gate'], lw['w_ex_up'], lw['w_ex_down'])


def _combine_kernel(h_ref, yg_ref, mf_ref, pp_ref, ps_ref, gple_ref, wpg_ref, wpp_ref,
                    gfin_ref, yp_ref, ys_ref, pbuf_ref):
    g = pl.program_id(0)
    last = pl.num_programs(0) - 1
    gates = mf_ref[...]
    h = h_ref[...] + (gates[:, 0:1] * yg_ref[0] + gates[:, 1:2] * yg_ref[1])
    p = _tile_embedding(pp_ref, ps_ref, pbuf_ref)
    out = _rms(_ple(h, p, gple_ref, wpg_ref, wpp_ref), gfin_ref[...])

    @pl.when(g < last)
    def _():
        yp_ref[...] = out

    @pl.when(g == last)
    def _():
        ys_ref[...] = out[0:ys_ref.shape[0], :]


def _combine(h, yg, mf, pp, ps, lw, g_final):
    n_tok = h.shape[0]
    tm = TOK_ROWS
    n_tiles = pl.cdiv(n_tok, tm)
    n_dec = ps.shape[0]
    weights = [lw['g_ple'], lw['w_ple_gate'], lw['w_ple_proj'], g_final]
    return pl.pallas_call(
        _combine_kernel,
        grid=(n_tiles,),
        in_specs=[pl.BlockSpec((tm, D_MODEL), lambda g: (g, 0)),
                  pl.BlockSpec((TOP_K, tm, D_MODEL), lambda g: (0, g, 0)),
                  pl.BlockSpec((tm, N_EXPERTS), lambda g: (g, 0))]
        + _embedding_specs(n_tiles, tm, n_dec)
        + [_const_spec(w.shape) for w in weights],
        out_specs=[pl.BlockSpec((tm, D_MODEL), lambda g: (jnp.minimum(g, n_tiles - 2), 0)),
                   _const_spec((n_dec, D_MODEL))],
        out_shape=[jax.ShapeDtypeStruct((n_tok - n_dec, D_MODEL), F32),
                   jax.ShapeDtypeStruct((n_dec, D_MODEL), F32)],
        scratch_shapes=[pltpu.VMEM((tm, D_PLE), F32)],
        compiler_params=_tc_params(),
        name="combine",
    )(h, yg, mf, pp, ps, *weights)


def _moe_layer(h, pp, ps, lw, g_final):
    n_tok = h.shape[0]
    te = EXPERT_ROWS
    mi, mf, counts = _router(h, lw)

    cnt = counts[0].astype(I32)
    padded = (cnt + te - 1) // te * te
    ends = jnp.cumsum(padded)
    starts = ends - padded
    experts = jnp.arange(N_EXPERTS, dtype=I32)
    start_of = lambda e: jnp.sum(jnp.where(e[:, None] == experts, starts, 0), axis=-1)
    dest = jnp.concatenate([start_of(mi[:, 0]) + mi[:, 2], start_of(mi[:, 1]) + mi[:, 3]])
    n_tiles = pl.cdiv(TOP_K * n_tok + N_EXPERTS * (te - 1), te)
    tile_start = jnp.arange(n_tiles, dtype=I32) * te
    last_used = jnp.max(jnp.where(padded > 0, experts, 0))
    tile_expert = jnp.minimum(
        jnp.sum(tile_start[:, None] >= ends[None, :], axis=-1).astype(I32), last_used)
    n_valid = (ends[-1:] // te).astype(I32)

    xs = _sc_scatter_rows(h, dest, n_tiles * te)
    y = _experts(xs, tile_expert, n_valid, lw)
    yg = _sc_gather_rows(y, dest).reshape(TOP_K, n_tok, D_MODEL)
    return _combine(h, yg, mf, pp, ps, lw, g_final)


def _row(x):
    return x.reshape(1, -1)


def kernel(x_prompt, x_sample, state_conv_a, state_conv_c, p_prompt, p_sample, g_mix, w_in, w_conv_a, w_s, b_s, g_ln_b, b_ln_b, w_conv_c, b_conv_c, g_ln_c, b_ln_c, g_out, w_o, g_ffn, w_ff_gate, w_ff_up, w_ff_down, w_router, w_ex_gate, w_ex_up, w_ex_down, g_ple, w_ple_gate, w_ple_proj, g_final):
    depth = g_mix.shape[0]
    n_seq, seq, _ = x_prompt.shape
    n_dec = x_sample.shape[0]
    n_prompt = n_seq * seq
    assert depth == 2 and x_sample.shape[1] == 1
    assert seq % TOK_ROWS == 0 and n_prompt % n_dec == 0 and TOK_ROWS % n_dec == 0

    hp = x_prompt.reshape(n_prompt, D_MODEL)
    hs = x_sample.reshape(n_dec, D_MODEL)
    hs_block = 0
    outs = {k: [] for k in ('a_p', 'a_s', 'c_p', 'c_s', 'v_s')}
    for i in range(depth):
        j = i // 2
        lw = {
            'g_mix': _row(g_mix[i]), 'w_in': w_in[i].astype(BF16), 'w_conv_a': w_conv_a[i],
            'w_s_cat': jnp.transpose(w_s[i], (1, 0, 2)).reshape(CHUNK, N_HEADS_B * CHUNK),
            'b_s_full': jnp.repeat(b_s[i].T, HEAD_DIM, axis=1),
            'g_ln_b': _row(g_ln_b[i]), 'b_ln_b': _row(b_ln_b[i]),
            'w_conv_c': w_conv_c[i], 'b_conv_c': _row(b_conv_c[i]),
            'g_ln_c': _row(g_ln_c[i]), 'b_ln_c': _row(b_ln_c[i]),
            'g_out': _row(g_out[i]), 'w_o': w_o[i].astype(BF16),
            'g_ffn': _row(g_ffn[i]), 'g_ple': _row(g_ple[i]),
            'w_ple_gate': w_ple_gate[i].astype(BF16), 'w_ple_proj': w_ple_proj[i].astype(BF16),
        }
        h, a_p, c_p, a_s, c_s, v_s = _mixer(
            hp, hs, hs_block, jnp.swapaxes(state_conv_a[i], 0, 1),
            jnp.swapaxes(state_conv_c[i], 0, 1), lw, n_seq, seq)
        outs['a_p'].append(a_p)
        outs['c_p'].append(c_p)
        outs['a_s'].append(jnp.swapaxes(a_s, 0, 1))
        outs['c_s'].append(jnp.swapaxes(c_s, 0, 1))
        outs['v_s'].append(v_s.reshape(n_dec, 1, D_B))

        pp = p_prompt[i].reshape(n_prompt, D_PLE)
        ps = p_sample[i].reshape(n_dec, D_PLE)
        if i % 2 == 0:
            lw.update({'w_ff_gate': w_ff_gate[j].astype(BF16), 'w_ff_up': w_ff_up[j].astype(BF16),
                       'w_ff_down': w_ff_down[j].astype(BF16)})
            h = _ffn_dense(h, pp, ps, lw)
            hp, hs, hs_block = h, h, n_prompt // n_dec
        else:
            lw.update({'w_router': w_router[j], 'w_ex_gate': w_ex_gate[j].astype(BF16),
                       'w_ex_up': w_ex_up[j].astype(BF16),
                       'w_ex_down': w_ex_down[j].astype(BF16)})
            y_prompt, y_sample = _moe_layer(h, pp, ps, lw, _row(g_final))

    return (y_prompt.reshape(x_prompt.shape), y_sample.reshape(x_sample.shape),
            jnp.stack(outs['a_p']), jnp.stack(outs['a_s']),
            jnp.stack(outs['c_p']), jnp.stack(outs['c_s']), jnp.stack(outs['v_s']))
```

```python
import functools

import jax
import jax.numpy as jnp
from jax import lax
from jax.experimental import pallas as pl
from jax.experimental.pallas import tpu as pltpu
from jax.experimental.pallas import tpu_sc as plsc

D_MODEL = 1024
HEAD_DIM = 64
D_A = 384
D_B = 256
D_C = 384
N_HEADS_B = D_B // HEAD_DIM
CONV_A = 3
CONV_C = 31
CHUNK = 128
D_IN = 3 * D_A + 2 * D_B + 2 * D_C
D_PLE = 256
N_EXPERTS = 8
TOP_K = 2
EPS = 1e-6

O1, O2, O3 = D_A, 2 * D_A, 3 * D_A
O4 = O3 + 2 * D_B

F32 = jnp.float32
BF16 = jnp.bfloat16
I32 = jnp.int32

VMEM_LIMIT_BYTES = 56 * 1024 * 1024
SUBLANES = 8

TOK_ROWS = 512
CONV_ROWS = 64
A_HALO = 8
C_HALO = 32
EXPERT_ROWS = 512
FF_COLS = 256

SC_CORES = 2
SC_WORKERS = 32
SC_WINDOW = 48


def _rms(x, g):
    return x * lax.rsqrt(jnp.mean(x * x, axis=-1, keepdims=True) + EPS) * g


def _ln(x, g, b):
    mu = jnp.mean(x, axis=-1, keepdims=True)
    xc = x - mu
    var = jnp.mean(xc * xc, axis=-1, keepdims=True)
    return xc * lax.rsqrt(var + EPS) * g + b


def _dot(a, b):
    return jnp.dot(a.astype(BF16), b, preferred_element_type=F32)


def _dot_full(a, b):
    return jnp.dot(a, b, preferred_element_type=F32, precision=lax.Precision.HIGHEST)


def _const_spec(shape):
    return pl.BlockSpec(shape, lambda *_: (0,) * len(shape), pipeline_mode=pl.Buffered(1))


def _whole_out_spec(shape):
    return pl.BlockSpec(shape, lambda *_: (0,) * len(shape))


def _tc_params():
    return pltpu.CompilerParams(dimension_semantics=("arbitrary",),
                                vmem_limit_bytes=VMEM_LIMIT_BYTES)


def _mixer_tail(h, y_a, y_b, y_c, gout_ref, wo_ref, dot):
    y = jnp.concatenate([_rms(y_a, gout_ref[:, :D_A]),
                         _rms(y_b, gout_ref[:, D_A:D_A + D_B]),
                         _rms(y_c, gout_ref[:, D_A + D_B:])], axis=-1)
    return h + dot(y, wo_ref[...])


def _mixer_seq_kernel(hp_ref, hs_ref, gmix_ref, win_ref, wca_ref, wscat_ref, bsfull_ref,
                      glnb_ref, blnb_ref, wcc_ref, bcc_ref, glnc_ref, blnc_ref, gout_ref,
                      wo_ref, hout_ref, newa_ref, newc_ref,
                      qext_ref, gext_ref, gshift_ref, conv_ref, *, tiles_per_seq, n_seq_tiles):
    g = pl.program_id(0)
    tm = hp_ref.shape[0]

    @pl.when(g < n_seq_tiles)
    def _sequence_tile():
        @pl.when(g % tiles_per_seq == 0)
        def _():
            qext_ref[0:A_HALO, :] = jnp.zeros((A_HALO, D_A), F32)
            gext_ref[0:C_HALO, :] = jnp.zeros((C_HALO, D_C), F32)

        h = hp_ref[...]
        z = _dot(_rms(h, gmix_ref[...]), win_ref[...])

        qext_ref[A_HALO:A_HALO + tm, :] = z[:, O1:O2] * z[:, O2:O3]
        conv_a = jnp.zeros((tm, D_A), F32)
        for k in range(CONV_A):
            off = A_HALO - (CONV_A - 1) + k
            conv_a = conv_a + wca_ref[k:k + 1, :] * qext_ref[off:off + tm, :]
        y_a = z[:, :O1] * conv_a
        last_q = qext_ref[A_HALO + tm - (CONV_A - 1):A_HALO + tm, :]
        newa_ref[...] = last_q
        qext_ref[A_HALO - (CONV_A - 1):A_HALO, :] = last_q

        zb = jax.nn.gelu(z[:, O3:O4])
        v = _ln(zb[:, D_B:], glnb_ref[...], blnb_ref[...])
        row = lax.broadcasted_iota(I32, (CHUNK, N_HEADS_B * CHUNK), 0)
        col = lax.broadcasted_iota(I32, (CHUNK, N_HEADS_B * CHUNK), 1)
        w_tril = jnp.where((col % CHUNK) <= row, wscat_ref[...], 0.0).astype(BF16)
        lane_head = lax.broadcasted_iota(I32, (CHUNK, D_B), 1) // HEAD_DIM
        s_chunks = []
        for c in range(tm // CHUNK):
            vc = v[c * CHUNK:(c + 1) * CHUNK, :]
            vstack = jnp.concatenate(
                [jnp.where(lane_head == hd, vc, 0.0) for hd in range(N_HEADS_B)], axis=0)
            s_chunks.append(_dot(w_tril, vstack.astype(BF16)) + bsfull_ref[...])
        y_b = zb[:, :D_B] * jnp.concatenate(s_chunks, axis=0)

        gext_ref[C_HALO:C_HALO + tm, :] = (z[:, O4:O4 + D_C]
                                           * jax.nn.sigmoid(z[:, O4 + D_C:]))
        n_shift = gshift_ref.shape[1]
        for s in range(1, SUBLANES):
            gshift_ref[s - 1] = gext_ref[s:s + n_shift, :]
        base = C_HALO - (CONV_C - 1)
        for r0 in range(0, tm, CONV_ROWS):
            acc = jnp.zeros((CONV_ROWS, D_C), F32)
            for k in range(CONV_C):
                lo = (base + k) // SUBLANES * SUBLANES + r0
                s = (base + k) % SUBLANES
                window = (gext_ref[lo:lo + CONV_ROWS, :] if s == 0
                          else gshift_ref[s - 1, lo:lo + CONV_ROWS, :])
                acc = acc + wcc_ref[k:k + 1, :] * window
            conv_ref[r0:r0 + CONV_ROWS, :] = acc
        y_c = jax.nn.silu(_ln(conv_ref[...] + bcc_ref[...], glnc_ref[...], blnc_ref[...]))
        last_g = gext_ref[C_HALO + tm - (CONV_C - 1):C_HALO + tm, :]
        newc_ref[...] = last_g
        gext_ref[C_HALO - (CONV_C - 1):C_HALO, :] = last_g

        hout_ref[...] = _mixer_tail(h, y_a, y_b, y_c, gout_ref, wo_ref, _dot)

    @pl.when(g == n_seq_tiles)
    def _append_sample_rows():
        hout_ref[0:hs_ref.shape[0], :] = hs_ref[...]


def _mixer_seq(hp, hs, lw, n_seq, seq):
    tm = TOK_ROWS
    tiles_per_seq = seq // tm
    n_seq_tiles = n_seq * tiles_per_seq
    join = hs is not None
    if not join:
        hs = jnp.zeros((SUBLANES, D_MODEL), F32)
    n_out = n_seq * seq + (hs.shape[0] if join else 0)
    weights = [lw['g_mix'], lw['w_in'], lw['w_conv_a'], lw['w_s_cat'], lw['b_s_full'],
               lw['g_ln_b'], lw['b_ln_b'], lw['w_conv_c'], lw['b_conv_c'], lw['g_ln_c'],
               lw['b_ln_c'], lw['g_out'], lw['w_o']]
    seq_of = lambda g: jnp.minimum(g // tiles_per_seq, n_seq - 1)
    n_shift = tm + C_HALO - SUBLANES
    return pl.pallas_call(
        functools.partial(_mixer_seq_kernel, tiles_per_seq=tiles_per_seq,
                          n_seq_tiles=n_seq_tiles),
        grid=(n_seq_tiles + int(join),),
        in_specs=[pl.BlockSpec((tm, D_MODEL), lambda g: (jnp.minimum(g, n_seq_tiles - 1), 0)),
                  _const_spec(hs.shape)]
        + [_const_spec(w.shape) for w in weights],
        out_specs=[pl.BlockSpec((tm, D_MODEL), lambda g: (g, 0)),
                   pl.BlockSpec((None, CONV_A - 1, D_A), lambda g: (seq_of(g), 0, 0)),
                   pl.BlockSpec((None, CONV_C - 1, D_C), lambda g: (seq_of(g), 0, 0))],
        out_shape=[jax.ShapeDtypeStruct((n_out, D_MODEL), F32),
                   jax.ShapeDtypeStruct((n_seq, CONV_A - 1, D_A), F32),
                   jax.ShapeDtypeStruct((n_seq, CONV_C - 1, D_C), F32)],
        scratch_shapes=[pltpu.VMEM((A_HALO + tm, D_A), F32),
                        pltpu.VMEM((C_HALO + tm, D_C), F32),
                        pltpu.VMEM((SUBLANES - 1, n_shift, D_C), F32),
                        pltpu.VMEM((tm, D_C), F32)],
        compiler_params=_tc_params(),
        name="mixer_seq",
    )(hp, hs, *weights)


def _mixer_row_kernel(h_ref, sa_ref, sc_ref, gmix_ref, win_ref, wca_ref, wscat_ref,
                      bsfull_ref, glnb_ref, blnb_ref, wcc_ref, bcc_ref, glnc_ref, blnc_ref,
                      gout_ref, wo_ref, hout_ref, newa_ref, newc_ref, v_ref):
    h = h_ref[...]
    z = _dot_full(_rms(h, gmix_ref[...]), win_ref[...])

    q = z[:, O1:O2] * z[:, O2:O3]
    conv_a = wca_ref[CONV_A - 1:CONV_A, :] * q
    for k in range(CONV_A - 1):
        conv_a = conv_a + wca_ref[k:k + 1, :] * sa_ref[k]
    y_a = z[:, :O1] * conv_a
    for k in range(CONV_A - 2):
        newa_ref[k] = sa_ref[k + 1]
    newa_ref[CONV_A - 2] = q

    zb = jax.nn.gelu(z[:, O3:O4])
    v = _ln(zb[:, D_B:], glnb_ref[...], blnb_ref[...])
    v_ref[...] = v
    w_diag0 = jnp.concatenate(
        [jnp.broadcast_to(wscat_ref[0:1, hd * CHUNK:hd * CHUNK + 1], (1, HEAD_DIM))
         for hd in range(N_HEADS_B)], axis=-1)
    y_b = zb[:, :D_B] * (w_diag0 * v + bsfull_ref[0:1, :])

    glu = z[:, O4:O4 + D_C] * jax.nn.sigmoid(z[:, O4 + D_C:])
    conv_c = wcc_ref[CONV_C - 1:CONV_C, :] * glu
    for k in range(CONV_C - 1):
        conv_c = conv_c + wcc_ref[k:k + 1, :] * sc_ref[k]
    y_c = jax.nn.silu(_ln(conv_c + bcc_ref[...], glnc_ref[...], blnc_ref[...]))
    for k in range(CONV_C - 2):
        newc_ref[k] = sc_ref[k + 1]
    newc_ref[CONV_C - 2] = glu

    hout_ref[...] = _mixer_tail(h, y_a, y_b, y_c, gout_ref, wo_ref, _dot_full)


def _mixer_row(h, h_block, state_a, state_c, lw):
    n_dec = state_a.shape[1]
    weights = [lw['g_mix'], lw['w_in_f32'], lw['w_conv_a'], lw['w_s_cat'], lw['b_s_full'],
               lw['g_ln_b'], lw['b_ln_b'], lw['w_conv_c'], lw['b_conv_c'], lw['g_ln_c'],
               lw['b_ln_c'], lw['g_out'], lw['w_o_f32']]
    return pl.pallas_call(
        _mixer_row_kernel,
        grid=(1,),
        in_specs=[pl.BlockSpec((n_dec, D_MODEL), lambda g: (h_block, 0)),
                  _const_spec(state_a.shape), _const_spec(state_c.shape)]
        + [_const_spec(w.shape) for w in weights],
        out_specs=[_whole_out_spec((n_dec, D_MODEL)), _whole_out_spec((CONV_A - 1, n_dec, D_A)),
                   _whole_out_spec((CONV_C - 1, n_dec, D_C)), _whole_out_spec((n_dec, D_B))],
        out_shape=[jax.ShapeDtypeStruct((n_dec, D_MODEL), F32),
                   jax.ShapeDtypeStruct((CONV_A - 1, n_dec, D_A), F32),
                   jax.ShapeDtypeStruct((CONV_C - 1, n_dec, D_C), F32),
                   jax.ShapeDtypeStruct((n_dec, D_B), F32)],
        compiler_params=_tc_params(),
        name="mixer_row",
    )(h, state_a, state_c, *weights)


def _ple(h, p, gple_ref, wpg_ref, wpp_ref, dot):
    gate = jax.nn.sigmoid(dot(_rms(h, gple_ref[...]), wpg_ref[...]))
    return h + gate * dot(p, wpp_ref[...])


def _ffn_dense_kernel(h_ref, p_ref, gffn_ref, wg_ref, wu_ref, wd_ref, gple_ref, wpg_ref,
                      wpp_ref, out_ref):
    h = h_ref[...]
    xn = _rms(h, gffn_ref[...]).astype(BF16)
    a = jax.nn.silu(_dot(xn, wg_ref[...])) * _dot(xn, wu_ref[...])
    h = h + _dot(a, wd_ref[...])
    out_ref[...] = _ple(h, p_ref[...], gple_ref, wpg_ref, wpp_ref, _dot)


def _ffn_dense(h, p, lw):
    n_tok = h.shape[0]
    tm = TOK_ROWS
    weights = [lw['g_ffn'], lw['w_ff_gate'], lw['w_ff_up'], lw['w_ff_down'], lw['g_ple'],
               lw['w_ple_gate'], lw['w_ple_proj']]
    return pl.pallas_call(
        _ffn_dense_kernel,
        grid=(n_tok // tm,),
        in_specs=[pl.BlockSpec((tm, D_MODEL), lambda g: (g, 0)),
                  pl.BlockSpec((tm, D_PLE), lambda g: (g, 0))]
        + [_const_spec(w.shape) for w in weights],
        out_specs=pl.BlockSpec((tm, D_MODEL), lambda g: (g, 0)),
        out_shape=jax.ShapeDtypeStruct(h.shape, F32),
        compiler_params=_tc_params(),
        name="ffn_dense",
    )(h, p, *weights)


def _ffn_row_kernel(h_ref, p_ref, gffn_ref, wg_ref, wu_ref, wd_ref, gple_ref, wpg_ref,
                    wpp_ref, out_ref, xn_ref, acc_ref):
    j = pl.program_id(0)

    @pl.when(j == 0)
    def _():
        h = h_ref[...]
        xn_ref[...] = _rms(h, gffn_ref[...])
        acc_ref[...] = h

    xn = xn_ref[...]
    a = jax.nn.silu(_dot_full(xn, wg_ref[...])) * _dot_full(xn, wu_ref[...])
    acc_ref[...] += _dot_full(a, wd_ref[...])

    @pl.when(j == pl.num_programs(0) - 1)
    def _():
        out_ref[...] = _ple(acc_ref[...], p_ref[...], gple_ref, wpg_ref, wpp_ref, _dot_full)


def _ffn_row(h, p, lw):
    n_dec = h.shape[0]
    d_ff = lw['w_ff_gate_f32'].shape[-1]
    return pl.pallas_call(
        _ffn_row_kernel,
        grid=(d_ff // FF_COLS,),
        in_specs=[_const_spec(h.shape), _const_spec(p.shape), _const_spec(lw['g_ffn'].shape),
                  pl.BlockSpec((D_MODEL, FF_COLS), lambda j: (0, j)),
                  pl.BlockSpec((D_MODEL, FF_COLS), lambda j: (0, j)),
                  pl.BlockSpec((FF_COLS, D_MODEL), lambda j: (j, 0)),
                  _const_spec(lw['g_ple'].shape), _const_spec(lw['w_ple_gate_f32'].shape),
                  _const_spec(lw['w_ple_proj_f32'].shape)],
        out_specs=_whole_out_spec(h.shape),
        out_shape=jax.ShapeDtypeStruct(h.shape, F32),
        scratch_shapes=[pltpu.VMEM((n_dec, D_MODEL), F32), pltpu.VMEM((n_dec, D_MODEL), F32)],
        compiler_params=_tc_params(),
        name="ffn_row",
    )(h, p, lw['g_ffn'], lw['w_ff_gate_f32'], lw['w_ff_up_f32'], lw['w_ff_down_f32'],
      lw['g_ple'], lw['w_ple_gate_f32'], lw['w_ple_proj_f32'])


def _router_kernel(h_ref, gffn_ref, wr_ref, mi_ref, mf_ref, cnt_ref, carry_ref, *, n_tok):
    g = pl.program_id(0)
    tm = h_ref.shape[0]

    @pl.when(g == 0)
    def _():
        carry_ref[...] = jnp.zeros(carry_ref.shape, F32)

    valid = (g * tm + lax.broadcasted_iota(I32, (tm, 1), 0)) < n_tok
    xn = _rms(jnp.where(valid, h_ref[...], 0.0), gffn_ref[...])
    logits = _dot_full(xn, wr_ref[...])
    lane = lax.broadcasted_iota(I32, logits.shape, 1)
    m1 = jnp.max(logits, axis=-1, keepdims=True)
    i1 = jnp.min(jnp.where(logits == m1, lane, N_EXPERTS), axis=-1, keepdims=True)
    rest = jnp.where(lane == i1, -jnp.inf, logits)
    m2 = jnp.max(rest, axis=-1, keepdims=True)
    i2 = jnp.min(jnp.where(rest == m2, lane, N_EXPERTS), axis=-1, keepdims=True)
    e2 = jnp.exp(m2 - m1)
    denom = 1.0 + e2
    w1 = 1.0 / denom
    w2 = e2 / denom

    oh1 = jnp.where((lane == i1) & valid, 1.0, 0.0)
    oh2 = jnp.where((lane == i2) & valid, 1.0, 0.0)
    member = oh1 + oh2
    r = lax.broadcasted_iota(I32, (tm, tm), 0)
    c = lax.broadcasted_iota(I32, (tm, tm), 1)
    before = jnp.where(c < r, 1.0, 0.0).astype(BF16)
    pos = _dot(before, member.astype(BF16)) + carry_ref[...]
    pos1 = jnp.sum(oh1 * pos, axis=-1, keepdims=True).astype(I32)
    pos2 = jnp.sum(oh2 * pos, axis=-1, keepdims=True).astype(I32)
    carry_ref[...] = carry_ref[...] + jnp.sum(member, axis=0, keepdims=True)
    cnt_ref[...] = carry_ref[...]

    mi_ref[...] = jnp.where(lane == 0, i1, jnp.where(lane == 1, i2,
                            jnp.where(lane == 2, pos1, jnp.where(lane == 3, pos2, 0))))
    mf_ref[...] = jnp.where(lane == 0, w1, jnp.where(lane == 1, w2, 0.0))


def _router(h, lw):
    n_tok = h.shape[0]
    tm = TOK_ROWS
    return pl.pallas_call(
        functools.partial(_router_kernel, n_tok=n_tok),
        grid=(pl.cdiv(n_tok, tm),),
        in_specs=[pl.BlockSpec((tm, D_MODEL), lambda g: (g, 0)),
                  _const_spec(lw['g_ffn'].shape), _const_spec(lw['w_router'].shape)],
        out_specs=[pl.BlockSpec((tm, N_EXPERTS), lambda g: (g, 0)),
                   pl.BlockSpec((tm, N_EXPERTS), lambda g: (g, 0)),
                   _whole_out_spec((1, N_EXPERTS))],
        out_shape=[jax.ShapeDtypeStruct((n_tok, N_EXPERTS), I32),
                   jax.ShapeDtypeStruct((n_tok, N_EXPERTS), F32),
                   jax.ShapeDtypeStruct((1, N_EXPERTS), F32)],
        scratch_shapes=[pltpu.VMEM((1, N_EXPERTS), F32)],
        compiler_params=_tc_params(),
        name="router",
    )(h, lw['g_ffn'], lw['w_router'])


def _sc_chunk(n_rows):
    per_worker = pl.cdiv(n_rows, SC_WORKERS)
    return pl.cdiv(per_worker, SC_WINDOW) * SC_WINDOW


def _sc_worker_base(n_rows, chunk):
    wid = lax.axis_index("s") * SC_CORES + lax.axis_index("c")
    return jnp.minimum(wid * chunk, n_rows - chunk)


def _sc_scatter_rows(x, dest, n_out):
    n = x.shape[0]
    chunk = _sc_chunk(n)
    mesh = plsc.VectorSubcoreMesh(core_axis_name="c", subcore_axis_name="s")

    @functools.partial(
        pl.kernel, mesh=mesh,
        out_type=jax.ShapeDtypeStruct((n_out, D_MODEL), x.dtype),
        scratch_types=[pltpu.VMEM((SC_WINDOW,), I32) for _ in range(TOP_K)]
        + [pltpu.VMEM((SC_WINDOW, D_MODEL), x.dtype), pltpu.SemaphoreType.DMA],
        name="sc_scatter_rows",
    )
    def scatter(x_hbm, dest_hbm, out_hbm, idx0_v, idx1_v, rows_v, sem):
        base = _sc_worker_base(n, chunk)

        @pl.loop(0, chunk // SC_WINDOW)
        def _(j):
            off = pl.multiple_of(base + j * SC_WINDOW, 8)
            pltpu.sync_copy(dest_hbm.at[pl.ds(off, SC_WINDOW)], idx0_v)
            pltpu.sync_copy(dest_hbm.at[pl.ds(n + off, SC_WINDOW)], idx1_v)
            pltpu.sync_copy(x_hbm.at[pl.ds(off, SC_WINDOW)], rows_v)
            first = pltpu.async_copy(rows_v, out_hbm.at[idx0_v], sem)
            second = pltpu.async_copy(rows_v, out_hbm.at[idx1_v], sem)
            first.wait()
            second.wait()

    return scatter(x, dest)


def _sc_gather_rows(y, idx):
    n = idx.shape[0]
    chunk = _sc_chunk(n)
    mesh = plsc.VectorSubcoreMesh(core_axis_name="c", subcore_axis_name="s")

    @functools.partial(
        pl.kernel, mesh=mesh,
        out_type=jax.ShapeDtypeStruct((n, D_MODEL), y.dtype),
        scratch_types=[pltpu.VMEM((SC_WINDOW,), I32),
                       pltpu.VMEM((SC_WINDOW, D_MODEL), y.dtype), pltpu.SemaphoreType.DMA],
        name="sc_gather_rows",
    )
    def gather(y_hbm, idx_hbm, out_hbm, idx_v, rows_v, sem):
        base = _sc_worker_base(n, chunk)

        @pl.loop(0, chunk // SC_WINDOW)
        def _(j):
            off = pl.multiple_of(base + j * SC_WINDOW, 8)
            pltpu.sync_copy(idx_hbm.at[pl.ds(off, SC_WINDOW)], idx_v)
            pltpu.async_copy(y_hbm.at[idx_v], rows_v, sem).wait()
            pltpu.sync_copy(rows_v, out_hbm.at[pl.ds(off, SC_WINDOW)])

    return gather(y, idx)


def _expert_kernel(tile_expert_ref, n_valid_ref, xs_ref, gffn_ref, wg_ref, wu_ref, wd_ref,
                   y_ref):
    @pl.when(pl.program_id(0) < n_valid_ref[0])
    def _():
        xn = _rms(xs_ref[...], gffn_ref[...]).astype(BF16)
        a = jax.nn.silu(_dot(xn, wg_ref[...])) * _dot(xn, wu_ref[...])
        y_ref[...] = _dot(a, wd_ref[...])


def _experts(xs, tile_expert, n_valid, lw):
    n_slots = xs.shape[0]
    te = EXPERT_ROWS
    d_exp = lw['w_ex_gate'].shape[-1]
    row_block = lambda g, tex, nv: (jnp.minimum(g, nv[0] - 1), 0)
    w_block = lambda g, tex, nv: (tex[g], 0, 0)
    return pl.pallas_call(
        _expert_kernel,
        grid_spec=pltpu.PrefetchScalarGridSpec(
            num_scalar_prefetch=2,
            grid=(n_slots // te,),
            in_specs=[pl.BlockSpec((te, D_MODEL), row_block),
                      pl.BlockSpec(lw['g_ffn'].shape, lambda g, tex, nv: (0, 0)),
                      pl.BlockSpec((None, D_MODEL, d_exp), w_block),
                      pl.BlockSpec((None, D_MODEL, d_exp), w_block),
                      pl.BlockSpec((None, d_exp, D_MODEL), w_block)],
            out_specs=pl.BlockSpec((te, D_MODEL), row_block)),
        out_shape=jax.ShapeDtypeStruct((n_slots, D_MODEL), F32),
        compiler_params=_tc_params(),
        name="experts",
    )(tile_expert, n_valid, xs, lw['g_ffn'], lw['w_ex_gate'], lw['w_ex_up'], lw['w_ex_down'])


def _combine_kernel(h_ref, yg_ref, mf_ref, pp_ref, ps_ref, gple_ref, wpg_ref, wpp_ref,
                    gfin_ref, yp_ref, ys_ref, pbuf_ref):
    g = pl.program_id(0)
    last = pl.num_programs(0) - 1
    n_dec = ps_ref.shape[0]
    gates = mf_ref[...]
    h = h_ref[...] + (gates[:, 0:1] * yg_ref[0] + gates[:, 1:2] * yg_ref[1])
    pbuf_ref[...] = pp_ref[...]

    @pl.when(g == last)
    def _():
        pbuf_ref[0:n_dec, :] = ps_ref[...]

    out = _rms(_ple(h, pbuf_ref[...], gple_ref, wpg_ref, wpp_ref, _dot), gfin_ref[...])

    @pl.when(g < last)
    def _():
        yp_ref[...] = out

    @pl.when(g == last)
    def _():
        ys_ref[...] = out[0:n_dec, :]


def _combine(h, yg, mf, pp, ps, lw, g_final):
    n_tok = h.shape[0]
    tm = TOK_ROWS
    n_tiles = pl.cdiv(n_tok, tm)
    n_dec = ps.shape[0]
    weights = [lw['g_ple'], lw['w_ple_gate'], lw['w_ple_proj'], g_final]
    seq_tile = lambda g: (jnp.minimum(g, n_tiles - 2), 0)
    return pl.pallas_call(
        _combine_kernel,
        grid=(n_tiles,),
        in_specs=[pl.BlockSpec((tm, D_MODEL), lambda g: (g, 0)),
                  pl.BlockSpec((TOP_K, tm, D_MODEL), lambda g: (0, g, 0)),
                  pl.BlockSpec((tm, N_EXPERTS), lambda g: (g, 0)),
                  pl.BlockSpec((tm, D_PLE), seq_tile),
                  _const_spec((n_dec, D_PLE))]
        + [_const_spec(w.shape) for w in weights],
        out_specs=[pl.BlockSpec((tm, D_MODEL), seq_tile),
                   _whole_out_spec((n_dec, D_MODEL))],
        out_shape=[jax.ShapeDtypeStruct((n_tok - n_dec, D_MODEL), F32),
                   jax.ShapeDtypeStruct((n_dec, D_MODEL), F32)],
        scratch_shapes=[pltpu.VMEM((tm, D_PLE), F32)],
        compiler_params=_tc_params(),
        name="combine",
    )(h, yg, mf, pp, ps, *weights)


def _moe_layer(h, pp, ps, lw, g_final):
    n_tok = h.shape[0]
    te = EXPERT_ROWS
    mi, mf, counts = _router(h, lw)

    cnt = counts[0].astype(I32)
    padded = (cnt + te - 1) // te * te
    ends = jnp.cumsum(padded)
    starts = ends - padded
    experts = jnp.arange(N_EXPERTS, dtype=I32)
    start_of = lambda e: jnp.sum(jnp.where(e[:, None] == experts, starts, 0), axis=-1)
    dest = jnp.concatenate([start_of(mi[:, 0]) + mi[:, 2], start_of(mi[:, 1]) + mi[:, 3]])
    n_tiles = pl.cdiv(TOP_K * n_tok + N_EXPERTS * (te - 1), te)
    tile_start = jnp.arange(n_tiles, dtype=I32) * te
    last_used = jnp.max(jnp.where(padded > 0, experts, 0))
    tile_expert = jnp.minimum(
        jnp.sum(tile_start[:, None] >= ends[None, :], axis=-1).astype(I32), last_used)
    n_valid = (ends[-1:] // te).astype(I32)

    xs = _sc_scatter_rows(h, dest, n_tiles * te)
    y = _experts(xs, tile_expert, n_valid, lw)
    yg = _sc_gather_rows(y, dest).reshape(TOP_K, n_tok, D_MODEL)
    return _combine(h, yg, mf, pp, ps, lw, g_final)


def _row(x):
    return x.reshape(1, -1)


def kernel(x_prompt, x_sample, state_conv_a, state_conv_c, p_prompt, p_sample, g_mix, w_in, w_conv_a, w_s, b_s, g_ln_b, b_ln_b, w_conv_c, b_conv_c, g_ln_c, b_ln_c, g_out, w_o, g_ffn, w_ff_gate, w_ff_up, w_ff_down, w_router, w_ex_gate, w_ex_up, w_ex_down, g_ple, w_ple_gate, w_ple_proj, g_final):
    depth = g_mix.shape[0]
    n_seq, seq, _ = x_prompt.shape
    n_dec = x_sample.shape[0]
    n_prompt = n_seq * seq
    assert depth == 2 and x_sample.shape[1] == 1
    assert seq % TOK_ROWS == 0 and TOK_ROWS % n_dec == 0 and w_ff_gate.shape[-1] % FF_COLS == 0

    hp = x_prompt.reshape(n_prompt, D_MODEL)
    hs = x_sample.reshape(n_dec, D_MODEL)
    outs = {k: [] for k in ('a_p', 'a_s', 'c_p', 'c_s', 'v_s')}
    for i in range(depth):
        j = i // 2
        lw = {
            'g_mix': _row(g_mix[i]), 'w_in': w_in[i].astype(BF16), 'w_in_f32': w_in[i],
            'w_conv_a': w_conv_a[i],
            'w_s_cat': jnp.transpose(w_s[i], (1, 0, 2)).reshape(CHUNK, N_HEADS_B * CHUNK),
            'b_s_full': jnp.repeat(b_s[i].T, HEAD_DIM, axis=1),
            'g_ln_b': _row(g_ln_b[i]), 'b_ln_b': _row(b_ln_b[i]),
            'w_conv_c': w_conv_c[i], 'b_conv_c': _row(b_conv_c[i]),
            'g_ln_c': _row(g_ln_c[i]), 'b_ln_c': _row(b_ln_c[i]),
            'g_out': _row(g_out[i]), 'w_o': w_o[i].astype(BF16), 'w_o_f32': w_o[i],
            'g_ffn': _row(g_ffn[i]), 'g_ple': _row(g_ple[i]),
            'w_ple_gate': w_ple_gate[i].astype(BF16), 'w_ple_proj': w_ple_proj[i].astype(BF16),
            'w_ple_gate_f32': w_ple_gate[i], 'w_ple_proj_f32': w_ple_proj[i],
        }
        hs, a_s, c_s, v_s = _mixer_row(hs, 0, jnp.swapaxes(state_conv_a[i], 0, 1),
                                       jnp.swapaxes(state_conv_c[i], 0, 1), lw)
        is_expert_layer = i % 2 == 1
        hp, a_p, c_p = _mixer_seq(hp, hs if is_expert_layer else None, lw, n_seq, seq)
        outs['a_p'].append(a_p)
        outs['c_p'].append(c_p)
        outs['a_s'].append(jnp.swapaxes(a_s, 0, 1))
        outs['c_s'].append(jnp.swapaxes(c_s, 0, 1))
        outs['v_s'].append(v_s.reshape(n_dec, 1, D_B))

        pp = p_prompt[i].reshape(n_prompt, D_PLE)
        ps = p_sample[i].reshape(n_dec, D_PLE)
        if not is_expert_layer:
            lw.update({'w_ff_gate': w_ff_gate[j].astype(BF16), 'w_ff_up': w_ff_up[j].astype(BF16),
                       'w_ff_down': w_ff_down[j].astype(BF16), 'w_ff_gate_f32': w_ff_gate[j],
                       'w_ff_up_f32': w_ff_up[j], 'w_ff_down_f32': w_ff_down[j]})
            hp = _ffn_dense(hp, pp, lw)
            hs = _ffn_row(hs, ps, lw)
        else:
            lw.update({'w_router': w_router[j], 'w_ex_gate': w_ex_gate[j].astype(BF16),
                       'w_ex_up': w_ex_up[j].astype(BF16),
                       'w_ex_down': w_ex_down[j].astype(BF16)})
            y_prompt, y_sample = _moe_layer(hp, pp, ps, lw, _row(g_final))

    return (y_prompt.reshape(x_prompt.shape), y_sample.reshape(x_sample.shape),
            jnp.stack(outs['a_p']), jnp.stack(outs['a_s']),
            jnp.stack(outs['c_p']), jnp.stack(outs['c_s']), jnp.stack(outs['v_s']))
```

```python
import functools

import jax
import jax.numpy as jnp
from jax import lax
from jax.experimental import pallas as pl
from jax.experimental.pallas import tpu as pltpu
from jax.experimental.pallas import tpu_sc as plsc

D_MODEL = 1024
HEAD_DIM = 64
D_A = 384
D_B = 256
D_C = 384
N_HEADS_B = D_B // HEAD_DIM
CONV_A = 3
CONV_C = 31
CHUNK = 128
D_IN = 3 * D_A + 2 * D_B + 2 * D_C
D_PLE = 256
N_EXPERTS = 8
TOP_K = 2
EPS = 1e-6

O1, O2, O3 = D_A, 2 * D_A, 3 * D_A
O4 = O3 + 2 * D_B

F32 = jnp.float32
BF16 = jnp.bfloat16
I32 = jnp.int32

VMEM_LIMIT_BYTES = 56 * 1024 * 1024
SUBLANES = 8

TOK_ROWS = 512
CONV_ROWS = 64
A_HALO = 8
C_HALO = 32
EXPERT_ROWS = 512
FF_COLS = 256

SC_CORES = 2
SC_WORKERS = 32
SC_WINDOW = 48


def _rms(x, g):
    return x * lax.rsqrt(jnp.mean(x * x, axis=-1, keepdims=True) + EPS) * g


def _ln(x, g, b):
    mu = jnp.mean(x, axis=-1, keepdims=True)
    xc = x - mu
    var = jnp.mean(xc * xc, axis=-1, keepdims=True)
    return xc * lax.rsqrt(var + EPS) * g + b


def _dot(a, b):
    return jnp.dot(a.astype(BF16), b, preferred_element_type=F32)


def _split(x):
    hi = x.astype(BF16)
    return hi, (x - hi.astype(F32)).astype(BF16)


def _dot_full(a, b):
    a_hi, a_lo = _split(a)
    b_hi, b_lo = _split(b)
    return (jnp.dot(jnp.concatenate([a_hi, a_lo], axis=1),
                    jnp.concatenate([b_hi, b_hi], axis=0), preferred_element_type=F32)
            + jnp.dot(a_hi, b_lo, preferred_element_type=F32))


def _const_spec(shape):
    return pl.BlockSpec(shape, lambda *_: (0,) * len(shape), pipeline_mode=pl.Buffered(1))


def _layer_spec(stacked, layer):
    idx = (layer,) if isinstance(layer, int) else tuple(layer)
    rest = stacked.shape[len(idx):]
    return pl.BlockSpec((None,) * len(idx) + rest, lambda *_: idx + (0,) * len(rest),
                        pipeline_mode=pl.Buffered(1))


def _whole_out_spec(shape):
    return pl.BlockSpec(shape, lambda *_: (0,) * len(shape))


def _tc_params():
    return pltpu.CompilerParams(dimension_semantics=("arbitrary",),
                                vmem_limit_bytes=VMEM_LIMIT_BYTES)


def _mixer_tail(h, y_a, y_b, y_c, gout_ref, wo_ref, dot):
    y = jnp.concatenate([_rms(y_a, gout_ref[:, :D_A]),
                         _rms(y_b, gout_ref[:, D_A:D_A + D_B]),
                         _rms(y_c, gout_ref[:, D_A + D_B:])], axis=-1)
    return h + dot(y, wo_ref[...])


def _mixer_seq_kernel(hp_ref, hs_ref, gmix_ref, win_ref, wca_ref, wscat_ref, bsfull_ref,
                      glnb_ref, blnb_ref, wcc_ref, bcc_ref, glnc_ref, blnc_ref, gout_ref,
                      wo_ref, hout_ref, newa_ref, newc_ref,
                      qext_ref, gext_ref, gshift_ref, conv_ref, *, tiles_per_seq, n_seq_tiles):
    g = pl.program_id(0)
    tm = hp_ref.shape[0]

    @pl.when(g < n_seq_tiles)
    def _sequence_tile():
        @pl.when(g % tiles_per_seq == 0)
        def _():
            qext_ref[0:A_HALO, :] = jnp.zeros((A_HALO, D_A), F32)
            gext_ref[0:C_HALO, :] = jnp.zeros((C_HALO, D_C), F32)

        h = hp_ref[...]
        z = _dot(_rms(h, gmix_ref[...]), win_ref[...])

        qext_ref[A_HALO:A_HALO + tm, :] = z[:, O1:O2] * z[:, O2:O3]
        conv_a = jnp.zeros((tm, D_A), F32)
        for k in range(CONV_A):
            off = A_HALO - (CONV_A - 1) + k
            conv_a = conv_a + wca_ref[k:k + 1, :] * qext_ref[off:off + tm, :]
        y_a = z[:, :O1] * conv_a
        last_q = qext_ref[A_HALO + tm - (CONV_A - 1):A_HALO + tm, :]
        newa_ref[...] = last_q
        qext_ref[A_HALO - (CONV_A - 1):A_HALO, :] = last_q

        zb = jax.nn.gelu(z[:, O3:O4])
        v = _ln(zb[:, D_B:], glnb_ref[...], blnb_ref[...])
        row = lax.broadcasted_iota(I32, (CHUNK, N_HEADS_B * CHUNK), 0)
        col = lax.broadcasted_iota(I32, (CHUNK, N_HEADS_B * CHUNK), 1)
        w_tril = jnp.where((col % CHUNK) <= row, wscat_ref[...], 0.0).astype(BF16)
        lane_head = lax.broadcasted_iota(I32, (CHUNK, D_B), 1) // HEAD_DIM
        s_chunks = []
        for c in range(tm // CHUNK):
            vc = v[c * CHUNK:(c + 1) * CHUNK, :]
            vstack = jnp.concatenate(
                [jnp.where(lane_head == hd, vc, 0.0) for hd in range(N_HEADS_B)], axis=0)
            s_chunks.append(_dot(w_tril, vstack.astype(BF16)) + bsfull_ref[...])
        y_b = zb[:, :D_B] * jnp.concatenate(s_chunks, axis=0)

        gext_ref[C_HALO:C_HALO + tm, :] = (z[:, O4:O4 + D_C]
                                           * jax.nn.sigmoid(z[:, O4 + D_C:]))
        n_shift = gshift_ref.shape[1]
        for s in range(1, SUBLANES):
            gshift_ref[s - 1] = gext_ref[s:s + n_shift, :]
        base = C_HALO - (CONV_C - 1)
        for r0 in range(0, tm, CONV_ROWS):
            acc = jnp.zeros((CONV_ROWS, D_C), F32)
            for k in range(CONV_C):
                lo = (base + k) // SUBLANES * SUBLANES + r0
                s = (base + k) % SUBLANES
                window = (gext_ref[lo:lo + CONV_ROWS, :] if s == 0
                          else gshift_ref[s - 1, lo:lo + CONV_ROWS, :])
                acc = acc + wcc_ref[k:k + 1, :] * window
            conv_ref[r0:r0 + CONV_ROWS, :] = acc
        y_c = jax.nn.silu(_ln(conv_ref[...] + bcc_ref[...], glnc_ref[...], blnc_ref[...]))
        last_g = gext_ref[C_HALO + tm - (CONV_C - 1):C_HALO + tm, :]
        newc_ref[...] = last_g
        gext_ref[C_HALO - (CONV_C - 1):C_HALO, :] = last_g

        hout_ref[...] = _mixer_tail(h, y_a, y_b, y_c, gout_ref, wo_ref, _dot)

    @pl.when(g == n_seq_tiles)
    def _append_sample_rows():
        hout_ref[0:hs_ref.shape[0], :] = hs_ref[...]


def _mixer_seq(hp, hs, lw, n_seq, seq):
    tm = TOK_ROWS
    tiles_per_seq = seq // tm
    n_seq_tiles = n_seq * tiles_per_seq
    join = hs is not None
    if not join:
        hs = jnp.zeros((SUBLANES, D_MODEL), F32)
    n_out = n_seq * seq + (hs.shape[0] if join else 0)
    weights = [lw[k] for k in ('g_mix', 'w_in', 'w_conv_a', 'w_s_cat', 'b_s_full', 'g_ln_b',
                               'b_ln_b', 'w_conv_c', 'b_conv_c', 'g_ln_c', 'b_ln_c', 'g_out',
                               'w_o')]
    seq_of = lambda g: jnp.minimum(g // tiles_per_seq, n_seq - 1)
    n_shift = tm + C_HALO - SUBLANES
    return pl.pallas_call(
        functools.partial(_mixer_seq_kernel, tiles_per_seq=tiles_per_seq,
                          n_seq_tiles=n_seq_tiles),
        grid=(n_seq_tiles + int(join),),
        in_specs=[pl.BlockSpec((tm, D_MODEL), lambda g: (jnp.minimum(g, n_seq_tiles - 1), 0)),
                  _const_spec(hs.shape)]
        + [_layer_spec(*w) for w in weights],
        out_specs=[pl.BlockSpec((tm, D_MODEL), lambda g: (g, 0)),
                   pl.BlockSpec((None, CONV_A - 1, D_A), lambda g: (seq_of(g), 0, 0)),
                   pl.BlockSpec((None, CONV_C - 1, D_C), lambda g: (seq_of(g), 0, 0))],
        out_shape=[jax.ShapeDtypeStruct((n_out, D_MODEL), F32),
                   jax.ShapeDtypeStruct((n_seq, CONV_A - 1, D_A), F32),
                   jax.ShapeDtypeStruct((n_seq, CONV_C - 1, D_C), F32)],
        scratch_shapes=[pltpu.VMEM((A_HALO + tm, D_A), F32),
                        pltpu.VMEM((C_HALO + tm, D_C), F32),
                        pltpu.VMEM((SUBLANES - 1, n_shift, D_C), F32),
                        pltpu.VMEM((tm, D_C), F32)],
        compiler_params=_tc_params(),
        name="mixer_seq",
    )(hp, hs, *[w for w, _ in weights])


def _mixer_row_kernel(h_ref, sa_ref, sc_ref, gmix_ref, win_ref, wca_ref, wscat_ref,
                      bsfull_ref, glnb_ref, blnb_ref, wcc_ref, bcc_ref, glnc_ref, blnc_ref,
                      gout_ref, wo_ref, hout_ref, newa_ref, newc_ref, v_ref):
    h = h_ref[...]
    z = _dot_full(_rms(h, gmix_ref[...]), win_ref[...])

    q = z[:, O1:O2] * z[:, O2:O3]
    conv_a = wca_ref[CONV_A - 1:CONV_A, :] * q
    for k in range(CONV_A - 1):
        conv_a = conv_a + wca_ref[k:k + 1, :] * sa_ref[k]
    y_a = z[:, :O1] * conv_a
    for k in range(CONV_A - 2):
        newa_ref[k] = sa_ref[k + 1]
    newa_ref[CONV_A - 2] = q

    zb = jax.nn.gelu(z[:, O3:O4])
    v = _ln(zb[:, D_B:], glnb_ref[...], blnb_ref[...])
    v_ref[...] = v
    w_diag0 = jnp.concatenate(
        [jnp.broadcast_to(wscat_ref[0:1, hd * CHUNK:hd * CHUNK + 1], (1, HEAD_DIM))
         for hd in range(N_HEADS_B)], axis=-1)
    y_b = zb[:, :D_B] * (w_diag0 * v + bsfull_ref[0:1, :])

    glu = z[:, O4:O4 + D_C] * jax.nn.sigmoid(z[:, O4 + D_C:])
    conv_c = wcc_ref[CONV_C - 1:CONV_C, :] * glu
    for k in range(CONV_C - 1):
        conv_c = conv_c + wcc_ref[k:k + 1, :] * sc_ref[k]
    y_c = jax.nn.silu(_ln(conv_c + bcc_ref[...], glnc_ref[...], blnc_ref[...]))
    for k in range(CONV_C - 2):
        newc_ref[k] = sc_ref[k + 1]
    newc_ref[CONV_C - 2] = glu

    hout_ref[...] = _mixer_tail(h, y_a, y_b, y_c, gout_ref, wo_ref, _dot_full)


def _mixer_row(h, h_block, lw):
    n_dec = lw['state_a'][0].shape[-2]
    weights = [lw[k] for k in ('state_a', 'state_c', 'g_mix', 'w_in_f32', 'w_conv_a', 'w_s_cat',
                               'b_s_full', 'g_ln_b', 'b_ln_b', 'w_conv_c', 'b_conv_c', 'g_ln_c',
                               'b_ln_c', 'g_out', 'w_o_f32')]
    return pl.pallas_call(
        _mixer_row_kernel,
        grid=(1,),
        in_specs=[pl.BlockSpec((n_dec, D_MODEL), lambda g: (h_block, 0))]
        + [_layer_spec(*w) for w in weights],
        out_specs=[_whole_out_spec((n_dec, D_MODEL)), _whole_out_spec((CONV_A - 1, n_dec, D_A)),
                   _whole_out_spec((CONV_C - 1, n_dec, D_C)), _whole_out_spec((n_dec, D_B))],
        out_shape=[jax.ShapeDtypeStruct((n_dec, D_MODEL), F32),
                   jax.ShapeDtypeStruct((CONV_A - 1, n_dec, D_A), F32),
                   jax.ShapeDtypeStruct((CONV_C - 1, n_dec, D_C), F32),
                   jax.ShapeDtypeStruct((n_dec, D_B), F32)],
        compiler_params=_tc_params(),
        name="mixer_row",
    )(h, *[w for w, _ in weights])


def _ple(h, p, gple_ref, wpg_ref, wpp_ref, dot):
    gate = jax.nn.sigmoid(dot(_rms(h, gple_ref[...]), wpg_ref[...]))
    return h + gate * dot(p, wpp_ref[...])


def _ffn_dense_kernel(h_ref, p_ref, gffn_ref, wg_ref, wu_ref, wd_ref, gple_ref, wpg_ref,
                      wpp_ref, out_ref):
    h = h_ref[...]
    xn = _rms(h, gffn_ref[...]).astype(BF16)
    a = jax.nn.silu(_dot(xn, wg_ref[...])) * _dot(xn, wu_ref[...])
    h = h + _dot(a, wd_ref[...])
    out_ref[...] = _ple(h, p_ref[...], gple_ref, wpg_ref, wpp_ref, _dot)


def _ffn_dense(h, lw):
    n_tok = h.shape[0]
    tm = TOK_ROWS
    p, layer = lw['p_seq']
    weights = [lw[k] for k in ('g_ffn', 'w_ff_gate', 'w_ff_up', 'w_ff_down', 'g_ple',
                               'w_ple_gate', 'w_ple_proj')]
    return pl.pallas_call(
        _ffn_dense_kernel,
        grid=(n_tok // tm,),
        in_specs=[pl.BlockSpec((tm, D_MODEL), lambda g: (g, 0)),
                  pl.BlockSpec((None, tm, D_PLE), lambda g: (layer, g, 0))]
        + [_layer_spec(*w) for w in weights],
        out_specs=pl.BlockSpec((tm, D_MODEL), lambda g: (g, 0)),
        out_shape=jax.ShapeDtypeStruct(h.shape, F32),
        compiler_params=_tc_params(),
        name="ffn_dense",
    )(h, p, *[w for w, _ in weights])


def _ffn_row_kernel(h_ref, p_ref, gffn_ref, wg_ref, wu_ref, wd_ref, gple_ref, wpg_ref,
                    wpp_ref, out_ref, xn_ref, acc_ref):
    j = pl.program_id(0)

    @pl.when(j == 0)
    def _():
        h = h_ref[...]
        xn_ref[...] = _rms(h, gffn_ref[...])
        acc_ref[...] = h

    xn = xn_ref[...]
    a = jax.nn.silu(_dot_full(xn, wg_ref[...])) * _dot_full(xn, wu_ref[...])
    acc_ref[...] += _dot_full(a, wd_ref[...])

    @pl.when(j == pl.num_programs(0) - 1)
    def _():
        out_ref[...] = _ple(acc_ref[...], p_ref[...], gple_ref, wpg_ref, wpp_ref, _dot_full)


def _ffn_row(h, lw):
    n_dec = h.shape[0]
    d_ff = lw['w_ff_gate_f32'][0].shape[-1]
    ff = lw['w_ff_gate_f32'][1]
    return pl.pallas_call(
        _ffn_row_kernel,
        grid=(d_ff // FF_COLS,),
        in_specs=[_const_spec(h.shape), _layer_spec(*lw['p_row']), _layer_spec(*lw['g_ffn']),
                  pl.BlockSpec((None, D_MODEL, FF_COLS), lambda j: (ff, 0, j)),
                  pl.BlockSpec((None, D_MODEL, FF_COLS), lambda j: (ff, 0, j)),
                  pl.BlockSpec((None, FF_COLS, D_MODEL), lambda j: (ff, j, 0)),
                  _layer_spec(*lw['g_ple']), _layer_spec(*lw['w_ple_gate_f32']),
                  _layer_spec(*lw['w_ple_proj_f32'])],
        out_specs=_whole_out_spec(h.shape),
        out_shape=jax.ShapeDtypeStruct(h.shape, F32),
        scratch_shapes=[pltpu.VMEM((n_dec, D_MODEL), F32), pltpu.VMEM((n_dec, D_MODEL), F32)],
        compiler_params=_tc_params(),
        name="ffn_row",
    )(h, *[lw[k][0] for k in ('p_row', 'g_ffn', 'w_ff_gate_f32', 'w_ff_up_f32',
                              'w_ff_down_f32', 'g_ple', 'w_ple_gate_f32', 'w_ple_proj_f32')])


def _router_kernel(h_ref, gffn_ref, wr_ref, mi_ref, mf_ref, cnt_ref, carry_ref, *, n_tok):
    g = pl.program_id(0)
    tm = h_ref.shape[0]

    @pl.when(g == 0)
    def _():
        carry_ref[...] = jnp.zeros(carry_ref.shape, F32)

    valid = (g * tm + lax.broadcasted_iota(I32, (tm, 1), 0)) < n_tok
    xn = _rms(jnp.where(valid, h_ref[...], 0.0), gffn_ref[...])
    logits = _dot_full(xn, wr_ref[...])
    lane = lax.broadcasted_iota(I32, logits.shape, 1)
    m1 = jnp.max(logits, axis=-1, keepdims=True)
    i1 = jnp.min(jnp.where(logits == m1, lane, N_EXPERTS), axis=-1, keepdims=True)
    rest = jnp.where(lane == i1, -jnp.inf, logits)
    m2 = jnp.max(rest, axis=-1, keepdims=True)
    i2 = jnp.min(jnp.where(rest == m2, lane, N_EXPERTS), axis=-1, keepdims=True)
    e2 = jnp.exp(m2 - m1)
    denom = 1.0 + e2
    w1 = 1.0 / denom
    w2 = e2 / denom

    oh1 = jnp.where((lane == i1) & valid, 1.0, 0.0)
    oh2 = jnp.where((lane == i2) & valid, 1.0, 0.0)
    member = oh1 + oh2
    r = lax.broadcasted_iota(I32, (tm, tm), 0)
    c = lax.broadcasted_iota(I32, (tm, tm), 1)
    before = jnp.where(c < r, 1.0, 0.0).astype(BF16)
    pos = _dot(before, member.astype(BF16)) + carry_ref[...]
    pos1 = jnp.sum(oh1 * pos, axis=-1, keepdims=True).astype(I32)
    pos2 = jnp.sum(oh2 * pos, axis=-1, keepdims=True).astype(I32)
    carry_ref[...] = carry_ref[...] + jnp.sum(member, axis=0, keepdims=True)
    cnt_ref[...] = carry_ref[...]

    mi_ref[...] = jnp.where(lane == 0, i1, jnp.where(lane == 1, i2,
                            jnp.where(lane == 2, pos1, jnp.where(lane == 3, pos2, 0))))
    mf_ref[...] = jnp.where(lane == 0, w1, jnp.where(lane == 1, w2, 0.0))


def _router(h, lw):
    n_tok = h.shape[0]
    tm = TOK_ROWS
    return pl.pallas_call(
        functools.partial(_router_kernel, n_tok=n_tok),
        grid=(pl.cdiv(n_tok, tm),),
        in_specs=[pl.BlockSpec((tm, D_MODEL), lambda g: (g, 0)),
                  _layer_spec(*lw['g_ffn']), _layer_spec(*lw['w_router'])],
        out_specs=[pl.BlockSpec((tm, N_EXPERTS), lambda g: (g, 0)),
                   pl.BlockSpec((tm, N_EXPERTS), lambda g: (g, 0)),
                   _whole_out_spec((1, N_EXPERTS))],
        out_shape=[jax.ShapeDtypeStruct((n_tok, N_EXPERTS), I32),
                   jax.ShapeDtypeStruct((n_tok, N_EXPERTS), F32),
                   jax.ShapeDtypeStruct((1, N_EXPERTS), F32)],
        scratch_shapes=[pltpu.VMEM((1, N_EXPERTS), F32)],
        compiler_params=_tc_params(),
        name="router",
    )(h, lw['g_ffn'][0], lw['w_router'][0])


def _sc_chunk(n_rows):
    per_worker = pl.cdiv(n_rows, SC_WORKERS)
    return pl.cdiv(per_worker, SC_WINDOW) * SC_WINDOW


def _sc_worker_base(n_rows, chunk):
    wid = lax.axis_index("s") * SC_CORES + lax.axis_index("c")
    return jnp.minimum(wid * chunk, n_rows - chunk)


def _sc_scatter_rows(x, dest, n_out):
    n = x.shape[0]
    chunk = _sc_chunk(n)
    mesh = plsc.VectorSubcoreMesh(core_axis_name="c", subcore_axis_name="s")

    @functools.partial(
        pl.kernel, mesh=mesh,
        out_type=jax.ShapeDtypeStruct((n_out, D_MODEL), x.dtype),
        scratch_types=[pltpu.VMEM((SC_WINDOW,), I32) for _ in range(TOP_K)]
        + [pltpu.VMEM((SC_WINDOW, D_MODEL), x.dtype), pltpu.SemaphoreType.DMA],
        name="sc_scatter_rows",
    )
    def scatter(x_hbm, dest_hbm, out_hbm, idx0_v, idx1_v, rows_v, sem):
        base = _sc_worker_base(n, chunk)

        @pl.loop(0, chunk // SC_WINDOW)
        def _(j):
            off = pl.multiple_of(base + j * SC_WINDOW, 8)
            pltpu.sync_copy(dest_hbm.at[pl.ds(off, SC_WINDOW)], idx0_v)
            pltpu.sync_copy(dest_hbm.at[pl.ds(n + off, SC_WINDOW)], idx1_v)
            pltpu.sync_copy(x_hbm.at[pl.ds(off, SC_WINDOW)], rows_v)
            first = pltpu.async_copy(rows_v, out_hbm.at[idx0_v], sem)
            second = pltpu.async_copy(rows_v, out_hbm.at[idx1_v], sem)
            first.wait()
            second.wait()

    return scatter(x, dest)


def _sc_gather_rows(y, idx):
    n = idx.shape[0]
    chunk = _sc_chunk(n)
    mesh = plsc.VectorSubcoreMesh(core_axis_name="c", subcore_axis_name="s")

    @functools.partial(
        pl.kernel, mesh=mesh,
        out_type=jax.ShapeDtypeStruct((n, D_MODEL), y.dtype),
        scratch_types=[pltpu.VMEM((SC_WINDOW,), I32),
                       pltpu.VMEM((SC_WINDOW, D_MODEL), y.dtype), pltpu.SemaphoreType.DMA],
        name="sc_gather_rows",
    )
    def gather(y_hbm, idx_hbm, out_hbm, idx_v, rows_v, sem):
        base = _sc_worker_base(n, chunk)

        @pl.loop(0, chunk // SC_WINDOW)
        def _(j):
            off = pl.multiple_of(base + j * SC_WINDOW, 8)
            pltpu.sync_copy(idx_hbm.at[pl.ds(off, SC_WINDOW)], idx_v)
            pltpu.async_copy(y_hbm.at[idx_v], rows_v, sem).wait()
            pltpu.sync_copy(rows_v, out_hbm.at[pl.ds(off, SC_WINDOW)])

    return gather(y, idx)


def _expert_kernel(tile_expert_ref, n_valid_ref, xs_ref, gffn_ref, wg_ref, wu_ref, wd_ref,
                   y_ref):
    @pl.when(pl.program_id(0) < n_valid_ref[0])
    def _():
        xn = _rms(xs_ref[...], gffn_ref[...]).astype(BF16)
        a = jax.nn.silu(_dot(xn, wg_ref[...])) * _dot(xn, wu_ref[...])
        y_ref[...] = _dot(a, wd_ref[...])


def _experts(xs, tile_expert, n_valid, lw):
    n_slots = xs.shape[0]
    te = EXPERT_ROWS
    d_exp = lw['w_ex_gate'][0].shape[-1]
    moe = lw['w_ex_gate'][1]
    row_block = lambda g, tex, nv: (jnp.minimum(g, nv[0] - 1), 0)
    w_block = lambda g, tex, nv: (moe, tex[g], 0, 0)
    return pl.pallas_call(
        _expert_kernel,
        grid_spec=pltpu.PrefetchScalarGridSpec(
            num_scalar_prefetch=2,
            grid=(n_slots // te,),
            in_specs=[pl.BlockSpec((te, D_MODEL), row_block),
                      _layer_spec(*lw['g_ffn']),
                      pl.BlockSpec((None, None, D_MODEL, d_exp), w_block),
                      pl.BlockSpec((None, None, D_MODEL, d_exp), w_block),
                      pl.BlockSpec((None, None, d_exp, D_MODEL), w_block)],
            out_specs=pl.BlockSpec((te, D_MODEL), row_block)),
        out_shape=jax.ShapeDtypeStruct((n_slots, D_MODEL), F32),
        compiler_params=_tc_params(),
        name="experts",
    )(tile_expert, n_valid, xs, *[lw[k][0] for k in ('g_ffn', 'w_ex_gate', 'w_ex_up',
                                                        'w_ex_down')])


def _combine_kernel(h_ref, yg_ref, mf_ref, pp_ref, ps_ref, gple_ref, wpg_ref, wpp_ref,
                    gfin_ref, yp_ref, ys_ref, pbuf_ref):
    g = pl.program_id(0)
    last = pl.num_programs(0) - 1
    n_dec = ps_ref.shape[0]
    gates = mf_ref[...]
    h = h_ref[...] + (gates[:, 0:1] * yg_ref[0] + gates[:, 1:2] * yg_ref[1])
    pbuf_ref[...] = pp_ref[...]

    @pl.when(g == last)
    def _():
        pbuf_ref[0:n_dec, :] = ps_ref[...]

    out = _rms(_ple(h, pbuf_ref[...], gple_ref, wpg_ref, wpp_ref, _dot), gfin_ref[...])

    @pl.when(g < last)
    def _():
        yp_ref[...] = out

    @pl.when(g == last)
    def _():
        ys_ref[...] = out[0:n_dec, :]


def _combine(h, yg, mf, lw):
    n_tok = h.shape[0]
    tm = TOK_ROWS
    n_tiles = pl.cdiv(n_tok, tm)
    pp, layer = lw['p_seq']
    n_dec = lw['p_row'][0].shape[-2]
    weights = [lw[k] for k in ('p_row', 'g_ple', 'w_ple_gate', 'w_ple_proj', 'g_final')]
    seq_tile = lambda g: (jnp.minimum(g, n_tiles - 2), 0)
    return pl.pallas_call(
        _combine_kernel,
        grid=(n_tiles,),
        in_specs=[pl.BlockSpec((tm, D_MODEL), lambda g: (g, 0)),
                  pl.BlockSpec((TOP_K, tm, D_MODEL), lambda g: (0, g, 0)),
                  pl.BlockSpec((tm, N_EXPERTS), lambda g: (g, 0)),
                  pl.BlockSpec((None, tm, D_PLE), lambda g: (layer,) + seq_tile(g))]
        + [_layer_spec(*w) for w in weights],
        out_specs=[pl.BlockSpec((tm, D_MODEL), seq_tile),
                   _whole_out_spec((n_dec, D_MODEL))],
        out_shape=[jax.ShapeDtypeStruct((n_tok - n_dec, D_MODEL), F32),
                   jax.ShapeDtypeStruct((n_dec, D_MODEL), F32)],
        scratch_shapes=[pltpu.VMEM((tm, D_PLE), F32)],
        compiler_params=_tc_params(),
        name="combine",
    )(h, yg, mf, pp, *[w for w, _ in weights])


def _moe_layer(h, lw):
    n_tok = h.shape[0]
    te = EXPERT_ROWS
    mi, mf, counts = _router(h, lw)

    cnt = counts[0].astype(I32)
    padded = (cnt + te - 1) // te * te
    ends = jnp.cumsum(padded)
    starts = ends - padded
    experts = jnp.arange(N_EXPERTS, dtype=I32)
    start_of = lambda e: jnp.sum(jnp.where(e[:, None] == experts, starts, 0), axis=-1)
    dest = jnp.concatenate([start_of(mi[:, 0]) + mi[:, 2], start_of(mi[:, 1]) + mi[:, 3]])
    n_tiles = pl.cdiv(TOP_K * n_tok + N_EXPERTS * (te - 1), te)
    tile_start = jnp.arange(n_tiles, dtype=I32) * te
    last_used = jnp.max(jnp.where(padded > 0, experts, 0))
    tile_expert = jnp.minimum(
        jnp.sum(tile_start[:, None] >= ends[None, :], axis=-1).astype(I32), last_used)
    n_valid = (ends[-1:] // te).astype(I32)

    xs = _sc_scatter_rows(h, dest, n_tiles * te)
    y = _experts(xs, tile_expert, n_valid, lw)
    yg = _sc_gather_rows(y, dest).reshape(TOP_K, n_tok, D_MODEL)
    return _combine(h, yg, mf, lw)


def kernel(x_prompt, x_sample, state_conv_a, state_conv_c, p_prompt, p_sample, g_mix, w_in, w_conv_a, w_s, b_s, g_ln_b, b_ln_b, w_conv_c, b_conv_c, g_ln_c, b_ln_c, g_out, w_o, g_ffn, w_ff_gate, w_ff_up, w_ff_down, w_router, w_ex_gate, w_ex_up, w_ex_down, g_ple, w_ple_gate, w_ple_proj, g_final):
    depth = g_mix.shape[0]
    n_seq, seq, _ = x_prompt.shape
    n_dec = x_sample.shape[0]
    n_prompt = n_seq * seq
    assert depth == 2 and x_sample.shape[1] == 1
    assert seq % TOK_ROWS == 0 and TOK_ROWS % n_dec == 0 and w_ff_gate.shape[-1] % FF_COLS == 0

    vec = lambda x: x.reshape(x.shape[0], 1, -1)
    per_layer = {
        'g_mix': vec(g_mix), 'w_in': w_in.astype(BF16), 'w_in_f32': w_in, 'w_conv_a': w_conv_a,
        'w_s_cat': jnp.transpose(w_s, (0, 2, 1, 3)).reshape(depth, CHUNK, N_HEADS_B * CHUNK),
        'b_s_full': jnp.repeat(jnp.swapaxes(b_s, 1, 2), HEAD_DIM, axis=2),
        'g_ln_b': vec(g_ln_b), 'b_ln_b': vec(b_ln_b), 'w_conv_c': w_conv_c,
        'b_conv_c': vec(b_conv_c), 'g_ln_c': vec(g_ln_c), 'b_ln_c': vec(b_ln_c),
        'g_out': vec(g_out), 'w_o': w_o.astype(BF16), 'w_o_f32': w_o,
        'g_ffn': vec(g_ffn), 'g_ple': vec(g_ple),
        'w_ple_gate': w_ple_gate.astype(BF16), 'w_ple_proj': w_ple_proj.astype(BF16),
        'w_ple_gate_f32': w_ple_gate, 'w_ple_proj_f32': w_ple_proj,
        'state_a': jnp.swapaxes(state_conv_a, 1, 2), 'state_c': jnp.swapaxes(state_conv_c, 1, 2),
        'p_seq': p_prompt.reshape(depth, n_prompt, D_PLE),
        'p_row': p_sample.reshape(depth, n_dec, D_PLE),
    }
    per_dense = {'w_ff_gate': w_ff_gate.astype(BF16), 'w_ff_up': w_ff_up.astype(BF16),
                 'w_ff_down': w_ff_down.astype(BF16), 'w_ff_gate_f32': w_ff_gate,
                 'w_ff_up_f32': w_ff_up, 'w_ff_down_f32': w_ff_down}
    per_moe = {'w_router': w_router, 'w_ex_gate': w_ex_gate.astype(BF16),
               'w_ex_up': w_ex_up.astype(BF16), 'w_ex_down': w_ex_down.astype(BF16)}

    hp = x_prompt.reshape(n_prompt, D_MODEL)
    hs = x_sample.reshape(n_dec, D_MODEL)
    outs = {k: [] for k in ('a_p', 'a_s', 'c_p', 'c_s', 'v_s')}
    for i in range(depth):
        is_expert_layer = i % 2 == 1
        lw = {k: (v, i) for k, v in per_layer.items()}
        lw.update({k: (v, i // 2) for k, v in (per_moe if is_expert_layer else per_dense).items()})
        lw['g_final'] = (g_final.reshape(1, 1, -1), 0)

        hs, a_s, c_s, v_s = _mixer_row(hs, 0, lw)
        hp, a_p, c_p = _mixer_seq(hp, hs if is_expert_layer else None, lw, n_seq, seq)
        for k, v in zip(('a_p', 'c_p', 'a_s', 'c_s', 'v_s'), (a_p, c_p, a_s, c_s, v_s)):
            outs[k].append(v)
        if not is_expert_layer:
            hp = _ffn_dense(hp, lw)
            hs = _ffn_row(hs, lw)
        else:
            y_prompt, y_sample = _moe_layer(hp, lw)

    return (y_prompt.reshape(x_prompt.shape), y_sample.reshape(x_sample.shape),
            jnp.stack(outs['a_p']), jnp.swapaxes(jnp.stack(outs['a_s']), 1, 2),
            jnp.stack(outs['c_p']), jnp.swapaxes(jnp.stack(outs['c_s']), 1, 2),
            jnp.stack(outs['v_s']).reshape(depth, n_dec, 1, D_B))
```

```python
import functools

import jax
import jax.numpy as jnp
from jax import lax
from jax.experimental import pallas as pl
from jax.experimental.pallas import tpu as pltpu
from jax.experimental.pallas import tpu_sc as plsc

D_MODEL = 1024
HEAD_DIM = 64
D_A = 384
D_B = 256
D_C = 384
N_HEADS_B = D_B // HEAD_DIM
CONV_A = 3
CONV_C = 31
CHUNK = 128
D_IN = 3 * D_A + 2 * D_B + 2 * D_C
D_PLE = 256
N_EXPERTS = 8
TOP_K = 2
EPS = 1e-6

O1, O2, O3 = D_A, 2 * D_A, 3 * D_A
O4 = O3 + 2 * D_B

F32 = jnp.float32
BF16 = jnp.bfloat16
I32 = jnp.int32

VMEM_LIMIT_BYTES = 56 * 1024 * 1024
SUBLANES = 8

TOK_ROWS = 512
CONV_ROWS = 64
A_HALO = 8
C_HALO = 32
EXPERT_ROWS = 512
FF_COLS = 256
MOE_CHUNKS = 2

SC_CORES = 2
SC_WORKERS = 32
SC_WINDOW = 48


def _rms(x, g):
    return x * lax.rsqrt(jnp.mean(x * x, axis=-1, keepdims=True) + EPS) * g


def _ln(x, g, b):
    mu = jnp.mean(x, axis=-1, keepdims=True)
    xc = x - mu
    var = jnp.mean(xc * xc, axis=-1, keepdims=True)
    return xc * lax.rsqrt(var + EPS) * g + b


def _dot(a, b):
    return jnp.dot(a.astype(BF16), b, preferred_element_type=F32)


def _split(x):
    hi = x.astype(BF16)
    return hi, (x - hi.astype(F32)).astype(BF16)


def _dot_full(a, b):
    a_hi, a_lo = _split(a)
    b_hi, b_lo = _split(b)
    return (jnp.dot(jnp.concatenate([a_hi, a_lo], axis=1),
                    jnp.concatenate([b_hi, b_hi], axis=0), preferred_element_type=F32)
            + jnp.dot(a_hi, b_lo, preferred_element_type=F32))


def _const_spec(shape):
    return pl.BlockSpec(shape, lambda *_: (0,) * len(shape), pipeline_mode=pl.Buffered(1))


def _layer_spec(stacked, layer):
    idx = (layer,) if isinstance(layer, int) else tuple(layer)
    rest = stacked.shape[len(idx):]
    return pl.BlockSpec((None,) * len(idx) + rest, lambda *_: idx + (0,) * len(rest),
                        pipeline_mode=pl.Buffered(1))


def _whole_out_spec(shape):
    return pl.BlockSpec(shape, lambda *_: (0,) * len(shape))


def _tc_params():
    return pltpu.CompilerParams(dimension_semantics=("arbitrary",),
                                vmem_limit_bytes=VMEM_LIMIT_BYTES)


def _mixer_tail(h, y_a, y_b, y_c, gout_ref, wo_ref, dot):
    y = jnp.concatenate([_rms(y_a, gout_ref[:, :D_A]),
                         _rms(y_b, gout_ref[:, D_A:D_A + D_B]),
                         _rms(y_c, gout_ref[:, D_A + D_B:])], axis=-1)
    return h + dot(y, wo_ref[...])


def _mixer_seq_kernel(hp_ref, hs_ref, gmix_ref, win_ref, wca_ref, wscat_ref, bsfull_ref,
                      glnb_ref, blnb_ref, wcc_ref, bcc_ref, glnc_ref, blnc_ref, gout_ref,
                      wo_ref, hout_ref, newa_ref, newc_ref,
                      qext_ref, gext_ref, gshift_ref, conv_ref, *, tiles_per_seq, n_seq_tiles):
    g = pl.program_id(0)
    tm = hp_ref.shape[0]

    @pl.when(g < n_seq_tiles)
    def _sequence_tile():
        @pl.when(g % tiles_per_seq == 0)
        def _():
            qext_ref[0:A_HALO, :] = jnp.zeros((A_HALO, D_A), F32)
            gext_ref[0:C_HALO, :] = jnp.zeros((C_HALO, D_C), F32)

        h = hp_ref[...]
        z = _dot(_rms(h, gmix_ref[...]), win_ref[...])

        qext_ref[A_HALO:A_HALO + tm, :] = z[:, O1:O2] * z[:, O2:O3]
        conv_a = jnp.zeros((tm, D_A), F32)
        for k in range(CONV_A):
            off = A_HALO - (CONV_A - 1) + k
            conv_a = conv_a + wca_ref[k:k + 1, :] * qext_ref[off:off + tm, :]
        y_a = z[:, :O1] * conv_a
        last_q = qext_ref[A_HALO + tm - (CONV_A - 1):A_HALO + tm, :]
        newa_ref[...] = last_q
        qext_ref[A_HALO - (CONV_A - 1):A_HALO, :] = last_q

        zb = jax.nn.gelu(z[:, O3:O4])
        v = _ln(zb[:, D_B:], glnb_ref[...], blnb_ref[...])
        row = lax.broadcasted_iota(I32, (CHUNK, N_HEADS_B * CHUNK), 0)
        col = lax.broadcasted_iota(I32, (CHUNK, N_HEADS_B * CHUNK), 1)
        w_tril = jnp.where((col % CHUNK) <= row, wscat_ref[...], 0.0).astype(BF16)
        lane_head = lax.broadcasted_iota(I32, (CHUNK, D_B), 1) // HEAD_DIM
        s_chunks = []
        for c in range(tm // CHUNK):
            vc = v[c * CHUNK:(c + 1) * CHUNK, :]
            vstack = jnp.concatenate(
                [jnp.where(lane_head == hd, vc, 0.0) for hd in range(N_HEADS_B)], axis=0)
            s_chunks.append(_dot(w_tril, vstack.astype(BF16)) + bsfull_ref[...])
        y_b = zb[:, :D_B] * jnp.concatenate(s_chunks, axis=0)

        gext_ref[C_HALO:C_HALO + tm, :] = (z[:, O4:O4 + D_C]
                                           * jax.nn.sigmoid(z[:, O4 + D_C:]))
        n_shift = gshift_ref.shape[1]
        for s in range(1, SUBLANES):
            gshift_ref[s - 1] = gext_ref[s:s + n_shift, :]
        base = C_HALO - (CONV_C - 1)
        for r0 in range(0, tm, CONV_ROWS):
            acc = jnp.zeros((CONV_ROWS, D_C), F32)
            for k in range(CONV_C):
                lo = (base + k) // SUBLANES * SUBLANES + r0
                s = (base + k) % SUBLANES
                window = (gext_ref[lo:lo + CONV_ROWS, :] if s == 0
                          else gshift_ref[s - 1, lo:lo + CONV_ROWS, :])
                acc = acc + wcc_ref[k:k + 1, :] * window
            conv_ref[r0:r0 + CONV_ROWS, :] = acc
        y_c = jax.nn.silu(_ln(conv_ref[...] + bcc_ref[...], glnc_ref[...], blnc_ref[...]))
        last_g = gext_ref[C_HALO + tm - (CONV_C - 1):C_HALO + tm, :]
        newc_ref[...] = last_g
        gext_ref[C_HALO - (CONV_C - 1):C_HALO, :] = last_g

        hout_ref[...] = _mixer_tail(h, y_a, y_b, y_c, gout_ref, wo_ref, _dot)

    @pl.when(g == n_seq_tiles)
    def _append_sample_rows():
        hout_ref[0:hs_ref.shape[0], :] = hs_ref[...]


def _mixer_seq(hp, hs, lw, n_seq, seq):
    tm = TOK_ROWS
    tiles_per_seq = seq // tm
    n_seq_tiles = n_seq * tiles_per_seq
    join = hs is not None
    if not join:
        hs = jnp.zeros((SUBLANES, D_MODEL), F32)
    n_out = n_seq * seq + (hs.shape[0] if join else 0)
    weights = [lw[k] for k in ('g_mix', 'w_in', 'w_conv_a', 'w_s_cat', 'b_s_full', 'g_ln_b',
                               'b_ln_b', 'w_conv_c', 'b_conv_c', 'g_ln_c', 'b_ln_c', 'g_out',
                               'w_o')]
    seq_of = lambda g: jnp.minimum(g // tiles_per_seq, n_seq - 1)
    n_shift = tm + C_HALO - SUBLANES
    return pl.pallas_call(
        functools.partial(_mixer_seq_kernel, tiles_per_seq=tiles_per_seq,
                          n_seq_tiles=n_seq_tiles),
        grid=(n_seq_tiles + int(join),),
        in_specs=[pl.BlockSpec((tm, D_MODEL), lambda g: (jnp.minimum(g, n_seq_tiles - 1), 0)),
                  _const_spec(hs.shape)]
        + [_layer_spec(*w) for w in weights],
        out_specs=[pl.BlockSpec((tm, D_MODEL), lambda g: (g, 0)),
                   pl.BlockSpec((None, CONV_A - 1, D_A), lambda g: (seq_of(g), 0, 0)),
                   pl.BlockSpec((None, CONV_C - 1, D_C), lambda g: (seq_of(g), 0, 0))],
        out_shape=[jax.ShapeDtypeStruct((n_out, D_MODEL), F32),
                   jax.ShapeDtypeStruct((n_seq, CONV_A - 1, D_A), F32),
                   jax.ShapeDtypeStruct((n_seq, CONV_C - 1, D_C), F32)],
        scratch_shapes=[pltpu.VMEM((A_HALO + tm, D_A), F32),
                        pltpu.VMEM((C_HALO + tm, D_C), F32),
                        pltpu.VMEM((SUBLANES - 1, n_shift, D_C), F32),
                        pltpu.VMEM((tm, D_C), F32)],
        compiler_params=_tc_params(),
        name="mixer_seq",
    )(hp, hs, *[w for w, _ in weights])


def _mixer_row_kernel(h_ref, sa_ref, sc_ref, gmix_ref, win_ref, wca_ref, wscat_ref,
                      bsfull_ref, glnb_ref, blnb_ref, wcc_ref, bcc_ref, glnc_ref, blnc_ref,
                      gout_ref, wo_ref, hout_ref, newa_ref, newc_ref, v_ref):
    h = h_ref[...]
    z = _dot_full(_rms(h, gmix_ref[...]), win_ref[...])

    q = z[:, O1:O2] * z[:, O2:O3]
    conv_a = wca_ref[CONV_A - 1:CONV_A, :] * q
    for k in range(CONV_A - 1):
        conv_a = conv_a + wca_ref[k:k + 1, :] * sa_ref[k]
    y_a = z[:, :O1] * conv_a
    for k in range(CONV_A - 2):
        newa_ref[k] = sa_ref[k + 1]
    newa_ref[CONV_A - 2] = q

    zb = jax.nn.gelu(z[:, O3:O4])
    v = _ln(zb[:, D_B:], glnb_ref[...], blnb_ref[...])
    v_ref[...] = v
    w_diag0 = jnp.concatenate(
        [jnp.broadcast_to(wscat_ref[0:1, hd * CHUNK:hd * CHUNK + 1], (1, HEAD_DIM))
         for hd in range(N_HEADS_B)], axis=-1)
    y_b = zb[:, :D_B] * (w_diag0 * v + bsfull_ref[0:1, :])

    glu = z[:, O4:O4 + D_C] * jax.nn.sigmoid(z[:, O4 + D_C:])
    conv_c = wcc_ref[CONV_C - 1:CONV_C, :] * glu
    for k in range(CONV_C - 1):
        conv_c = conv_c + wcc_ref[k:k + 1, :] * sc_ref[k]
    y_c = jax.nn.silu(_ln(conv_c + bcc_ref[...], glnc_ref[...], blnc_ref[...]))
    for k in range(CONV_C - 2):
        newc_ref[k] = sc_ref[k + 1]
    newc_ref[CONV_C - 2] = glu

    hout_ref[...] = _mixer_tail(h, y_a, y_b, y_c, gout_ref, wo_ref, _dot_full)


def _mixer_row(h, h_block, lw):
    n_dec = lw['state_a'][0].shape[-2]
    weights = [lw[k] for k in ('state_a', 'state_c', 'g_mix', 'w_in_f32', 'w_conv_a', 'w_s_cat',
                               'b_s_full', 'g_ln_b', 'b_ln_b', 'w_conv_c', 'b_conv_c', 'g_ln_c',
                               'b_ln_c', 'g_out', 'w_o_f32')]
    return pl.pallas_call(
        _mixer_row_kernel,
        grid=(1,),
        in_specs=[pl.BlockSpec((n_dec, D_MODEL), lambda g: (h_block, 0))]
        + [_layer_spec(*w) for w in weights],
        out_specs=[_whole_out_spec((n_dec, D_MODEL)), _whole_out_spec((CONV_A - 1, n_dec, D_A)),
                   _whole_out_spec((CONV_C - 1, n_dec, D_C)), _whole_out_spec((n_dec, D_B))],
        out_shape=[jax.ShapeDtypeStruct((n_dec, D_MODEL), F32),
                   jax.ShapeDtypeStruct((CONV_A - 1, n_dec, D_A), F32),
                   jax.ShapeDtypeStruct((CONV_C - 1, n_dec, D_C), F32),
                   jax.ShapeDtypeStruct((n_dec, D_B), F32)],
        compiler_params=_tc_params(),
        name="mixer_row",
    )(h, *[w for w, _ in weights])


def _ple(h, p, gple_ref, wpg_ref, wpp_ref, dot):
    gate = jax.nn.sigmoid(dot(_rms(h, gple_ref[...]), wpg_ref[...]))
    return h + gate * dot(p, wpp_ref[...])


def _ffn_dense_kernel(h_ref, p_ref, gffn_ref, wg_ref, wu_ref, wd_ref, gple_ref, wpg_ref,
                      wpp_ref, out_ref):
    h = h_ref[...]
    xn = _rms(h, gffn_ref[...]).astype(BF16)
    a = jax.nn.silu(_dot(xn, wg_ref[...])) * _dot(xn, wu_ref[...])
    h = h + _dot(a, wd_ref[...])
    out_ref[...] = _ple(h, p_ref[...], gple_ref, wpg_ref, wpp_ref, _dot)


def _ffn_dense(h, lw):
    n_tok = h.shape[0]
    tm = TOK_ROWS
    p, layer = lw['p_seq']
    weights = [lw[k] for k in ('g_ffn', 'w_ff_gate', 'w_ff_up', 'w_ff_down', 'g_ple',
                               'w_ple_gate', 'w_ple_proj')]
    return pl.pallas_call(
        _ffn_dense_kernel,
        grid=(n_tok // tm,),
        in_specs=[pl.BlockSpec((tm, D_MODEL), lambda g: (g, 0)),
                  pl.BlockSpec((None, tm, D_PLE), lambda g: (layer, g, 0))]
        + [_layer_spec(*w) for w in weights],
        out_specs=pl.BlockSpec((tm, D_MODEL), lambda g: (g, 0)),
        out_shape=jax.ShapeDtypeStruct(h.shape, F32),
        compiler_params=_tc_params(),
        name="ffn_dense",
    )(h, p, *[w for w, _ in weights])


def _ffn_row_kernel(h_ref, p_ref, gffn_ref, wg_ref, wu_ref, wd_ref, gple_ref, wpg_ref,
                    wpp_ref, out_ref, xn_ref, acc_ref):
    j = pl.program_id(0)

    @pl.when(j == 0)
    def _():
        h = h_ref[...]
        xn_ref[...] = _rms(h, gffn_ref[...])
        acc_ref[...] = h

    xn = xn_ref[...]
    a = jax.nn.silu(_dot_full(xn, wg_ref[...])) * _dot_full(xn, wu_ref[...])
    acc_ref[...] += _dot_full(a, wd_ref[...])

    @pl.when(j == pl.num_programs(0) - 1)
    def _():
        out_ref[...] = _ple(acc_ref[...], p_ref[...], gple_ref, wpg_ref, wpp_ref, _dot_full)


def _ffn_row(h, lw):
    n_dec = h.shape[0]
    d_ff = lw['w_ff_gate_f32'][0].shape[-1]
    ff = lw['w_ff_gate_f32'][1]
    return pl.pallas_call(
        _ffn_row_kernel,
        grid=(d_ff // FF_COLS,),
        in_specs=[_const_spec(h.shape), _layer_spec(*lw['p_row']), _layer_spec(*lw['g_ffn']),
                  pl.BlockSpec((None, D_MODEL, FF_COLS), lambda j: (ff, 0, j)),
                  pl.BlockSpec((None, D_MODEL, FF_COLS), lambda j: (ff, 0, j)),
                  pl.BlockSpec((None, FF_COLS, D_MODEL), lambda j: (ff, j, 0)),
                  _layer_spec(*lw['g_ple']), _layer_spec(*lw['w_ple_gate_f32']),
                  _layer_spec(*lw['w_ple_proj_f32'])],
        out_specs=_whole_out_spec(h.shape),
        out_shape=jax.ShapeDtypeStruct(h.shape, F32),
        scratch_shapes=[pltpu.VMEM((n_dec, D_MODEL), F32), pltpu.VMEM((n_dec, D_MODEL), F32)],
        compiler_params=_tc_params(),
        name="ffn_row",
    )(h, *[lw[k][0] for k in ('p_row', 'g_ffn', 'w_ff_gate_f32', 'w_ff_up_f32',
                              'w_ff_down_f32', 'g_ple', 'w_ple_gate_f32', 'w_ple_proj_f32')])


def _router_kernel(h_ref, gffn_ref, wr_ref, mi_ref, mf_ref, cnt_ref, carry_ref, *, n_tok):
    g = pl.program_id(0)
    tm = h_ref.shape[0]

    @pl.when(g == 0)
    def _():
        carry_ref[...] = jnp.zeros(carry_ref.shape, F32)

    valid = (g * tm + lax.broadcasted_iota(I32, (tm, 1), 0)) < n_tok
    xn = _rms(jnp.where(valid, h_ref[...], 0.0), gffn_ref[...])
    logits = _dot_full(xn, wr_ref[...])
    lane = lax.broadcasted_iota(I32, logits.shape, 1)
    m1 = jnp.max(logits, axis=-1, keepdims=True)
    i1 = jnp.min(jnp.where(logits == m1, lane, N_EXPERTS), axis=-1, keepdims=True)
    rest = jnp.where(lane == i1, -jnp.inf, logits)
    m2 = jnp.max(rest, axis=-1, keepdims=True)
    i2 = jnp.min(jnp.where(rest == m2, lane, N_EXPERTS), axis=-1, keepdims=True)
    e2 = jnp.exp(m2 - m1)
    denom = 1.0 + e2
    w1 = 1.0 / denom
    w2 = e2 / denom

    oh1 = jnp.where((lane == i1) & valid, 1.0, 0.0)
    oh2 = jnp.where((lane == i2) & valid, 1.0, 0.0)
    member = oh1 + oh2
    r = lax.broadcasted_iota(I32, (tm, tm), 0)
    c = lax.broadcasted_iota(I32, (tm, tm), 1)
    before = jnp.where(c < r, 1.0, 0.0).astype(BF16)
    pos = _dot(before, member.astype(BF16)) + carry_ref[...]
    pos1 = jnp.sum(oh1 * pos, axis=-1, keepdims=True).astype(I32)
    pos2 = jnp.sum(oh2 * pos, axis=-1, keepdims=True).astype(I32)
    carry_ref[...] = carry_ref[...] + jnp.sum(member, axis=0, keepdims=True)
    cnt_ref[...] = carry_ref[...]

    mi_ref[...] = jnp.where(lane == 0, i1, jnp.where(lane == 1, i2,
                            jnp.where(lane == 2, pos1, jnp.where(lane == 3, pos2, 0))))
    mf_ref[...] = jnp.where(lane == 0, w1, jnp.where(lane == 1, w2, 0.0))


def _router(h, tile0, n_tok, lw):
    tm = TOK_ROWS
    return pl.pallas_call(
        functools.partial(_router_kernel, n_tok=n_tok),
        grid=(pl.cdiv(n_tok, tm),),
        in_specs=[pl.BlockSpec((tm, D_MODEL), lambda g: (tile0 + g, 0)),
                  _layer_spec(*lw['g_ffn']), _layer_spec(*lw['w_router'])],
        out_specs=[pl.BlockSpec((tm, N_EXPERTS), lambda g: (g, 0)),
                   pl.BlockSpec((tm, N_EXPERTS), lambda g: (g, 0)),
                   _whole_out_spec((1, N_EXPERTS))],
        out_shape=[jax.ShapeDtypeStruct((n_tok, N_EXPERTS), I32),
                   jax.ShapeDtypeStruct((n_tok, N_EXPERTS), F32),
                   jax.ShapeDtypeStruct((1, N_EXPERTS), F32)],
        scratch_shapes=[pltpu.VMEM((1, N_EXPERTS), F32)],
        compiler_params=_tc_params(),
        name="router",
    )(h, lw['g_ffn'][0], lw['w_router'][0])


def _sc_chunk(n_rows):
    per_worker = pl.cdiv(n_rows, SC_WORKERS)
    return pl.cdiv(per_worker, SC_WINDOW) * SC_WINDOW


def _sc_worker_base(n_rows, chunk):
    wid = lax.axis_index("s") * SC_CORES + lax.axis_index("c")
    return jnp.minimum(wid * chunk, n_rows - chunk)


def _sc_scatter_rows(x, row0, dest, n_out):
    n = dest.shape[0] // TOP_K
    chunk = _sc_chunk(n)
    mesh = plsc.VectorSubcoreMesh(core_axis_name="c", subcore_axis_name="s")

    @functools.partial(
        pl.kernel, mesh=mesh,
        out_type=jax.ShapeDtypeStruct((n_out, D_MODEL), x.dtype),
        scratch_types=[pltpu.VMEM((SC_WINDOW,), I32) for _ in range(TOP_K)]
        + [pltpu.VMEM((SC_WINDOW, D_MODEL), x.dtype), pltpu.SemaphoreType.DMA],
        name="sc_scatter_rows",
    )
    def scatter(x_hbm, dest_hbm, out_hbm, idx0_v, idx1_v, rows_v, sem):
        base = _sc_worker_base(n, chunk)

        @pl.loop(0, chunk // SC_WINDOW)
        def _(j):
            off = pl.multiple_of(base + j * SC_WINDOW, 8)
            pltpu.sync_copy(dest_hbm.at[pl.ds(off, SC_WINDOW)], idx0_v)
            pltpu.sync_copy(dest_hbm.at[pl.ds(n + off, SC_WINDOW)], idx1_v)
            pltpu.sync_copy(x_hbm.at[pl.ds(row0 + off, SC_WINDOW)], rows_v)
            first = pltpu.async_copy(rows_v, out_hbm.at[idx0_v], sem)
            second = pltpu.async_copy(rows_v, out_hbm.at[idx1_v], sem)
            first.wait()
            second.wait()

    return scatter(x, dest)


def _sc_gather_rows(y, idx):
    n = idx.shape[0]
    chunk = _sc_chunk(n)
    mesh = plsc.VectorSubcoreMesh(core_axis_name="c", subcore_axis_name="s")

    @functools.partial(
        pl.kernel, mesh=mesh,
        out_type=jax.ShapeDtypeStruct((n, D_MODEL), y.dtype),
        scratch_types=[pltpu.VMEM((SC_WINDOW,), I32),
                       pltpu.VMEM((SC_WINDOW, D_MODEL), y.dtype), pltpu.SemaphoreType.DMA],
        name="sc_gather_rows",
    )
    def gather(y_hbm, idx_hbm, out_hbm, idx_v, rows_v, sem):
        base = _sc_worker_base(n, chunk)

        @pl.loop(0, chunk // SC_WINDOW)
        def _(j):
            off = pl.multiple_of(base + j * SC_WINDOW, 8)
            pltpu.sync_copy(idx_hbm.at[pl.ds(off, SC_WINDOW)], idx_v)
            pltpu.async_copy(y_hbm.at[idx_v], rows_v, sem).wait()
            pltpu.sync_copy(rows_v, out_hbm.at[pl.ds(off, SC_WINDOW)])

    return gather(y, idx)


def _expert_kernel(tile_expert_ref, n_valid_ref, xs_ref, gffn_ref, wg_ref, wu_ref, wd_ref,
                   y_ref):
    @pl.when(pl.program_id(0) < n_valid_ref[0])
    def _():
        xn = _rms(xs_ref[...], gffn_ref[...]).astype(BF16)
        a = jax.nn.silu(_dot(xn, wg_ref[...])) * _dot(xn, wu_ref[...])
        y_ref[...] = _dot(a, wd_ref[...])


def _experts(xs, tile_expert, n_valid, lw):
    n_slots = xs.shape[0]
    te = EXPERT_ROWS
    d_exp = lw['w_ex_gate'][0].shape[-1]
    moe = lw['w_ex_gate'][1]
    row_block = lambda g, tex, nv: (jnp.minimum(g, nv[0] - 1), 0)
    w_block = lambda g, tex, nv: (moe, tex[g], 0, 0)
    return pl.pallas_call(
        _expert_kernel,
        grid_spec=pltpu.PrefetchScalarGridSpec(
            num_scalar_prefetch=2,
            grid=(n_slots // te,),
            in_specs=[pl.BlockSpec((te, D_MODEL), row_block),
                      _layer_spec(*lw['g_ffn']),
                      pl.BlockSpec((None, None, D_MODEL, d_exp), w_block),
                      pl.BlockSpec((None, None, D_MODEL, d_exp), w_block),
                      pl.BlockSpec((None, None, d_exp, D_MODEL), w_block)],
            out_specs=pl.BlockSpec((te, D_MODEL), row_block)),
        out_shape=jax.ShapeDtypeStruct((n_slots, D_MODEL), F32),
        compiler_params=_tc_params(),
        name="experts",
    )(tile_expert, n_valid, xs, *[lw[k][0] for k in ('g_ffn', 'w_ex_gate', 'w_ex_up',
                                                        'w_ex_down')])


def _combine_math(h_ref, yg_ref, mf_ref, p, gple_ref, wpg_ref, wpp_ref, gfin_ref):
    gates = mf_ref[...]
    h = h_ref[...] + (gates[:, 0:1] * yg_ref[0] + gates[:, 1:2] * yg_ref[1])
    return _rms(_ple(h, p, gple_ref, wpg_ref, wpp_ref, _dot), gfin_ref[...])


def _combine_seq_kernel(h_ref, yg_ref, mf_ref, pp_ref, gple_ref, wpg_ref, wpp_ref, gfin_ref,
                        ybuf_ref, yp_ref):
    del ybuf_ref
    yp_ref[...] = _combine_math(h_ref, yg_ref, mf_ref, pp_ref[...], gple_ref, wpg_ref,
                                wpp_ref, gfin_ref)


def _combine_tail_kernel(h_ref, yg_ref, mf_ref, pp_ref, ps_ref, gple_ref, wpg_ref, wpp_ref,
                         gfin_ref, ybuf_ref, yp_ref, ys_ref, pbuf_ref):
    del ybuf_ref
    g = pl.program_id(0)
    last = pl.num_programs(0) - 1
    n_dec = ps_ref.shape[0]
    pbuf_ref[...] = pp_ref[...]

    @pl.when(g == last)
    def _():
        pbuf_ref[0:n_dec, :] = ps_ref[...]

    out = _combine_math(h_ref, yg_ref, mf_ref, pbuf_ref[...], gple_ref, wpg_ref, wpp_ref,
                        gfin_ref)

    @pl.when(g < last)
    def _():
        yp_ref[...] = out

    @pl.when(g == last)
    def _():
        ys_ref[...] = out[0:n_dec, :]


def _combine(h, tile0, yg, mf, y_seq, lw, with_rows):
    tm = TOK_ROWS
    n_tiles = pl.cdiv(yg.shape[1], tm)
    last_seq_tile = y_seq.shape[0] // tm - 1
    pp, layer = lw['p_seq']
    n_dec = lw['p_row'][0].shape[-2]
    weights = [lw[k] for k in (('p_row',) if with_rows else ())
               + ('g_ple', 'w_ple_gate', 'w_ple_proj', 'g_final')]
    seq_tile = lambda g: (jnp.minimum(tile0 + g, last_seq_tile), 0)
    n_in = 4 + len(weights)
    out = pl.pallas_call(
        _combine_tail_kernel if with_rows else _combine_seq_kernel,
        grid=(n_tiles,),
        in_specs=[pl.BlockSpec((tm, D_MODEL), lambda g: (tile0 + g, 0)),
                  pl.BlockSpec((TOP_K, tm, D_MODEL), lambda g: (0, g, 0)),
                  pl.BlockSpec((tm, N_EXPERTS), lambda g: (g, 0)),
                  pl.BlockSpec((None, tm, D_PLE), lambda g: (layer,) + seq_tile(g))]
        + [_layer_spec(*w) for w in weights]
        + [pl.BlockSpec(memory_space=pl.ANY)],
        out_specs=[pl.BlockSpec((tm, D_MODEL), seq_tile)]
        + ([_whole_out_spec((n_dec, D_MODEL))] if with_rows else []),
        out_shape=[jax.ShapeDtypeStruct(y_seq.shape, F32)]
        + ([jax.ShapeDtypeStruct((n_dec, D_MODEL), F32)] if with_rows else []),
        scratch_shapes=[pltpu.VMEM((tm, D_PLE), F32)] if with_rows else [],
        input_output_aliases={n_in: 0},
        compiler_params=_tc_params(),
        name="combine",
    )(h, yg, mf, pp, *[w for w, _ in weights], y_seq)
    return out if with_rows else (out[0], None)


def _moe_layer(h, y_seq, lw):
    n_tok = h.shape[0]
    tm = TOK_ROWS
    te = EXPERT_ROWS
    tiles = pl.cdiv(n_tok, tm)
    bounds = [tiles * c // MOE_CHUNKS for c in range(MOE_CHUNKS + 1)]
    y_rows = None
    for c in range(MOE_CHUNKS):
        tile0 = bounds[c]
        n = min(bounds[c + 1] * tm, n_tok) - tile0 * tm
        mi, mf, counts = _router(h, tile0, n, lw)

        cnt = counts[0].astype(I32)
        padded = (cnt + te - 1) // te * te
        ends = jnp.cumsum(padded)
        starts = ends - padded
        experts = jnp.arange(N_EXPERTS, dtype=I32)
        start_of = lambda e: jnp.sum(jnp.where(e[:, None] == experts, starts, 0), axis=-1)
        dest = jnp.concatenate([start_of(mi[:, 0]) + mi[:, 2], start_of(mi[:, 1]) + mi[:, 3]])
        n_tiles = pl.cdiv(TOP_K * n + N_EXPERTS * (te - 1), te)
        tile_start = jnp.arange(n_tiles, dtype=I32) * te
        last_used = jnp.max(jnp.where(padded > 0, experts, 0))
        tile_expert = jnp.minimum(
            jnp.sum(tile_start[:, None] >= ends[None, :], axis=-1).astype(I32), last_used)
        n_valid = (ends[-1:] // te).astype(I32)

        xs = _sc_scatter_rows(h, tile0 * tm, dest, n_tiles * te)
        y = _experts(xs, tile_expert, n_valid, lw)
        yg = _sc_gather_rows(y, dest).reshape(TOP_K, n, D_MODEL)
        y_seq, rows = _combine(h, tile0, yg, mf, y_seq, lw, with_rows=c == MOE_CHUNKS - 1)
        y_rows = rows if rows is not None else y_rows
    return y_seq, y_rows


def kernel(x_prompt, x_sample, state_conv_a, state_conv_c, p_prompt, p_sample, g_mix, w_in, w_conv_a, w_s, b_s, g_ln_b, b_ln_b, w_conv_c, b_conv_c, g_ln_c, b_ln_c, g_out, w_o, g_ffn, w_ff_gate, w_ff_up, w_ff_down, w_router, w_ex_gate, w_ex_up, w_ex_down, g_ple, w_ple_gate, w_ple_proj, g_final):
    depth = g_mix.shape[0]
    n_seq, seq, _ = x_prompt.shape
    n_dec = x_sample.shape[0]
    n_prompt = n_seq * seq
    assert depth == 2 and x_sample.shape[1] == 1
    assert seq % TOK_ROWS == 0 and TOK_ROWS % n_dec == 0 and w_ff_gate.shape[-1] % FF_COLS == 0

    vec = lambda x: x.reshape(x.shape[0], 1, -1)
    per_layer = {
        'g_mix': vec(g_mix), 'w_in': w_in.astype(BF16), 'w_in_f32': w_in, 'w_conv_a': w_conv_a,
        'w_s_cat': jnp.transpose(w_s, (0, 2, 1, 3)).reshape(depth, CHUNK, N_HEADS_B * CHUNK),
        'b_s_full': jnp.repeat(jnp.swapaxes(b_s, 1, 2), HEAD_DIM, axis=2),
        'g_ln_b': vec(g_ln_b), 'b_ln_b': vec(b_ln_b), 'w_conv_c': w_conv_c,
        'b_conv_c': vec(b_conv_c), 'g_ln_c': vec(g_ln_c), 'b_ln_c': vec(b_ln_c),
        'g_out': vec(g_out), 'w_o': w_o.astype(BF16), 'w_o_f32': w_o,
        'g_ffn': vec(g_ffn), 'g_ple': vec(g_ple),
        'w_ple_gate': w_ple_gate.astype(BF16), 'w_ple_proj': w_ple_proj.astype(BF16),
        'w_ple_gate_f32': w_ple_gate, 'w_ple_proj_f32': w_ple_proj,
        'state_a': jnp.swapaxes(state_conv_a, 1, 2), 'state_c': jnp.swapaxes(state_conv_c, 1, 2),
        'p_seq': p_prompt.reshape(depth, n_prompt, D_PLE),
        'p_row': p_sample.reshape(depth, n_dec, D_PLE),
    }
    per_dense = {'w_ff_gate': w_ff_gate.astype(BF16), 'w_ff_up': w_ff_up.astype(BF16),
                 'w_ff_down': w_ff_down.astype(BF16), 'w_ff_gate_f32': w_ff_gate,
                 'w_ff_up_f32': w_ff_up, 'w_ff_down_f32': w_ff_down}
    per_moe = {'w_router': w_router, 'w_ex_gate': w_ex_gate.astype(BF16),
               'w_ex_up': w_ex_up.astype(BF16), 'w_ex_down': w_ex_down.astype(BF16)}

    hp = x_prompt.reshape(n_prompt, D_MODEL)
    hs = x_sample.reshape(n_dec, D_MODEL)
    outs = {k: [] for k in ('a_p', 'a_s', 'c_p', 'c_s', 'v_s')}
    for i in range(depth):
        is_expert_layer = i % 2 == 1
        lw = {k: (v, i) for k, v in per_layer.items()}
        lw.update({k: (v, i // 2) for k, v in (per_moe if is_expert_layer else per_dense).items()})
        lw['g_final'] = (g_final.reshape(1, 1, -1), 0)

        hs, a_s, c_s, v_s = _mixer_row(hs, 0, lw)
        mixer_in = hp
        hp, a_p, c_p = _mixer_seq(hp, hs if is_expert_layer else None, lw, n_seq, seq)
        for k, v in zip(('a_p', 'c_p', 'a_s', 'c_s', 'v_s'), (a_p, c_p, a_s, c_s, v_s)):
            outs[k].append(v)
        if not is_expert_layer:
            hp = _ffn_dense(hp, lw)
            hs = _ffn_row(hs, lw)
        else:
            y_prompt, y_sample = _moe_layer(hp, mixer_in, lw)

    return (y_prompt.reshape(x_prompt.shape), y_sample.reshape(x_sample.shape),
            jnp.stack(outs['a_p']), jnp.swapaxes(jnp.stack(outs['a_s']), 1, 2),
            jnp.stack(outs['c_p']), jnp.swapaxes(jnp.stack(outs['c_s']), 1, 2),
            jnp.stack(outs['v_s']).reshape(depth, n_dec, 1, D_B))
```

```python
import functools

import jax
import jax.numpy as jnp
from jax import lax
from jax.experimental import pallas as pl
from jax.experimental.pallas import tpu as pltpu
from jax.experimental.pallas import tpu_sc as plsc

D_MODEL = 1024
HEAD_DIM = 64
D_A = 384
D_B = 256
D_C = 384
N_HEADS_B = D_B // HEAD_DIM
CONV_A = 3
CONV_C = 31
CHUNK = 128
D_IN = 3 * D_A + 2 * D_B + 2 * D_C
D_PLE = 256
N_EXPERTS = 8
TOP_K = 2
EPS = 1e-6

O1, O2, O3 = D_A, 2 * D_A, 3 * D_A
O4 = O3 + 2 * D_B

F32 = jnp.float32
BF16 = jnp.bfloat16
I32 = jnp.int32

VMEM_LIMIT_BYTES = 56 * 1024 * 1024
SUBLANES = 8

TOK_ROWS = 512
CONV_ROWS = 64
A_HALO = 8
C_HALO = 32
EXPERT_ROWS = 512
FF_COLS = 256
MOE_CHUNKS = 2

SC_CORES = 2
SC_WORKERS = 32
SC_WINDOW = 48


def _rms(x, g):
    return x * lax.rsqrt(jnp.mean(x * x, axis=-1, keepdims=True) + EPS) * g


def _ln(x, g, b):
    mu = jnp.mean(x, axis=-1, keepdims=True)
    xc = x - mu
    var = jnp.mean(xc * xc, axis=-1, keepdims=True)
    return xc * lax.rsqrt(var + EPS) * g + b


def _dot(a, b):
    return jnp.dot(a.astype(BF16), b, preferred_element_type=F32)


def _split(x):
    hi = x.astype(BF16)
    return hi, (x - hi.astype(F32)).astype(BF16)


def _dot_full(a, b):
    a_hi, a_lo = _split(a)
    b_hi, b_lo = _split(b)
    return (jnp.dot(jnp.concatenate([a_hi, a_lo], axis=1),
                    jnp.concatenate([b_hi, b_hi], axis=0), preferred_element_type=F32)
            + jnp.dot(a_hi, b_lo, preferred_element_type=F32))


def _const_spec(shape):
    return pl.BlockSpec(shape, lambda *_: (0,) * len(shape), pipeline_mode=pl.Buffered(1))


def _layer_spec(stacked, layer):
    idx = (layer,) if isinstance(layer, int) else tuple(layer)
    rest = stacked.shape[len(idx):]
    return pl.BlockSpec((None,) * len(idx) + rest, lambda *_: idx + (0,) * len(rest),
                        pipeline_mode=pl.Buffered(1))


def _whole_out_spec(shape):
    return pl.BlockSpec(shape, lambda *_: (0,) * len(shape))


def _tc_params():
    return pltpu.CompilerParams(dimension_semantics=("arbitrary",),
                                vmem_limit_bytes=VMEM_LIMIT_BYTES)


def _mixer_tail(h, y_a, y_b, y_c, gout_ref, wo_ref, dot):
    y = jnp.concatenate([_rms(y_a, gout_ref[:, :D_A]),
                         _rms(y_b, gout_ref[:, D_A:D_A + D_B]),
                         _rms(y_c, gout_ref[:, D_A + D_B:])], axis=-1)
    return h + dot(y, wo_ref[...])


def _mixer_seq_kernel(hp_ref, hs_ref, gmix_ref, win_ref, wca_ref, wscat_ref, bsfull_ref,
                      glnb_ref, blnb_ref, wcc_ref, bcc_ref, glnc_ref, blnc_ref, gout_ref,
                      wo_ref, ride_ref, hout_ref, newa_ref, newc_ref, ride_out_ref,
                      qext_ref, gext_ref, gshift_ref, conv_ref, *, tiles_per_seq, n_seq_tiles):
    g = pl.program_id(0)
    tm = hp_ref.shape[0]

    @pl.when(g < n_seq_tiles)
    def _sequence_tile():
        ride_out_ref[...] = ride_ref[...].astype(BF16)

        @pl.when(g % tiles_per_seq == 0)
        def _():
            qext_ref[0:A_HALO, :] = jnp.zeros((A_HALO, D_A), F32)
            gext_ref[0:C_HALO, :] = jnp.zeros((C_HALO, D_C), F32)

        h = hp_ref[...]
        z = _dot(_rms(h, gmix_ref[...]), win_ref[...])

        qext_ref[A_HALO:A_HALO + tm, :] = z[:, O1:O2] * z[:, O2:O3]
        conv_a = jnp.zeros((tm, D_A), F32)
        for k in range(CONV_A):
            off = A_HALO - (CONV_A - 1) + k
            conv_a = conv_a + wca_ref[k:k + 1, :] * qext_ref[off:off + tm, :]
        y_a = z[:, :O1] * conv_a
        last_q = qext_ref[A_HALO + tm - (CONV_A - 1):A_HALO + tm, :]
        newa_ref[...] = last_q
        qext_ref[A_HALO - (CONV_A - 1):A_HALO, :] = last_q

        zb = jax.nn.gelu(z[:, O3:O4])
        v = _ln(zb[:, D_B:], glnb_ref[...], blnb_ref[...])
        row = lax.broadcasted_iota(I32, (CHUNK, N_HEADS_B * CHUNK), 0)
        col = lax.broadcasted_iota(I32, (CHUNK, N_HEADS_B * CHUNK), 1)
        w_tril = jnp.where((col % CHUNK) <= row, wscat_ref[...], 0.0).astype(BF16)
        lane_head = lax.broadcasted_iota(I32, (CHUNK, D_B), 1) // HEAD_DIM
        s_chunks = []
        for c in range(tm // CHUNK):
            vc = v[c * CHUNK:(c + 1) * CHUNK, :]
            vstack = jnp.concatenate(
                [jnp.where(lane_head == hd, vc, 0.0) for hd in range(N_HEADS_B)], axis=0)
            s_chunks.append(_dot(w_tril, vstack.astype(BF16)) + bsfull_ref[...])
        y_b = zb[:, :D_B] * jnp.concatenate(s_chunks, axis=0)

        gext_ref[C_HALO:C_HALO + tm, :] = (z[:, O4:O4 + D_C]
                                           * jax.nn.sigmoid(z[:, O4 + D_C:]))
        n_shift = gshift_ref.shape[1]
        for s in range(1, SUBLANES):
            gshift_ref[s - 1] = gext_ref[s:s + n_shift, :]
        base = C_HALO - (CONV_C - 1)
        for r0 in range(0, tm, CONV_ROWS):
            acc = jnp.zeros((CONV_ROWS, D_C), F32)
            for k in range(CONV_C):
                lo = (base + k) // SUBLANES * SUBLANES + r0
                s = (base + k) % SUBLANES
                window = (gext_ref[lo:lo + CONV_ROWS, :] if s == 0
                          else gshift_ref[s - 1, lo:lo + CONV_ROWS, :])
                acc = acc + wcc_ref[k:k + 1, :] * window
            conv_ref[r0:r0 + CONV_ROWS, :] = acc
        y_c = jax.nn.silu(_ln(conv_ref[...] + bcc_ref[...], glnc_ref[...], blnc_ref[...]))
        last_g = gext_ref[C_HALO + tm - (CONV_C - 1):C_HALO + tm, :]
        newc_ref[...] = last_g
        gext_ref[C_HALO - (CONV_C - 1):C_HALO, :] = last_g

        hout_ref[...] = _mixer_tail(h, y_a, y_b, y_c, gout_ref, wo_ref, _dot)

    @pl.when(g == n_seq_tiles)
    def _append_sample_rows():
        hout_ref[0:hs_ref.shape[0], :] = hs_ref[...]


def _ride_specs(ride, n_steps):
    rows = ride.shape[0] // n_steps
    assert rows * n_steps == ride.shape[0]
    slab = lambda g: (jnp.minimum(g, n_steps - 1), 0)
    return (pl.BlockSpec((rows, ride.shape[1]), slab), pl.BlockSpec((rows, ride.shape[1]), slab),
            jax.ShapeDtypeStruct(ride.shape, BF16))


def _mixer_seq(hp, hs, lw, n_seq, seq, ride):
    tm = TOK_ROWS
    tiles_per_seq = seq // tm
    n_seq_tiles = n_seq * tiles_per_seq
    join = hs is not None
    if not join:
        hs = jnp.zeros((SUBLANES, D_MODEL), F32)
    n_out = n_seq * seq + (hs.shape[0] if join else 0)
    weights = [lw[k] for k in ('g_mix', 'w_in', 'w_conv_a', 'w_s_cat', 'b_s_full', 'g_ln_b',
                               'b_ln_b', 'w_conv_c', 'b_conv_c', 'g_ln_c', 'b_ln_c', 'g_out',
                               'w_o')]
    seq_of = lambda g: jnp.minimum(g // tiles_per_seq, n_seq - 1)
    n_shift = tm + C_HALO - SUBLANES
    ride_in_spec, ride_out_spec, ride_out_shape = _ride_specs(ride, n_seq_tiles)
    return pl.pallas_call(
        functools.partial(_mixer_seq_kernel, tiles_per_seq=tiles_per_seq,
                          n_seq_tiles=n_seq_tiles),
        grid=(n_seq_tiles + int(join),),
        in_specs=[pl.BlockSpec((tm, D_MODEL), lambda g: (jnp.minimum(g, n_seq_tiles - 1), 0)),
                  _const_spec(hs.shape)]
        + [_layer_spec(*w) for w in weights] + [ride_in_spec],
        out_specs=[pl.BlockSpec((tm, D_MODEL), lambda g: (g, 0)),
                   pl.BlockSpec((None, CONV_A - 1, D_A), lambda g: (seq_of(g), 0, 0)),
                   pl.BlockSpec((None, CONV_C - 1, D_C), lambda g: (seq_of(g), 0, 0)),
                   ride_out_spec],
        out_shape=[jax.ShapeDtypeStruct((n_out, D_MODEL), F32),
                   jax.ShapeDtypeStruct((n_seq, CONV_A - 1, D_A), F32),
                   jax.ShapeDtypeStruct((n_seq, CONV_C - 1, D_C), F32),
                   ride_out_shape],
        scratch_shapes=[pltpu.VMEM((A_HALO + tm, D_A), F32),
                        pltpu.VMEM((C_HALO + tm, D_C), F32),
                        pltpu.VMEM((SUBLANES - 1, n_shift, D_C), F32),
                        pltpu.VMEM((tm, D_C), F32)],
        compiler_params=_tc_params(),
        name="mixer_seq",
    )(hp, hs, *[w for w, _ in weights], ride)


def _mixer_row_kernel(h_ref, sa_ref, sc_ref, gmix_ref, win_ref, wca_ref, wscat_ref,
                      bsfull_ref, glnb_ref, blnb_ref, wcc_ref, bcc_ref, glnc_ref, blnc_ref,
                      gout_ref, wo_ref, hout_ref, newa_ref, newc_ref, v_ref):
    h = h_ref[...]
    z = _dot_full(_rms(h, gmix_ref[...]), win_ref[...])

    q = z[:, O1:O2] * z[:, O2:O3]
    conv_a = wca_ref[CONV_A - 1:CONV_A, :] * q
    for k in range(CONV_A - 1):
        conv_a = conv_a + wca_ref[k:k + 1, :] * sa_ref[k]
    y_a = z[:, :O1] * conv_a
    for k in range(CONV_A - 2):
        newa_ref[k] = sa_ref[k + 1]
    newa_ref[CONV_A - 2] = q

    zb = jax.nn.gelu(z[:, O3:O4])
    v = _ln(zb[:, D_B:], glnb_ref[...], blnb_ref[...])
    v_ref[...] = v
    w_diag0 = jnp.concatenate(
        [jnp.broadcast_to(wscat_ref[0:1, hd * CHUNK:hd * CHUNK + 1], (1, HEAD_DIM))
         for hd in range(N_HEADS_B)], axis=-1)
    y_b = zb[:, :D_B] * (w_diag0 * v + bsfull_ref[0:1, :])

    glu = z[:, O4:O4 + D_C] * jax.nn.sigmoid(z[:, O4 + D_C:])
    conv_c = wcc_ref[CONV_C - 1:CONV_C, :] * glu
    for k in range(CONV_C - 1):
        conv_c = conv_c + wcc_ref[k:k + 1, :] * sc_ref[k]
    y_c = jax.nn.silu(_ln(conv_c + bcc_ref[...], glnc_ref[...], blnc_ref[...]))
    for k in range(CONV_C - 2):
        newc_ref[k] = sc_ref[k + 1]
    newc_ref[CONV_C - 2] = glu

    hout_ref[...] = _mixer_tail(h, y_a, y_b, y_c, gout_ref, wo_ref, _dot_full)


def _mixer_row(h, h_block, lw):
    n_dec = lw['state_a'][0].shape[-2]
    weights = [lw[k] for k in ('state_a', 'state_c', 'g_mix', 'w_in_f32', 'w_conv_a', 'w_s_cat',
                               'b_s_full', 'g_ln_b', 'b_ln_b', 'w_conv_c', 'b_conv_c', 'g_ln_c',
                               'b_ln_c', 'g_out', 'w_o_f32')]
    return pl.pallas_call(
        _mixer_row_kernel,
        grid=(1,),
        in_specs=[pl.BlockSpec((n_dec, D_MODEL), lambda g: (h_block, 0))]
        + [_layer_spec(*w) for w in weights],
        out_specs=[_whole_out_spec((n_dec, D_MODEL)), _whole_out_spec((CONV_A - 1, n_dec, D_A)),
                   _whole_out_spec((CONV_C - 1, n_dec, D_C)), _whole_out_spec((n_dec, D_B))],
        out_shape=[jax.ShapeDtypeStruct((n_dec, D_MODEL), F32),
                   jax.ShapeDtypeStruct((CONV_A - 1, n_dec, D_A), F32),
                   jax.ShapeDtypeStruct((CONV_C - 1, n_dec, D_C), F32),
                   jax.ShapeDtypeStruct((n_dec, D_B), F32)],
        compiler_params=_tc_params(),
        name="mixer_row",
    )(h, *[w for w, _ in weights])


def _ple(h, p, gple_ref, wpg_ref, wpp_ref, dot):
    gate = jax.nn.sigmoid(dot(_rms(h, gple_ref[...]), wpg_ref[...]))
    return h + gate * dot(p, wpp_ref[...])


def _ffn_dense_kernel(h_ref, p_ref, gffn_ref, wg_ref, wu_ref, wd_ref, gple_ref, wpg_ref,
                      wpp_ref, ride_ref, out_ref, ride_out_ref):
    ride_out_ref[...] = ride_ref[...].astype(BF16)
    h = h_ref[...]
    xn = _rms(h, gffn_ref[...]).astype(BF16)
    a = jax.nn.silu(_dot(xn, wg_ref[...])) * _dot(xn, wu_ref[...])
    h = h + _dot(a, wd_ref[...])
    out_ref[...] = _ple(h, p_ref[...], gple_ref, wpg_ref, wpp_ref, _dot)


def _ffn_dense(h, lw, ride):
    n_tok = h.shape[0]
    tm = TOK_ROWS
    p, layer = lw['p_seq']
    weights = [lw[k] for k in ('g_ffn', 'w_ff_gate', 'w_ff_up', 'w_ff_down', 'g_ple',
                               'w_ple_gate', 'w_ple_proj')]
    ride_in_spec, ride_out_spec, ride_out_shape = _ride_specs(ride, n_tok // tm)
    return pl.pallas_call(
        _ffn_dense_kernel,
        grid=(n_tok // tm,),
        in_specs=[pl.BlockSpec((tm, D_MODEL), lambda g: (g, 0)),
                  pl.BlockSpec((None, tm, D_PLE), lambda g: (layer, g, 0))]
        + [_layer_spec(*w) for w in weights] + [ride_in_spec],
        out_specs=[pl.BlockSpec((tm, D_MODEL), lambda g: (g, 0)), ride_out_spec],
        out_shape=[jax.ShapeDtypeStruct(h.shape, F32), ride_out_shape],
        compiler_params=_tc_params(),
        name="ffn_dense",
    )(h, p, *[w for w, _ in weights], ride)


def _ffn_row_kernel(h_ref, p_ref, gffn_ref, wg_ref, wu_ref, wd_ref, gple_ref, wpg_ref,
                    wpp_ref, out_ref, xn_ref, acc_ref):
    j = pl.program_id(0)

    @pl.when(j == 0)
    def _():
        h = h_ref[...]
        xn_ref[...] = _rms(h, gffn_ref[...])
        acc_ref[...] = h

    xn = xn_ref[...]
    a = jax.nn.silu(_dot_full(xn, wg_ref[...])) * _dot_full(xn, wu_ref[...])
    acc_ref[...] += _dot_full(a, wd_ref[...])

    @pl.when(j == pl.num_programs(0) - 1)
    def _():
        out_ref[...] = _ple(acc_ref[...], p_ref[...], gple_ref, wpg_ref, wpp_ref, _dot_full)


def _ffn_row(h, lw):
    n_dec = h.shape[0]
    d_ff = lw['w_ff_gate_f32'][0].shape[-1]
    ff = lw['w_ff_gate_f32'][1]
    return pl.pallas_call(
        _ffn_row_kernel,
        grid=(d_ff // FF_COLS,),
        in_specs=[_const_spec(h.shape), _layer_spec(*lw['p_row']), _layer_spec(*lw['g_ffn']),
                  pl.BlockSpec((None, D_MODEL, FF_COLS), lambda j: (ff, 0, j)),
                  pl.BlockSpec((None, D_MODEL, FF_COLS), lambda j: (ff, 0, j)),
                  pl.BlockSpec((None, FF_COLS, D_MODEL), lambda j: (ff, j, 0)),
                  _layer_spec(*lw['g_ple']), _layer_spec(*lw['w_ple_gate_f32']),
                  _layer_spec(*lw['w_ple_proj_f32'])],
        out_specs=_whole_out_spec(h.shape),
        out_shape=jax.ShapeDtypeStruct(h.shape, F32),
        scratch_shapes=[pltpu.VMEM((n_dec, D_MODEL), F32), pltpu.VMEM((n_dec, D_MODEL), F32)],
        compiler_params=_tc_params(),
        name="ffn_row",
    )(h, *[lw[k][0] for k in ('p_row', 'g_ffn', 'w_ff_gate_f32', 'w_ff_up_f32',
                              'w_ff_down_f32', 'g_ple', 'w_ple_gate_f32', 'w_ple_proj_f32')])


def _router_kernel(h_ref, gffn_ref, wr_ref, mi_ref, mf_ref, cnt_ref, carry_ref, *, n_tok):
    g = pl.program_id(0)
    tm = h_ref.shape[0]

    @pl.when(g == 0)
    def _():
        carry_ref[...] = jnp.zeros(carry_ref.shape, F32)

    valid = (g * tm + lax.broadcasted_iota(I32, (tm, 1), 0)) < n_tok
    xn = _rms(jnp.where(valid, h_ref[...], 0.0), gffn_ref[...])
    logits = _dot_full(xn, wr_ref[...])
    lane = lax.broadcasted_iota(I32, logits.shape, 1)
    m1 = jnp.max(logits, axis=-1, keepdims=True)
    i1 = jnp.min(jnp.where(logits == m1, lane, N_EXPERTS), axis=-1, keepdims=True)
    rest = jnp.where(lane == i1, -jnp.inf, logits)
    m2 = jnp.max(rest, axis=-1, keepdims=True)
    i2 = jnp.min(jnp.where(rest == m2, lane, N_EXPERTS), axis=-1, keepdims=True)
    e2 = jnp.exp(m2 - m1)
    denom = 1.0 + e2
    w1 = 1.0 / denom
    w2 = e2 / denom

    oh1 = jnp.where((lane == i1) & valid, 1.0, 0.0)
    oh2 = jnp.where((lane == i2) & valid, 1.0, 0.0)
    member = oh1 + oh2
    r = lax.broadcasted_iota(I32, (tm, tm), 0)
    c = lax.broadcasted_iota(I32, (tm, tm), 1)
    before = jnp.where(c < r, 1.0, 0.0).astype(BF16)
    pos = _dot(before, member.astype(BF16)) + carry_ref[...]
    pos1 = jnp.sum(oh1 * pos, axis=-1, keepdims=True).astype(I32)
    pos2 = jnp.sum(oh2 * pos, axis=-1, keepdims=True).astype(I32)
    carry_ref[...] = carry_ref[...] + jnp.sum(member, axis=0, keepdims=True)
    cnt_ref[...] = carry_ref[...]

    mi_ref[...] = jnp.where(lane == 0, i1, jnp.where(lane == 1, i2,
                            jnp.where(lane == 2, pos1, jnp.where(lane == 3, pos2, 0))))
    mf_ref[...] = jnp.where(lane == 0, w1, jnp.where(lane == 1, w2, 0.0))


def _router(h, tile0, n_tok, lw):
    tm = TOK_ROWS
    return pl.pallas_call(
        functools.partial(_router_kernel, n_tok=n_tok),
        grid=(pl.cdiv(n_tok, tm),),
        in_specs=[pl.BlockSpec((tm, D_MODEL), lambda g: (tile0 + g, 0)),
                  _layer_spec(*lw['g_ffn']), _layer_spec(*lw['w_router'])],
        out_specs=[pl.BlockSpec((tm, N_EXPERTS), lambda g: (g, 0)),
                   pl.BlockSpec((tm, N_EXPERTS), lambda g: (g, 0)),
                   _whole_out_spec((1, N_EXPERTS))],
        out_shape=[jax.ShapeDtypeStruct((n_tok, N_EXPERTS), I32),
                   jax.ShapeDtypeStruct((n_tok, N_EXPERTS), F32),
                   jax.ShapeDtypeStruct((1, N_EXPERTS), F32)],
        scratch_shapes=[pltpu.VMEM((1, N_EXPERTS), F32)],
        compiler_params=_tc_params(),
        name="router",
    )(h, lw['g_ffn'][0], lw['w_router'][0])


def _sc_chunk(n_rows):
    per_worker = pl.cdiv(n_rows, SC_WORKERS)
    return pl.cdiv(per_worker, SC_WINDOW) * SC_WINDOW


def _sc_worker_base(n_rows, chunk):
    wid = lax.axis_index("s") * SC_CORES + lax.axis_index("c")
    return jnp.minimum(wid * chunk, n_rows - chunk)


def _sc_scatter_rows(x, row0, dest, n_out):
    n = dest.shape[0] // TOP_K
    chunk = _sc_chunk(n)
    mesh = plsc.VectorSubcoreMesh(core_axis_name="c", subcore_axis_name="s")

    @functools.partial(
        pl.kernel, mesh=mesh,
        out_type=jax.ShapeDtypeStruct((n_out, D_MODEL), x.dtype),
        scratch_types=[pltpu.VMEM((SC_WINDOW,), I32) for _ in range(TOP_K)]
        + [pltpu.VMEM((SC_WINDOW, D_MODEL), x.dtype), pltpu.SemaphoreType.DMA],
        name="sc_scatter_rows",
    )
    def scatter(x_hbm, dest_hbm, out_hbm, idx0_v, idx1_v, rows_v, sem):
        base = _sc_worker_base(n, chunk)

        @pl.loop(0, chunk // SC_WINDOW)
        def _(j):
            off = pl.multiple_of(base + j * SC_WINDOW, 8)
            pltpu.sync_copy(dest_hbm.at[pl.ds(off, SC_WINDOW)], idx0_v)
            pltpu.sync_copy(dest_hbm.at[pl.ds(n + off, SC_WINDOW)], idx1_v)
            pltpu.sync_copy(x_hbm.at[pl.ds(row0 + off, SC_WINDOW)], rows_v)
            first = pltpu.async_copy(rows_v, out_hbm.at[idx0_v], sem)
            second = pltpu.async_copy(rows_v, out_hbm.at[idx1_v], sem)
            first.wait()
            second.wait()

    return scatter(x, dest)


def _sc_gather_rows(y, idx):
    n = idx.shape[0]
    chunk = _sc_chunk(n)
    mesh = plsc.VectorSubcoreMesh(core_axis_name="c", subcore_axis_name="s")

    @functools.partial(
        pl.kernel, mesh=mesh,
        out_type=jax.ShapeDtypeStruct((n, D_MODEL), y.dtype),
        scratch_types=[pltpu.VMEM((SC_WINDOW,), I32),
                       pltpu.VMEM((SC_WINDOW, D_MODEL), y.dtype), pltpu.SemaphoreType.DMA],
        name="sc_gather_rows",
    )
    def gather(y_hbm, idx_hbm, out_hbm, idx_v, rows_v, sem):
        base = _sc_worker_base(n, chunk)

        @pl.loop(0, chunk // SC_WINDOW)
        def _(j):
            off = pl.multiple_of(base + j * SC_WINDOW, 8)
            pltpu.sync_copy(idx_hbm.at[pl.ds(off, SC_WINDOW)], idx_v)
            pltpu.async_copy(y_hbm.at[idx_v], rows_v, sem).wait()
            pltpu.sync_copy(rows_v, out_hbm.at[pl.ds(off, SC_WINDOW)])

    return gather(y, idx)


def _expert_kernel(tile_expert_ref, n_valid_ref, xs_ref, gffn_ref, wg_ref, wu_ref, wd_ref,
                   y_ref):
    @pl.when(pl.program_id(0) < n_valid_ref[0])
    def _():
        xn = _rms(xs_ref[...], gffn_ref[...]).astype(BF16)
        a = jax.nn.silu(_dot(xn, wg_ref[...])) * _dot(xn, wu_ref[...])
        y_ref[...] = _dot(a, wd_ref[...])


def _experts(xs, tile_expert, n_valid, lw):
    n_slots = xs.shape[0]
    te = EXPERT_ROWS
    d_exp = lw['w_ex_gate'][0].shape[-1]
    moe = lw['w_ex_gate'][1]
    row_block = lambda g, tex, nv: (jnp.minimum(g, nv[0] - 1), 0)
    w_block = lambda g, tex, nv: (moe, tex[g], 0, 0)
    return pl.pallas_call(
        _expert_kernel,
        grid_spec=pltpu.PrefetchScalarGridSpec(
            num_scalar_prefetch=2,
            grid=(n_slots // te,),
            in_specs=[pl.BlockSpec((te, D_MODEL), row_block),
                      _layer_spec(*lw['g_ffn']),
                      pl.BlockSpec((None, None, D_MODEL, d_exp), w_block),
                      pl.BlockSpec((None, None, D_MODEL, d_exp), w_block),
                      pl.BlockSpec((None, None, d_exp, D_MODEL), w_block)],
            out_specs=pl.BlockSpec((te, D_MODEL), row_block)),
        out_shape=jax.ShapeDtypeStruct((n_slots, D_MODEL), F32),
        compiler_params=_tc_params(),
        name="experts",
    )(tile_expert, n_valid, xs, *[lw[k][0] for k in ('g_ffn', 'w_ex_gate', 'w_ex_up',
                                                        'w_ex_down')])


def _combine_math(h_ref, yg_ref, mf_ref, p, gple_ref, wpg_ref, wpp_ref, gfin_ref):
    gates = mf_ref[...]
    h = h_ref[...] + (gates[:, 0:1] * yg_ref[0] + gates[:, 1:2] * yg_ref[1])
    return _rms(_ple(h, p, gple_ref, wpg_ref, wpp_ref, _dot), gfin_ref[...])


def _combine_seq_kernel(h_ref, yg_ref, mf_ref, pp_ref, gple_ref, wpg_ref, wpp_ref, gfin_ref,
                        ybuf_ref, yp_ref):
    del ybuf_ref
    yp_ref[...] = _combine_math(h_ref, yg_ref, mf_ref, pp_ref[...], gple_ref, wpg_ref,
                                wpp_ref, gfin_ref)


def _combine_tail_kernel(h_ref, yg_ref, mf_ref, pp_ref, ps_ref, gple_ref, wpg_ref, wpp_ref,
                         gfin_ref, ybuf_ref, yp_ref, ys_ref, pbuf_ref):
    del ybuf_ref
    g = pl.program_id(0)
    last = pl.num_programs(0) - 1
    n_dec = ps_ref.shape[0]
    pbuf_ref[...] = pp_ref[...]

    @pl.when(g == last)
    def _():
        pbuf_ref[0:n_dec, :] = ps_ref[...]

    out = _combine_math(h_ref, yg_ref, mf_ref, pbuf_ref[...], gple_ref, wpg_ref, wpp_ref,
                        gfin_ref)

    @pl.when(g < last)
    def _():
        yp_ref[...] = out

    @pl.when(g == last)
    def _():
        ys_ref[...] = out[0:n_dec, :]


def _combine(h, tile0, yg, mf, y_seq, lw, with_rows):
    tm = TOK_ROWS
    n_tiles = pl.cdiv(yg.shape[1], tm)
    last_seq_tile = y_seq.shape[0] // tm - 1
    pp, layer = lw['p_seq']
    n_dec = lw['p_row'][0].shape[-2]
    weights = [lw[k] for k in (('p_row',) if with_rows else ())
               + ('g_ple', 'w_ple_gate', 'w_ple_proj', 'g_final')]
    seq_tile = lambda g: (jnp.minimum(tile0 + g, last_seq_tile), 0)
    n_in = 4 + len(weights)
    out = pl.pallas_call(
        _combine_tail_kernel if with_rows else _combine_seq_kernel,
        grid=(n_tiles,),
        in_specs=[pl.BlockSpec((tm, D_MODEL), lambda g: (tile0 + g, 0)),
                  pl.BlockSpec((TOP_K, tm, D_MODEL), lambda g: (0, g, 0)),
                  pl.BlockSpec((tm, N_EXPERTS), lambda g: (g, 0)),
                  pl.BlockSpec((None, tm, D_PLE), lambda g: (layer,) + seq_tile(g))]
        + [_layer_spec(*w) for w in weights]
        + [pl.BlockSpec(memory_space=pl.ANY)],
        out_specs=[pl.BlockSpec((tm, D_MODEL), seq_tile)]
        + ([_whole_out_spec((n_dec, D_MODEL))] if with_rows else []),
        out_shape=[jax.ShapeDtypeStruct(y_seq.shape, F32)]
        + ([jax.ShapeDtypeStruct((n_dec, D_MODEL), F32)] if with_rows else []),
        scratch_shapes=[pltpu.VMEM((tm, D_PLE), F32)] if with_rows else [],
        input_output_aliases={n_in: 0},
        compiler_params=_tc_params(),
        name="combine",
    )(h, yg, mf, pp, *[w for w, _ in weights], y_seq)
    return out if with_rows else (out[0], None)


def _moe_layer(h, y_seq, lw):
    n_tok = h.shape[0]
    tm = TOK_ROWS
    te = EXPERT_ROWS
    tiles = pl.cdiv(n_tok, tm)
    bounds = [tiles * c // MOE_CHUNKS for c in range(MOE_CHUNKS + 1)]
    y_rows = None
    for c in range(MOE_CHUNKS):
        tile0 = bounds[c]
        n = min(bounds[c + 1] * tm, n_tok) - tile0 * tm
        mi, mf, counts = _router(h, tile0, n, lw)

        cnt = counts[0].astype(I32)
        padded = (cnt + te - 1) // te * te
        ends = jnp.cumsum(padded)
        starts = ends - padded
        experts = jnp.arange(N_EXPERTS, dtype=I32)
        start_of = lambda e: jnp.sum(jnp.where(e[:, None] == experts, starts, 0), axis=-1)
        dest = jnp.concatenate([start_of(mi[:, 0]) + mi[:, 2], start_of(mi[:, 1]) + mi[:, 3]])
        n_tiles = pl.cdiv(TOP_K * n + N_EXPERTS * (te - 1), te)
        tile_start = jnp.arange(n_tiles, dtype=I32) * te
        last_used = jnp.max(jnp.where(padded > 0, experts, 0))
        tile_expert = jnp.minimum(
            jnp.sum(tile_start[:, None] >= ends[None, :], axis=-1).astype(I32), last_used)
        n_valid = (ends[-1:] // te).astype(I32)

        xs = _sc_scatter_rows(h, tile0 * tm, dest, n_tiles * te)
        y = _experts(xs, tile_expert, n_valid, lw)
        yg = _sc_gather_rows(y, dest).reshape(TOP_K, n, D_MODEL)
        y_seq, rows = _combine(h, tile0, yg, mf, y_seq, lw, with_rows=c == MOE_CHUNKS - 1)
        y_rows = rows if rows is not None else y_rows
    return y_seq, y_rows


def kernel(x_prompt, x_sample, state_conv_a, state_conv_c, p_prompt, p_sample, g_mix, w_in, w_conv_a, w_s, b_s, g_ln_b, b_ln_b, w_conv_c, b_conv_c, g_ln_c, b_ln_c, g_out, w_o, g_ffn, w_ff_gate, w_ff_up, w_ff_down, w_router, w_ex_gate, w_ex_up, w_ex_down, g_ple, w_ple_gate, w_ple_proj, g_final):
    depth = g_mix.shape[0]
    n_seq, seq, _ = x_prompt.shape
    n_dec = x_sample.shape[0]
    n_prompt = n_seq * seq
    assert depth == 2 and x_sample.shape[1] == 1
    assert seq % TOK_ROWS == 0 and TOK_ROWS % n_dec == 0 and w_ff_gate.shape[-1] % FF_COLS == 0

    vec = lambda x: x.reshape(x.shape[0], 1, -1)
    per_layer = {
        'g_mix': vec(g_mix), 'w_in': w_in.astype(BF16), 'w_in_f32': w_in, 'w_conv_a': w_conv_a,
        'w_s_cat': jnp.transpose(w_s, (0, 2, 1, 3)).reshape(depth, CHUNK, N_HEADS_B * CHUNK),
        'b_s_full': jnp.repeat(jnp.swapaxes(b_s, 1, 2), HEAD_DIM, axis=2),
        'g_ln_b': vec(g_ln_b), 'b_ln_b': vec(b_ln_b), 'w_conv_c': w_conv_c,
        'b_conv_c': vec(b_conv_c), 'g_ln_c': vec(g_ln_c), 'b_ln_c': vec(b_ln_c),
        'g_out': vec(g_out), 'w_o': w_o.astype(BF16), 'w_o_f32': w_o,
        'g_ffn': vec(g_ffn), 'g_ple': vec(g_ple),
        'w_ple_gate': w_ple_gate.astype(BF16), 'w_ple_proj': w_ple_proj.astype(BF16),
        'w_ple_gate_f32': w_ple_gate, 'w_ple_proj_f32': w_ple_proj,
        'state_a': jnp.swapaxes(state_conv_a, 1, 2), 'state_c': jnp.swapaxes(state_conv_c, 1, 2),
        'p_seq': p_prompt.reshape(depth, n_prompt, D_PLE),
        'p_row': p_sample.reshape(depth, n_dec, D_PLE),
    }
    per_dense = {'w_ff_gate': w_ff_gate.astype(BF16), 'w_ff_up': w_ff_up.astype(BF16),
                 'w_ff_down': w_ff_down.astype(BF16), 'w_ff_gate_f32': w_ff_gate,
                 'w_ff_up_f32': w_ff_up, 'w_ff_down_f32': w_ff_down}
    per_moe = {'w_router': w_router}
    to_convert = [('w_ex_gate', w_ex_gate), ('w_ex_up', w_ex_up), ('w_ex_down', w_ex_down)]

    def with_ride(call):
        name, w = to_convert.pop(0)
        *outputs, w_bf16 = call(w.reshape(-1, w.shape[-1]))
        per_moe[name] = w_bf16.reshape(w.shape)
        return outputs

    hp = x_prompt.reshape(n_prompt, D_MODEL)
    hs = x_sample.reshape(n_dec, D_MODEL)
    outs = {k: [] for k in ('a_p', 'a_s', 'c_p', 'c_s', 'v_s')}
    for i in range(depth):
        is_expert_layer = i % 2 == 1
        lw = {k: (v, i) for k, v in per_layer.items()}
        lw.update({k: (v, i // 2) for k, v in (per_dense if not is_expert_layer else {}).items()})
        lw['g_final'] = (g_final.reshape(1, 1, -1), 0)

        hs, a_s, c_s, v_s = _mixer_row(hs, 0, lw)
        mixer_in = hp
        hp, a_p, c_p = with_ride(functools.partial(
            _mixer_seq, hp, hs if is_expert_layer else None, lw, n_seq, seq))
        for k, v in zip(('a_p', 'c_p', 'a_s', 'c_s', 'v_s'), (a_p, c_p, a_s, c_s, v_s)):
            outs[k].append(v)
        if not is_expert_layer:
            hp, = with_ride(functools.partial(_ffn_dense, hp, lw))
            hs = _ffn_row(hs, lw)
        else:
            lw.update({k: (v, i // 2) for k, v in per_moe.items()})
            y_prompt, y_sample = _moe_layer(hp, mixer_in, lw)

    return (y_prompt.reshape(x_prompt.shape), y_sample.reshape(x_sample.shape),
            jnp.stack(outs['a_p']), jnp.swapaxes(jnp.stack(outs['a_s']), 1, 2),
            jnp.stack(outs['c_p']), jnp.swapaxes(jnp.stack(outs['c_s']), 1, 2),
            jnp.stack(outs['v_s']).reshape(depth, n_dec, 1, D_B))
```

```python
import functools

import jax
import jax.numpy as jnp
from jax import lax
from jax.experimental import pallas as pl
from jax.experimental.pallas import tpu as pltpu
from jax.experimental.pallas import tpu_sc as plsc

D_MODEL = 1024
HEAD_DIM = 64
D_A = 384
D_B = 256
D_C = 384
N_HEADS_B = D_B // HEAD_DIM
CONV_A = 3
CONV_C = 31
CHUNK = 128
D_IN = 3 * D_A + 2 * D_B + 2 * D_C
D_PLE = 256
N_EXPERTS = 8
TOP_K = 2
EPS = 1e-6

O1, O2, O3 = D_A, 2 * D_A, 3 * D_A
O4 = O3 + 2 * D_B

F32 = jnp.float32
BF16 = jnp.bfloat16
I32 = jnp.int32

VMEM_LIMIT_BYTES = 56 * 1024 * 1024
SUBLANES = 8

TOK_ROWS = 512
CONV_ROWS = 64
A_HALO = 8
C_HALO = 32
EXPERT_ROWS = 256
FF_COLS = 256
MOE_CHUNKS = 2

SC_CORES = 2
SC_WORKERS = 32
SC_WINDOW = 48


def _rms(x, g):
    return x * lax.rsqrt(jnp.mean(x * x, axis=-1, keepdims=True) + EPS) * g


def _ln(x, g, b):
    mu = jnp.mean(x, axis=-1, keepdims=True)
    xc = x - mu
    var = jnp.mean(xc * xc, axis=-1, keepdims=True)
    return xc * lax.rsqrt(var + EPS) * g + b


def _dot(a, b):
    return jnp.dot(a.astype(BF16), b, preferred_element_type=F32)


def _split(x):
    hi = x.astype(BF16)
    return hi, (x - hi.astype(F32)).astype(BF16)


def _dot_full(a, b):
    a_hi, a_lo = _split(a)
    b_hi, b_lo = _split(b)
    m = a.shape[0]
    by_hi = jnp.dot(jnp.concatenate([a_hi, a_lo], axis=0), b_hi, preferred_element_type=F32)
    return by_hi[:m] + by_hi[m:] + jnp.dot(a_hi, b_lo, preferred_element_type=F32)


def _const_spec(shape):
    return pl.BlockSpec(shape, lambda *_: (0,) * len(shape), pipeline_mode=pl.Buffered(1))


def _layer_spec(stacked, layer):
    idx = (layer,) if isinstance(layer, int) else tuple(layer)
    rest = stacked.shape[len(idx):]
    return pl.BlockSpec((None,) * len(idx) + rest, lambda *_: idx + (0,) * len(rest),
                        pipeline_mode=pl.Buffered(1))


def _whole_out_spec(shape):
    return pl.BlockSpec(shape, lambda *_: (0,) * len(shape))


def _tc_params():
    return pltpu.CompilerParams(dimension_semantics=("arbitrary",),
                                vmem_limit_bytes=VMEM_LIMIT_BYTES)


def _normed_groups(y_a, y_b, y_c, gout_ref):
    return jnp.concatenate([_rms(y_a, gout_ref[:, :D_A]),
                            _rms(y_b, gout_ref[:, D_A:D_A + D_B]),
                            _rms(y_c, gout_ref[:, D_A + D_B:])], axis=-1)


def _mixer_seq_kernel(hp_ref, hs_ref, gmix_ref, win_ref, wca_ref, wscat_ref, bsfull_ref,
                      glnb_ref, blnb_ref, wcc_ref, bcc_ref, glnc_ref, blnc_ref, gout_ref,
                      wo_ref, ride_ref, hout_ref, newa_ref, newc_ref, ride_out_ref,
                      qext_ref, gext_ref, gshift_ref, conv_ref, *, tiles_per_seq, n_seq_tiles):
    g = pl.program_id(0)
    tm = hp_ref.shape[0]

    @pl.when(g < n_seq_tiles)
    def _sequence_tile():
        ride_out_ref[...] = ride_ref[...].astype(BF16)

        @pl.when(g % tiles_per_seq == 0)
        def _():
            qext_ref[0:A_HALO, :] = jnp.zeros((A_HALO, D_A), F32)
            gext_ref[0:C_HALO, :] = jnp.zeros((C_HALO, D_C), F32)

        h = hp_ref[...]
        z = _dot(_rms(h, gmix_ref[...]), win_ref[...])

        qext_ref[A_HALO:A_HALO + tm, :] = z[:, O1:O2] * z[:, O2:O3]
        conv_a = jnp.zeros((tm, D_A), F32)
        for k in range(CONV_A):
            off = A_HALO - (CONV_A - 1) + k
            conv_a = conv_a + wca_ref[k:k + 1, :] * qext_ref[off:off + tm, :]
        y_a = z[:, :O1] * conv_a
        last_q = qext_ref[A_HALO + tm - (CONV_A - 1):A_HALO + tm, :]
        newa_ref[...] = last_q
        qext_ref[A_HALO - (CONV_A - 1):A_HALO, :] = last_q

        zb = jax.nn.gelu(z[:, O3:O4])
        v = _ln(zb[:, D_B:], glnb_ref[...], blnb_ref[...])
        row = lax.broadcasted_iota(I32, (CHUNK, N_HEADS_B * CHUNK), 0)
        col = lax.broadcasted_iota(I32, (CHUNK, N_HEADS_B * CHUNK), 1)
        w_tril = jnp.where((col % CHUNK) <= row, wscat_ref[...], 0.0).astype(BF16)
        lane_head = lax.broadcasted_iota(I32, (CHUNK, D_B), 1) // HEAD_DIM
        s_chunks = []
        for c in range(tm // CHUNK):
            vc = v[c * CHUNK:(c + 1) * CHUNK, :]
            vstack = jnp.concatenate(
                [jnp.where(lane_head == hd, vc, 0.0) for hd in range(N_HEADS_B)], axis=0)
            s_chunks.append(_dot(w_tril, vstack.astype(BF16)) + bsfull_ref[...])
        y_b = zb[:, :D_B] * jnp.concatenate(s_chunks, axis=0)

        gext_ref[C_HALO:C_HALO + tm, :] = (z[:, O4:O4 + D_C]
                                           * jax.nn.sigmoid(z[:, O4 + D_C:]))
        n_shift = gshift_ref.shape[1]
        for s in range(1, SUBLANES):
            gshift_ref[s - 1] = gext_ref[s:s + n_shift, :]
        base = C_HALO - (CONV_C - 1)
        for r0 in range(0, tm, CONV_ROWS):
            acc = jnp.zeros((CONV_ROWS, D_C), F32)
            for k in range(CONV_C):
                lo = (base + k) // SUBLANES * SUBLANES + r0
                s = (base + k) % SUBLANES
                window = (gext_ref[lo:lo + CONV_ROWS, :] if s == 0
                          else gshift_ref[s - 1, lo:lo + CONV_ROWS, :])
                acc = acc + wcc_ref[k:k + 1, :] * window
            conv_ref[r0:r0 + CONV_ROWS, :] = acc
        y_c = jax.nn.silu(_ln(conv_ref[...] + bcc_ref[...], glnc_ref[...], blnc_ref[...]))
        last_g = gext_ref[C_HALO + tm - (CONV_C - 1):C_HALO + tm, :]
        newc_ref[...] = last_g
        gext_ref[C_HALO - (CONV_C - 1):C_HALO, :] = last_g

        hout_ref[...] = h + _dot(_normed_groups(y_a, y_b, y_c, gout_ref), wo_ref[...])

    @pl.when(g == n_seq_tiles)
    def _append_sample_rows():
        hout_ref[0:hs_ref.shape[0], :] = hs_ref[...]


def _ride_specs(ride, n_steps):
    rows = ride.shape[0] // n_steps
    assert rows * n_steps == ride.shape[0]
    slab = lambda g: (jnp.minimum(g, n_steps - 1), 0)
    return (pl.BlockSpec((rows, ride.shape[1]), slab), pl.BlockSpec((rows, ride.shape[1]), slab),
            jax.ShapeDtypeStruct(ride.shape, BF16))


def _mixer_seq(hp, hs, lw, n_seq, seq, ride):
    tm = TOK_ROWS
    tiles_per_seq = seq // tm
    n_seq_tiles = n_seq * tiles_per_seq
    join = hs is not None
    if not join:
        hs = jnp.zeros((SUBLANES, D_MODEL), F32)
    n_out = n_seq * seq + (hs.shape[0] if join else 0)
    weights = [lw[k] for k in ('g_mix', 'w_in', 'w_conv_a', 'w_s_cat', 'b_s_full', 'g_ln_b',
                               'b_ln_b', 'w_conv_c', 'b_conv_c', 'g_ln_c', 'b_ln_c', 'g_out',
                               'w_o')]
    seq_of = lambda g: jnp.minimum(g // tiles_per_seq, n_seq - 1)
    n_shift = tm + C_HALO - SUBLANES
    ride_in_spec, ride_out_spec, ride_out_shape = _ride_specs(ride, n_seq_tiles)
    return pl.pallas_call(
        functools.partial(_mixer_seq_kernel, tiles_per_seq=tiles_per_seq,
                          n_seq_tiles=n_seq_tiles),
        grid=(n_seq_tiles + int(join),),
        in_specs=[pl.BlockSpec((tm, D_MODEL), lambda g: (jnp.minimum(g, n_seq_tiles - 1), 0)),
                  _const_spec(hs.shape)]
        + [_layer_spec(*w) for w in weights] + [ride_in_spec],
        out_specs=[pl.BlockSpec((tm, D_MODEL), lambda g: (g, 0)),
                   pl.BlockSpec((None, CONV_A - 1, D_A), lambda g: (seq_of(g), 0, 0)),
                   pl.BlockSpec((None, CONV_C - 1, D_C), lambda g: (seq_of(g), 0, 0)),
                   ride_out_spec],
        out_shape=[jax.ShapeDtypeStruct((n_out, D_MODEL), F32),
                   jax.ShapeDtypeStruct((n_seq, CONV_A - 1, D_A), F32),
                   jax.ShapeDtypeStruct((n_seq, CONV_C - 1, D_C), F32),
                   ride_out_shape],
        scratch_shapes=[pltpu.VMEM((A_HALO + tm, D_A), F32),
                        pltpu.VMEM((C_HALO + tm, D_C), F32),
                        pltpu.VMEM((SUBLANES - 1, n_shift, D_C), F32),
                        pltpu.VMEM((tm, D_C), F32)],
        compiler_params=_tc_params(),
        name="mixer_seq",
    )(hp, hs, *[w for w, _ in weights], ride)


def _mixer_row_kernel(h_ref, sa_ref, sc_ref, gmix_ref, win_ref, wca_ref, wscat_ref,
                      bsfull_ref, glnb_ref, blnb_ref, wcc_ref, bcc_ref, glnc_ref, blnc_ref,
                      gout_ref, wo_ref, wpg_ref, wpp_ref, hout_ref, newa_ref, newc_ref, v_ref,
                      win_bf_ref, wo_bf_ref, wpg_bf_ref, wpp_bf_ref):
    for src, dst in ((win_ref, win_bf_ref), (wo_ref, wo_bf_ref), (wpg_ref, wpg_bf_ref),
                     (wpp_ref, wpp_bf_ref)):
        dst[...] = src[...].astype(BF16)

    h = h_ref[...]
    z = _dot_full(_rms(h, gmix_ref[...]), win_ref[...])

    q = z[:, O1:O2] * z[:, O2:O3]
    conv_a = wca_ref[CONV_A - 1:CONV_A, :] * q
    for k in range(CONV_A - 1):
        conv_a = conv_a + wca_ref[k:k + 1, :] * sa_ref[k]
    y_a = z[:, :O1] * conv_a
    for k in range(CONV_A - 2):
        newa_ref[k] = sa_ref[k + 1]
    newa_ref[CONV_A - 2] = q

    zb = jax.nn.gelu(z[:, O3:O4])
    v = _ln(zb[:, D_B:], glnb_ref[...], blnb_ref[...])
    v_ref[...] = v
    w_diag0 = jnp.concatenate(
        [jnp.broadcast_to(wscat_ref[0:1, hd * CHUNK:hd * CHUNK + 1], (1, HEAD_DIM))
         for hd in range(N_HEADS_B)], axis=-1)
    y_b = zb[:, :D_B] * (w_diag0 * v + bsfull_ref[0:1, :])

    glu = z[:, O4:O4 + D_C] * jax.nn.sigmoid(z[:, O4 + D_C:])
    conv_c = wcc_ref[CONV_C - 1:CONV_C, :] * glu
    for k in range(CONV_C - 1):
        conv_c = conv_c + wcc_ref[k:k + 1, :] * sc_ref[k]
    y_c = jax.nn.silu(_ln(conv_c + bcc_ref[...], glnc_ref[...], blnc_ref[...]))
    for k in range(CONV_C - 2):
        newc_ref[k] = sc_ref[k + 1]
    newc_ref[CONV_C - 2] = glu

    hout_ref[...] = h + _dot_full(_normed_groups(y_a, y_b, y_c, gout_ref), wo_ref[...])


CONVERTED_BY_MIXER_ROW = ('w_in', 'w_o', 'w_ple_gate', 'w_ple_proj')


def _mixer_row(h, h_block, lw):
    n_dec = lw['state_a'][0].shape[-2]
    weights = [lw[k] for k in ('state_a', 'state_c', 'g_mix', 'w_in_f32', 'w_conv_a', 'w_s_cat',
                               'b_s_full', 'g_ln_b', 'b_ln_b', 'w_conv_c', 'b_conv_c', 'g_ln_c',
                               'b_ln_c', 'g_out', 'w_o_f32', 'w_ple_gate_f32', 'w_ple_proj_f32')]
    bf16_shapes = [lw[k + '_f32'][0].shape[1:] for k in CONVERTED_BY_MIXER_ROW]
    return pl.pallas_call(
        _mixer_row_kernel,
        grid=(1,),
        in_specs=[pl.BlockSpec((n_dec, D_MODEL), lambda g: (h_block, 0))]
        + [_layer_spec(*w) for w in weights],
        out_specs=[_whole_out_spec((n_dec, D_MODEL)), _whole_out_spec((CONV_A - 1, n_dec, D_A)),
                   _whole_out_spec((CONV_C - 1, n_dec, D_C)), _whole_out_spec((n_dec, D_B))]
        + [_whole_out_spec(s) for s in bf16_shapes],
        out_shape=[jax.ShapeDtypeStruct((n_dec, D_MODEL), F32),
                   jax.ShapeDtypeStruct((CONV_A - 1, n_dec, D_A), F32),
                   jax.ShapeDtypeStruct((CONV_C - 1, n_dec, D_C), F32),
                   jax.ShapeDtypeStruct((n_dec, D_B), F32)]
        + [jax.ShapeDtypeStruct(s, BF16) for s in bf16_shapes],
        compiler_params=_tc_params(),
        name="mixer_row",
    )(h, *[w for w, _ in weights])


def _ple(h, p, gple_ref, wpg_ref, wpp_ref, dot):
    gate = jax.nn.sigmoid(dot(_rms(h, gple_ref[...]), wpg_ref[...]))
    return h + gate * dot(p, wpp_ref[...])


def _ffn_dense_kernel(h_ref, p_ref, gffn_ref, wg_ref, wu_ref, wd_ref, gple_ref, wpg_ref,
                      wpp_ref, ride_ref, out_ref, ride_out_ref):
    ride_out_ref[...] = ride_ref[...].astype(BF16)
    h = h_ref[...]
    xn = _rms(h, gffn_ref[...]).astype(BF16)
    a = jax.nn.silu(_dot(xn, wg_ref[...])) * _dot(xn, wu_ref[...])
    h = h + _dot(a, wd_ref[...])
    out_ref[...] = _ple(h, p_ref[...], gple_ref, wpg_ref, wpp_ref, _dot)


def _ffn_dense(h, lw, ride):
    n_tok = h.shape[0]
    tm = TOK_ROWS
    p, layer = lw['p_seq']
    weights = [lw[k] for k in ('g_ffn', 'w_ff_gate', 'w_ff_up', 'w_ff_down', 'g_ple',
                               'w_ple_gate', 'w_ple_proj')]
    ride_in_spec, ride_out_spec, ride_out_shape = _ride_specs(ride, n_tok // tm)
    return pl.pallas_call(
        _ffn_dense_kernel,
        grid=(n_tok // tm,),
        in_specs=[pl.BlockSpec((tm, D_MODEL), lambda g: (g, 0)),
                  pl.BlockSpec((None, tm, D_PLE), lambda g: (layer, g, 0))]
        + [_layer_spec(*w) for w in weights] + [ride_in_spec],
        out_specs=[pl.BlockSpec((tm, D_MODEL), lambda g: (g, 0)), ride_out_spec],
        out_shape=[jax.ShapeDtypeStruct(h.shape, F32), ride_out_shape],
        compiler_params=_tc_params(),
        name="ffn_dense",
    )(h, p, *[w for w, _ in weights], ride)


def _ffn_row_kernel(h_ref, p_ref, gffn_ref, wg_ref, wu_ref, wd_ref, gple_ref, wpg_ref,
                    wpp_ref, out_ref, wg_bf_ref, wu_bf_ref, wd_bf_ref, xn_ref, acc_ref):
    j = pl.program_id(0)

    @pl.when(j == 0)
    def _():
        h = h_ref[...]
        xn_ref[...] = _rms(h, gffn_ref[...])
        acc_ref[...] = h

    for src, dst in ((wg_ref, wg_bf_ref), (wu_ref, wu_bf_ref), (wd_ref, wd_bf_ref)):
        dst[...] = src[...].astype(BF16)

    xn = xn_ref[...]
    a = jax.nn.silu(_dot_full(xn, wg_ref[...])) * _dot_full(xn, wu_ref[...])
    acc_ref[...] += _dot_full(a, wd_ref[...])

    @pl.when(j == pl.num_programs(0) - 1)
    def _():
        out_ref[...] = _ple(acc_ref[...], p_ref[...], gple_ref, wpg_ref, wpp_ref, _dot_full)


def _ffn_row(h, lw):
    n_dec = h.shape[0]
    d_ff = lw['w_ff_gate_f32'][0].shape[-1]
    ff = lw['w_ff_gate_f32'][1]
    return pl.pallas_call(
        _ffn_row_kernel,
        grid=(d_ff // FF_COLS,),
        in_specs=[_const_spec(h.shape), _layer_spec(*lw['p_row']), _layer_spec(*lw['g_ffn']),
                  pl.BlockSpec((None, D_MODEL, FF_COLS), lambda j: (ff, 0, j)),
                  pl.BlockSpec((None, D_MODEL, FF_COLS), lambda j: (ff, 0, j)),
                  pl.BlockSpec((None, FF_COLS, D_MODEL), lambda j: (ff, j, 0)),
                  _layer_spec(*lw['g_ple']), _layer_spec(*lw['w_ple_gate_f32']),
                  _layer_spec(*lw['w_ple_proj_f32'])],
        out_specs=[_whole_out_spec(h.shape),
                   pl.BlockSpec((D_MODEL, FF_COLS), lambda j: (0, j)),
                   pl.BlockSpec((D_MODEL, FF_COLS), lambda j: (0, j)),
                   pl.BlockSpec((FF_COLS, D_MODEL), lambda j: (j, 0))],
        out_shape=[jax.ShapeDtypeStruct(h.shape, F32),
                   jax.ShapeDtypeStruct((D_MODEL, d_ff), BF16),
                   jax.ShapeDtypeStruct((D_MODEL, d_ff), BF16),
                   jax.ShapeDtypeStruct((d_ff, D_MODEL), BF16)],
        scratch_shapes=[pltpu.VMEM((n_dec, D_MODEL), F32), pltpu.VMEM((n_dec, D_MODEL), F32)],
        compiler_params=_tc_params(),
        name="ffn_row",
    )(h, *[lw[k][0] for k in ('p_row', 'g_ffn', 'w_ff_gate_f32', 'w_ff_up_f32',
                              'w_ff_down_f32', 'g_ple', 'w_ple_gate_f32', 'w_ple_proj_f32')])


def _router_kernel(h_ref, gffn_ref, wr_ref, mi_ref, mf_ref, cnt_ref, carry_ref, *, n_tok):
    g = pl.program_id(0)
    tm = h_ref.shape[0]

    @pl.when(g == 0)
    def _():
        carry_ref[...] = jnp.zeros(carry_ref.shape, F32)

    valid = (g * tm + lax.broadcasted_iota(I32, (tm, 1), 0)) < n_tok
    xn = _rms(jnp.where(valid, h_ref[...], 0.0), gffn_ref[...])
    logits = _dot_full(xn, wr_ref[...])
    lane = lax.broadcasted_iota(I32, logits.shape, 1)
    m1 = jnp.max(logits, axis=-1, keepdims=True)
    i1 = jnp.min(jnp.where(logits == m1, lane, N_EXPERTS), axis=-1, keepdims=True)
    rest = jnp.where(lane == i1, -jnp.inf, logits)
    m2 = jnp.max(rest, axis=-1, keepdims=True)
    i2 = jnp.min(jnp.where(rest == m2, lane, N_EXPERTS), axis=-1, keepdims=True)
    e2 = jnp.exp(m2 - m1)
    denom = 1.0 + e2
    w1 = 1.0 / denom
    w2 = e2 / denom

    oh1 = jnp.where((lane == i1) & valid, 1.0, 0.0)
    oh2 = jnp.where((lane == i2) & valid, 1.0, 0.0)
    member = oh1 + oh2
    r = lax.broadcasted_iota(I32, (tm, tm), 0)
    c = lax.broadcasted_iota(I32, (tm, tm), 1)
    before = jnp.where(c < r, 1.0, 0.0).astype(BF16)
    pos = _dot(before, member.astype(BF16)) + carry_ref[...]
    pos1 = jnp.sum(oh1 * pos, axis=-1, keepdims=True).astype(I32)
    pos2 = jnp.sum(oh2 * pos, axis=-1, keepdims=True).astype(I32)
    carry_ref[...] = carry_ref[...] + jnp.sum(member, axis=0, keepdims=True)
    cnt_ref[...] = carry_ref[...]

    mi_ref[...] = jnp.where(lane == 0, i1, jnp.where(lane == 1, i2,
                            jnp.where(lane == 2, pos1, jnp.where(lane == 3, pos2, 0))))
    mf_ref[...] = jnp.where(lane == 0, w1, jnp.where(lane == 1, w2, 0.0))


def _router(h, tile0, n_tok, lw):
    tm = TOK_ROWS
    return pl.pallas_call(
        functools.partial(_router_kernel, n_tok=n_tok),
        grid=(pl.cdiv(n_tok, tm),),
        in_specs=[pl.BlockSpec((tm, D_MODEL), lambda g: (tile0 + g, 0)),
                  _layer_spec(*lw['g_ffn']), _layer_spec(*lw['w_router'])],
        out_specs=[pl.BlockSpec((tm, N_EXPERTS), lambda g: (g, 0)),
                   pl.BlockSpec((tm, N_EXPERTS), lambda g: (g, 0)),
                   _whole_out_spec((1, N_EXPERTS))],
        out_shape=[jax.ShapeDtypeStruct((n_tok, N_EXPERTS), I32),
                   jax.ShapeDtypeStruct((n_tok, N_EXPERTS), F32),
                   jax.ShapeDtypeStruct((1, N_EXPERTS), F32)],
        scratch_shapes=[pltpu.VMEM((1, N_EXPERTS), F32)],
        compiler_params=_tc_params(),
        name="router",
    )(h, lw['g_ffn'][0], lw['w_router'][0])


def _sc_chunk(n_rows):
    per_worker = pl.cdiv(n_rows, SC_WORKERS)
    return pl.cdiv(per_worker, SC_WINDOW) * SC_WINDOW


def _sc_worker_base(n_rows, chunk):
    wid = lax.axis_index("s") * SC_CORES + lax.axis_index("c")
    return jnp.minimum(wid * chunk, n_rows - chunk)


def _sc_scatter_rows(x, row0, dest, n_out):
    n = dest.shape[0] // TOP_K
    chunk = _sc_chunk(n)
    mesh = plsc.VectorSubcoreMesh(core_axis_name="c", subcore_axis_name="s")

    @functools.partial(
        pl.kernel, mesh=mesh,
        out_type=jax.ShapeDtypeStruct((n_out, D_MODEL), x.dtype),
        scratch_types=[pltpu.VMEM((SC_WINDOW,), I32) for _ in range(TOP_K)]
        + [pltpu.VMEM((SC_WINDOW, D_MODEL), x.dtype), pltpu.SemaphoreType.DMA],
        name="sc_scatter_rows",
    )
    def scatter(x_hbm, dest_hbm, out_hbm, idx0_v, idx1_v, rows_v, sem):
        base = _sc_worker_base(n, chunk)

        @pl.loop(0, chunk // SC_WINDOW)
        def _(j):
            off = pl.multiple_of(base + j * SC_WINDOW, 8)
            pltpu.sync_copy(dest_hbm.at[pl.ds(off, SC_WINDOW)], idx0_v)
            pltpu.sync_copy(dest_hbm.at[pl.ds(n + off, SC_WINDOW)], idx1_v)
            pltpu.sync_copy(x_hbm.at[pl.ds(row0 + off, SC_WINDOW)], rows_v)
            first = pltpu.async_copy(rows_v, out_hbm.at[idx0_v], sem)
            second = pltpu.async_copy(rows_v, out_hbm.at[idx1_v], sem)
            first.wait()
            second.wait()

    return scatter(x, dest)


def _sc_gather_rows(y, idx):
    n = idx.shape[0]
    chunk = _sc_chunk(n)
    mesh = plsc.VectorSubcoreMesh(core_axis_name="c", subcore_axis_name="s")

    @functools.partial(
        pl.kernel, mesh=mesh,
        out_type=jax.ShapeDtypeStruct((n, D_MODEL), y.dtype),
        scratch_types=[pltpu.VMEM((SC_WINDOW,), I32),
                       pltpu.VMEM((SC_WINDOW, D_MODEL), y.dtype), pltpu.SemaphoreType.DMA],
        name="sc_gather_rows",
    )
    def gather(y_hbm, idx_hbm, out_hbm, idx_v, rows_v, sem):
        base = _sc_worker_base(n, chunk)

        @pl.loop(0, chunk // SC_WINDOW)
        def _(j):
            off = pl.multiple_of(base + j * SC_WINDOW, 8)
            pltpu.sync_copy(idx_hbm.at[pl.ds(off, SC_WINDOW)], idx_v)
            pltpu.async_copy(y_hbm.at[idx_v], rows_v, sem).wait()
            pltpu.sync_copy(rows_v, out_hbm.at[pl.ds(off, SC_WINDOW)])

    return gather(y, idx)


def _expert_kernel(tile_expert_ref, n_valid_ref, xs_ref, gffn_ref, wg_ref, wu_ref, wd_ref,
                   y_ref):
    @pl.when(pl.program_id(0) < n_valid_ref[0])
    def _():
        xn = _rms(xs_ref[...], gffn_ref[...]).astype(BF16)
        a = jax.nn.silu(_dot(xn, wg_ref[...])) * _dot(xn, wu_ref[...])
        y_ref[...] = _dot(a, wd_ref[...])


def _experts(xs, tile_expert, n_valid, lw):
    n_slots = xs.shape[0]
    te = EXPERT_ROWS
    d_exp = lw['w_ex_gate'][0].shape[-1]
    moe = lw['w_ex_gate'][1]
    row_block = lambda g, tex, nv: (jnp.minimum(g, nv[0] - 1), 0)
    w_block = lambda g, tex, nv: (moe, tex[g], 0, 0)
    return pl.pallas_call(
        _expert_kernel,
        grid_spec=pltpu.PrefetchScalarGridSpec(
            num_scalar_prefetch=2,
            grid=(n_slots // te,),
            in_specs=[pl.BlockSpec((te, D_MODEL), row_block),
                      _layer_spec(*lw['g_ffn']),
                      pl.BlockSpec((None, None, D_MODEL, d_exp), w_block),
                      pl.BlockSpec((None, None, D_MODEL, d_exp), w_block),
                      pl.BlockSpec((None, None, d_exp, D_MODEL), w_block)],
            out_specs=pl.BlockSpec((te, D_MODEL), row_block)),
        out_shape=jax.ShapeDtypeStruct((n_slots, D_MODEL), F32),
        compiler_params=_tc_params(),
        name="experts",
    )(tile_expert, n_valid, xs, *[lw[k][0] for k in ('g_ffn', 'w_ex_gate', 'w_ex_up',
                                                        'w_ex_down')])


def _combine_math(h_ref, yg_ref, mf_ref, p, gple_ref, wpg_ref, wpp_ref, gfin_ref):
    gates = mf_ref[...]
    h = h_ref[...] + (gates[:, 0:1] * yg_ref[0] + gates[:, 1:2] * yg_ref[1])
    return _rms(_ple(h, p, gple_ref, wpg_ref, wpp_ref, _dot), gfin_ref[...])


def _combine_seq_kernel(h_ref, yg_ref, mf_ref, pp_ref, gple_ref, wpg_ref, wpp_ref, gfin_ref,
                        ybuf_ref, yp_ref):
    del ybuf_ref
    yp_ref[...] = _combine_math(h_ref, yg_ref, mf_ref, pp_ref[...], gple_ref, wpg_ref,
                                wpp_ref, gfin_ref)


def _combine_tail_kernel(h_ref, yg_ref, mf_ref, pp_ref, ps_ref, gple_ref, wpg_ref, wpp_ref,
                         gfin_ref, ybuf_ref, yp_ref, ys_ref, pbuf_ref):
    del ybuf_ref
    g = pl.program_id(0)
    last = pl.num_programs(0) - 1
    n_dec = ps_ref.shape[0]
    pbuf_ref[...] = pp_ref[...]

    @pl.when(g == last)
    def _():
        pbuf_ref[0:n_dec, :] = ps_ref[...]

    out = _combine_math(h_ref, yg_ref, mf_ref, pbuf_ref[...], gple_ref, wpg_ref, wpp_ref,
                        gfin_ref)

    @pl.when(g < last)
    def _():
        yp_ref[...] = out

    @pl.when(g == last)
    def _():
        ys_ref[...] = out[0:n_dec, :]


def _combine(h, tile0, yg, mf, y_seq, lw, with_rows):
    tm = TOK_ROWS
    n_tiles = pl.cdiv(yg.shape[1], tm)
    last_seq_tile = y_seq.shape[0] // tm - 1
    pp, layer = lw['p_seq']
    n_dec = lw['p_row'][0].shape[-2]
    weights = [lw[k] for k in (('p_row',) if with_rows else ())
               + ('g_ple', 'w_ple_gate', 'w_ple_proj', 'g_final')]
    seq_tile = lambda g: (jnp.minimum(tile0 + g, last_seq_tile), 0)
    n_in = 4 + len(weights)
    out = pl.pallas_call(
        _combine_tail_kernel if with_rows else _combine_seq_kernel,
        grid=(n_tiles,),
        in_specs=[pl.BlockSpec((tm, D_MODEL), lambda g: (tile0 + g, 0)),
                  pl.BlockSpec((TOP_K, tm, D_MODEL), lambda g: (0, g, 0)),
                  pl.BlockSpec((tm, N_EXPERTS), lambda g: (g, 0)),
                  pl.BlockSpec((None, tm, D_PLE), lambda g: (layer,) + seq_tile(g))]
        + [_layer_spec(*w) for w in weights]
        + [pl.BlockSpec(memory_space=pl.ANY)],
        out_specs=[pl.BlockSpec((tm, D_MODEL), seq_tile)]
        + ([_whole_out_spec((n_dec, D_MODEL))] if with_rows else []),
        out_shape=[jax.ShapeDtypeStruct(y_seq.shape, F32)]
        + ([jax.ShapeDtypeStruct((n_dec, D_MODEL), F32)] if with_rows else []),
        scratch_shapes=[pltpu.VMEM((tm, D_PLE), F32)] if with_rows else [],
        input_output_aliases={n_in: 0},
        compiler_params=_tc_params(),
        name="combine",
    )(h, yg, mf, pp, *[w for w, _ in weights], y_seq)
    return out if with_rows else (out[0], None)


def _moe_layer(h, y_seq, lw):
    n_tok = h.shape[0]
    tm = TOK_ROWS
    te = EXPERT_ROWS
    tiles = pl.cdiv(n_tok, tm)
    bounds = [tiles * c // MOE_CHUNKS for c in range(MOE_CHUNKS + 1)]
    y_rows = None
    for c in range(MOE_CHUNKS):
        tile0 = bounds[c]
        n = min(bounds[c + 1] * tm, n_tok) - tile0 * tm
        mi, mf, counts = _router(h, tile0, n, lw)

        cnt = counts[0].astype(I32)
        padded = (cnt + te - 1) // te * te
        ends = jnp.cumsum(padded)
        starts = ends - padded
        experts = jnp.arange(N_EXPERTS, dtype=I32)
        start_of = lambda e: jnp.sum(jnp.where(e[:, None] == experts, starts, 0), axis=-1)
        dest = jnp.concatenate([start_of(mi[:, 0]) + mi[:, 2], start_of(mi[:, 1]) + mi[:, 3]])
        n_tiles = pl.cdiv(TOP_K * n + N_EXPERTS * (te - 1), te)
        tile_start = jnp.arange(n_tiles, dtype=I32) * te
        last_used = jnp.max(jnp.where(padded > 0, experts, 0))
        tile_expert = jnp.minimum(
            jnp.sum(tile_start[:, None] >= ends[None, :], axis=-1).astype(I32), last_used)
        n_valid = (ends[-1:] // te).astype(I32)

        xs = _sc_scatter_rows(h, tile0 * tm, dest, n_tiles * te)
        y = _experts(xs, tile_expert, n_valid, lw)
        yg = _sc_gather_rows(y, dest).reshape(TOP_K, n, D_MODEL)
        y_seq, rows = _combine(h, tile0, yg, mf, y_seq, lw, with_rows=c == MOE_CHUNKS - 1)
        y_rows = rows if rows is not None else y_rows
    return y_seq, y_rows


def kernel(x_prompt, x_sample, state_conv_a, state_conv_c, p_prompt, p_sample, g_mix, w_in, w_conv_a, w_s, b_s, g_ln_b, b_ln_b, w_conv_c, b_conv_c, g_ln_c, b_ln_c, g_out, w_o, g_ffn, w_ff_gate, w_ff_up, w_ff_down, w_router, w_ex_gate, w_ex_up, w_ex_down, g_ple, w_ple_gate, w_ple_proj, g_final):
    depth = g_mix.shape[0]
    n_seq, seq, _ = x_prompt.shape
    n_dec = x_sample.shape[0]
    n_prompt = n_seq * seq
    assert depth == 2 and x_sample.shape[1] == 1
    assert seq % TOK_ROWS == 0 and TOK_ROWS % n_dec == 0 and w_ff_gate.shape[-1] % FF_COLS == 0

    vec = lambda x: x.reshape(x.shape[0], 1, -1)
    per_layer = {
        'g_mix': vec(g_mix), 'w_in_f32': w_in, 'w_conv_a': w_conv_a,
        'w_s_cat': jnp.transpose(w_s, (0, 2, 1, 3)).reshape(depth, CHUNK, N_HEADS_B * CHUNK),
        'b_s_full': jnp.repeat(jnp.swapaxes(b_s, 1, 2), HEAD_DIM, axis=2),
        'g_ln_b': vec(g_ln_b), 'b_ln_b': vec(b_ln_b), 'w_conv_c': w_conv_c,
        'b_conv_c': vec(b_conv_c), 'g_ln_c': vec(g_ln_c), 'b_ln_c': vec(b_ln_c),
        'g_out': vec(g_out), 'w_o_f32': w_o, 'g_ffn': vec(g_ffn), 'g_ple': vec(g_ple),
        'w_ple_gate_f32': w_ple_gate, 'w_ple_proj_f32': w_ple_proj,
        'state_a': jnp.swapaxes(state_conv_a, 1, 2), 'state_c': jnp.swapaxes(state_conv_c, 1, 2),
        'p_seq': p_prompt.reshape(depth, n_prompt, D_PLE),
        'p_row': p_sample.reshape(depth, n_dec, D_PLE),
    }
    per_dense = {'w_ff_gate_f32': w_ff_gate, 'w_ff_up_f32': w_ff_up, 'w_ff_down_f32': w_ff_down}
    one_layer = lambda w: (w.reshape((1,) + w.shape), 0)
    per_moe = {'w_router': w_router}
    to_convert = [('w_ex_gate', w_ex_gate), ('w_ex_up', w_ex_up), ('w_ex_down', w_ex_down)]

    def with_ride(call):
        name, w = to_convert.pop(0)
        *outputs, w_bf16 = call(w.reshape(-1, w.shape[-1]))
        per_moe[name] = w_bf16.reshape(w.shape)
        return outputs

    hp = x_prompt.reshape(n_prompt, D_MODEL)
    hs = x_sample.reshape(n_dec, D_MODEL)
    outs = {k: [] for k in ('a_p', 'a_s', 'c_p', 'c_s', 'v_s')}
    for i in range(depth):
        is_expert_layer = i % 2 == 1
        lw = {k: (v, i) for k, v in per_layer.items()}
        lw.update({k: (v, i // 2) for k, v in (per_dense if not is_expert_layer else {}).items()})
        lw['g_final'] = (g_final.reshape(1, 1, -1), 0)

        hs, a_s, c_s, v_s, *converted = _mixer_row(hs, 0, lw)
        lw.update({k: one_layer(w) for k, w in zip(CONVERTED_BY_MIXER_ROW, converted)})
        mixer_in = hp
        hp, a_p, c_p = with_ride(functools.partial(
            _mixer_seq, hp, hs if is_expert_layer else None, lw, n_seq, seq))
        for k, v in zip(('a_p', 'c_p', 'a_s', 'c_s', 'v_s'), (a_p, c_p, a_s, c_s, v_s)):
            outs[k].append(v)
        if not is_expert_layer:
            hs, *converted = _ffn_row(hs, lw)
            lw.update({k: one_layer(w) for k, w in
                       zip(('w_ff_gate', 'w_ff_up', 'w_ff_down'), converted)})
            hp, = with_ride(functools.partial(_ffn_dense, hp, lw))
        else:
            lw.update({k: (v, i // 2) for k, v in per_moe.items()})
            y_prompt, y_sample = _moe_layer(hp, mixer_in, lw)

    return (y_prompt.reshape(x_prompt.shape), y_sample.reshape(x_sample.shape),
            jnp.stack(outs['a_p']), jnp.swapaxes(jnp.stack(outs['a_s']), 1, 2),
            jnp.stack(outs['c_p']), jnp.swapaxes(jnp.stack(outs['c_s']), 1, 2),
            jnp.stack(outs['v_s']).reshape(depth, n_dec, 1, D_B))
```

```python
import functools

import jax
import jax.numpy as jnp
from jax import lax
from jax.experimental import pallas as pl
from jax.experimental.pallas import tpu as pltpu
from jax.experimental.pallas import tpu_sc as plsc

D_MODEL = 1024
HEAD_DIM = 64
D_A = 384
D_B = 256
D_C = 384
N_HEADS_B = D_B // HEAD_DIM
CONV_A = 3
CONV_C = 31
CHUNK = 128
D_IN = 3 * D_A + 2 * D_B + 2 * D_C
D_PLE = 256
N_EXPERTS = 8
TOP_K = 2
EPS = 1e-6

O1, O2, O3 = D_A, 2 * D_A, 3 * D_A
O4 = O3 + 2 * D_B

F32 = jnp.float32
BF16 = jnp.bfloat16
I32 = jnp.int32

VMEM_LIMIT_BYTES = 56 * 1024 * 1024
SUBLANES = 8
LANES = 128

TOK_ROWS = 512
CONV_ROWS = 64
A_HALO = 8
C_HALO = 32
EXPERT_ROWS = 512
FF_COLS = 256
MOE_CHUNKS = 2

SC_CORES = 2
SC_WORKERS = 32
SC_WINDOW = 48


def _rms(x, g):
    return x * lax.rsqrt(jnp.mean(x * x, axis=-1, keepdims=True) + EPS) * g


def _ln(x, g, b):
    mu = jnp.mean(x, axis=-1, keepdims=True)
    xc = x - mu
    var = jnp.mean(xc * xc, axis=-1, keepdims=True)
    return xc * lax.rsqrt(var + EPS) * g + b


def _dot(a, b):
    return jnp.dot(a.astype(BF16), b, preferred_element_type=F32)


def _split(x):
    hi = x.astype(BF16)
    return hi, (x - hi.astype(F32)).astype(BF16)


def _dot_full(a, b):
    a_hi, a_lo = _split(a)
    b_hi, b_lo = _split(b)
    m = a.shape[0]
    by_hi = jnp.dot(jnp.concatenate([a_hi, a_lo], axis=0), b_hi, preferred_element_type=F32)
    return by_hi[:m] + by_hi[m:] + jnp.dot(a_hi, b_lo, preferred_element_type=F32)


def _const_spec(shape):
    return pl.BlockSpec(shape, lambda *_: (0,) * len(shape), pipeline_mode=pl.Buffered(1))


def _layer_spec(stacked, layer):
    idx = (layer,) if isinstance(layer, int) else tuple(layer)
    rest = stacked.shape[len(idx):]
    return pl.BlockSpec((None,) * len(idx) + rest, lambda *_: idx + (0,) * len(rest),
                        pipeline_mode=pl.Buffered(1))


def _whole_out_spec(shape):
    return pl.BlockSpec(shape, lambda *_: (0,) * len(shape))


def _tc_params():
    return pltpu.CompilerParams(dimension_semantics=("arbitrary",),
                                vmem_limit_bytes=VMEM_LIMIT_BYTES)


def _normed_groups(y_a, y_b, y_c, gout_ref):
    return jnp.concatenate([_rms(y_a, gout_ref[:, :D_A]),
                            _rms(y_b, gout_ref[:, D_A:D_A + D_B]),
                            _rms(y_c, gout_ref[:, D_A + D_B:])], axis=-1)


def _mixer_seq_kernel(hp_ref, hs_ref, gmix_ref, win_ref, wca_ref, wscat_ref, bsfull_ref,
                      glnb_ref, blnb_ref, wcc_ref, bcc_ref, glnc_ref, blnc_ref, gout_ref,
                      wo_ref, ride_ref, hout_ref, newa_ref, newc_ref, ride_out_ref,
                      qext_ref, gext_ref, gshift_ref, conv_ref, *, tiles_per_seq, n_seq_tiles):
    g = pl.program_id(0)
    tm = hp_ref.shape[0]

    @pl.when(g < n_seq_tiles)
    def _sequence_tile():
        ride_out_ref[...] = ride_ref[...].astype(BF16)

        @pl.when(g % tiles_per_seq == 0)
        def _():
            qext_ref[0:A_HALO, :] = jnp.zeros((A_HALO, D_A), F32)
            gext_ref[0:C_HALO, :] = jnp.zeros((C_HALO, D_C), F32)

        h = hp_ref[...]
        z = _dot(_rms(h, gmix_ref[...]), win_ref[...])

        qext_ref[A_HALO:A_HALO + tm, :] = z[:, O1:O2] * z[:, O2:O3]
        conv_a = jnp.zeros((tm, D_A), F32)
        for k in range(CONV_A):
            off = A_HALO - (CONV_A - 1) + k
            conv_a = conv_a + wca_ref[k:k + 1, :] * qext_ref[off:off + tm, :]
        y_a = z[:, :O1] * conv_a
        last_q = qext_ref[A_HALO + tm - (CONV_A - 1):A_HALO + tm, :]
        newa_ref[...] = last_q
        qext_ref[A_HALO - (CONV_A - 1):A_HALO, :] = last_q

        zb = jax.nn.gelu(z[:, O3:O4])
        v = _ln(zb[:, D_B:], glnb_ref[...], blnb_ref[...])
        row = lax.broadcasted_iota(I32, (CHUNK, N_HEADS_B * CHUNK), 0)
        col = lax.broadcasted_iota(I32, (CHUNK, N_HEADS_B * CHUNK), 1)
        w_tril = jnp.where((col % CHUNK) <= row, wscat_ref[...], 0.0).astype(BF16)
        lane_head = lax.broadcasted_iota(I32, (CHUNK, D_B), 1) // HEAD_DIM
        s_chunks = []
        for c in range(tm // CHUNK):
            vc = v[c * CHUNK:(c + 1) * CHUNK, :]
            vstack = jnp.concatenate(
                [jnp.where(lane_head == hd, vc, 0.0) for hd in range(N_HEADS_B)], axis=0)
            s_chunks.append(_dot(w_tril, vstack.astype(BF16)) + bsfull_ref[...])
        y_b = zb[:, :D_B] * jnp.concatenate(s_chunks, axis=0)

        gext_ref[C_HALO:C_HALO + tm, :] = (z[:, O4:O4 + D_C]
                                           * jax.nn.sigmoid(z[:, O4 + D_C:]))
        n_shift = gshift_ref.shape[1]
        for s in range(1, SUBLANES):
            gshift_ref[s - 1] = gext_ref[s:s + n_shift, :]
        base = C_HALO - (CONV_C - 1)
        for r0 in range(0, tm, CONV_ROWS):
            acc = jnp.zeros((CONV_ROWS, D_C), F32)
            for k in range(CONV_C):
                lo = (base + k) // SUBLANES * SUBLANES + r0
                s = (base + k) % SUBLANES
                window = (gext_ref[lo:lo + CONV_ROWS, :] if s == 0
                          else gshift_ref[s - 1, lo:lo + CONV_ROWS, :])
                acc = acc + wcc_ref[k:k + 1, :] * window
            conv_ref[r0:r0 + CONV_ROWS, :] = acc
        y_c = jax.nn.silu(_ln(conv_ref[...] + bcc_ref[...], glnc_ref[...], blnc_ref[...]))
        last_g = gext_ref[C_HALO + tm - (CONV_C - 1):C_HALO + tm, :]
        newc_ref[...] = last_g
        gext_ref[C_HALO - (CONV_C - 1):C_HALO, :] = last_g

        hout_ref[...] = h + _dot(_normed_groups(y_a, y_b, y_c, gout_ref), wo_ref[...])

    @pl.when(g == n_seq_tiles)
    def _append_sample_rows():
        hout_ref[0:hs_ref.shape[0], :] = hs_ref[...]


def _ride_specs(ride, n_steps):
    rows = ride.shape[0] // n_steps
    assert rows * n_steps == ride.shape[0]
    slab = lambda g: (jnp.minimum(g, n_steps - 1), 0)
    return (pl.BlockSpec((rows, ride.shape[1]), slab), pl.BlockSpec((rows, ride.shape[1]), slab),
            jax.ShapeDtypeStruct(ride.shape, BF16))


def _mixer_seq(hp, hs, lw, n_seq, seq, ride):
    tm = TOK_ROWS
    tiles_per_seq = seq // tm
    n_seq_tiles = n_seq * tiles_per_seq
    join = hs is not None
    if not join:
        hs = jnp.zeros((SUBLANES, D_MODEL), F32)
    n_out = n_seq * seq + (hs.shape[0] if join else 0)
    weights = [lw[k] for k in ('g_mix', 'w_in', 'w_conv_a', 'w_s_cat', 'b_s_full', 'g_ln_b',
                               'b_ln_b', 'w_conv_c', 'b_conv_c', 'g_ln_c', 'b_ln_c', 'g_out',
                               'w_o')]
    seq_of = lambda g: jnp.minimum(g // tiles_per_seq, n_seq - 1)
    n_shift = tm + C_HALO - SUBLANES
    ride_in_spec, ride_out_spec, ride_out_shape = _ride_specs(ride, n_seq_tiles)
    return pl.pallas_call(
        functools.partial(_mixer_seq_kernel, tiles_per_seq=tiles_per_seq,
                          n_seq_tiles=n_seq_tiles),
        grid=(n_seq_tiles + int(join),),
        in_specs=[pl.BlockSpec((tm, D_MODEL), lambda g: (jnp.minimum(g, n_seq_tiles - 1), 0)),
                  _const_spec(hs.shape)]
        + [_layer_spec(*w) for w in weights] + [ride_in_spec],
        out_specs=[pl.BlockSpec((tm, D_MODEL), lambda g: (g, 0)),
                   pl.BlockSpec((None, CONV_A - 1, D_A), lambda g: (seq_of(g), 0, 0)),
                   pl.BlockSpec((None, CONV_C - 1, D_C), lambda g: (seq_of(g), 0, 0)),
                   ride_out_spec],
        out_shape=[jax.ShapeDtypeStruct((n_out, D_MODEL), F32),
                   jax.ShapeDtypeStruct((n_seq, CONV_A - 1, D_A), F32),
                   jax.ShapeDtypeStruct((n_seq, CONV_C - 1, D_C), F32),
                   ride_out_shape],
        scratch_shapes=[pltpu.VMEM((A_HALO + tm, D_A), F32),
                        pltpu.VMEM((C_HALO + tm, D_C), F32),
                        pltpu.VMEM((SUBLANES - 1, n_shift, D_C), F32),
                        pltpu.VMEM((tm, D_C), F32)],
        compiler_params=_tc_params(),
        name="mixer_seq",
    )(hp, hs, *[w for w, _ in weights], ride)


def _mixer_row_kernel(h_ref, sa_ref, sc_ref, gmix_ref, win_ref, wca_ref, wscat_ref,
                      bsfull_ref, glnb_ref, blnb_ref, wcc_ref, bcc_ref, glnc_ref, blnc_ref,
                      gout_ref, wo_ref, wpg_ref, wpp_ref, hout_ref, newa_ref, newc_ref, v_ref,
                      win_bf_ref, wo_bf_ref, wpg_bf_ref, wpp_bf_ref):
    for src, dst in ((win_ref, win_bf_ref), (wo_ref, wo_bf_ref), (wpg_ref, wpg_bf_ref),
                     (wpp_ref, wpp_bf_ref)):
        dst[...] = src[...].astype(BF16)

    h = h_ref[...]
    z = _dot_full(_rms(h, gmix_ref[...]), win_ref[...])

    q = z[:, O1:O2] * z[:, O2:O3]
    conv_a = wca_ref[CONV_A - 1:CONV_A, :] * q
    for k in range(CONV_A - 1):
        conv_a = conv_a + wca_ref[k:k + 1, :] * sa_ref[k]
    y_a = z[:, :O1] * conv_a
    for k in range(CONV_A - 2):
        newa_ref[:, k, :] = sa_ref[k + 1]
    newa_ref[:, CONV_A - 2, :] = q

    zb = jax.nn.gelu(z[:, O3:O4])
    v = _ln(zb[:, D_B:], glnb_ref[...], blnb_ref[...])
    v_ref[...] = v
    w_diag0 = jnp.concatenate(
        [jnp.broadcast_to(wscat_ref[0:1, hd * CHUNK:hd * CHUNK + 1], (1, HEAD_DIM))
         for hd in range(N_HEADS_B)], axis=-1)
    y_b = zb[:, :D_B] * (w_diag0 * v + bsfull_ref[0:1, :])

    glu = z[:, O4:O4 + D_C] * jax.nn.sigmoid(z[:, O4 + D_C:])
    conv_c = wcc_ref[CONV_C - 1:CONV_C, :] * glu
    for k in range(CONV_C - 1):
        conv_c = conv_c + wcc_ref[k:k + 1, :] * sc_ref[k]
    y_c = jax.nn.silu(_ln(conv_c + bcc_ref[...], glnc_ref[...], blnc_ref[...]))
    for k in range(CONV_C - 2):
        newc_ref[:, k, :] = sc_ref[k + 1]
    newc_ref[:, CONV_C - 2, :] = glu

    hout_ref[...] = h + _dot_full(_normed_groups(y_a, y_b, y_c, gout_ref), wo_ref[...])


CONVERTED_BY_MIXER_ROW = ('w_in', 'w_o', 'w_ple_gate', 'w_ple_proj')


def _mixer_row(h, h_block, lw):
    n_dec = lw['state_a'][0].shape[-2]
    weights = [lw[k] for k in ('state_a', 'state_c', 'g_mix', 'w_in_f32', 'w_conv_a', 'w_s_cat',
                               'b_s_full', 'g_ln_b', 'b_ln_b', 'w_conv_c', 'b_conv_c', 'g_ln_c',
                               'b_ln_c', 'g_out', 'w_o_f32', 'w_ple_gate_f32', 'w_ple_proj_f32')]
    bf16_shapes = [lw[k + '_f32'][0].shape[1:] for k in CONVERTED_BY_MIXER_ROW]
    return pl.pallas_call(
        _mixer_row_kernel,
        grid=(1,),
        in_specs=[pl.BlockSpec((n_dec, D_MODEL), lambda g: (h_block, 0))]
        + [_layer_spec(*w) for w in weights],
        out_specs=[_whole_out_spec((n_dec, D_MODEL)), _whole_out_spec((n_dec, CONV_A - 1, D_A)),
                   _whole_out_spec((n_dec, CONV_C - 1, D_C)), _whole_out_spec((n_dec, D_B))]
        + [_whole_out_spec(s) for s in bf16_shapes],
        out_shape=[jax.ShapeDtypeStruct((n_dec, D_MODEL), F32),
                   jax.ShapeDtypeStruct((n_dec, CONV_A - 1, D_A), F32),
                   jax.ShapeDtypeStruct((n_dec, CONV_C - 1, D_C), F32),
                   jax.ShapeDtypeStruct((n_dec, D_B), F32)]
        + [jax.ShapeDtypeStruct(s, BF16) for s in bf16_shapes],
        compiler_params=_tc_params(),
        name="mixer_row",
    )(h, *[w for w, _ in weights])


def _ple(h, p, gple_ref, wpg_ref, wpp_ref, dot):
    gate = jax.nn.sigmoid(dot(_rms(h, gple_ref[...]), wpg_ref[...]))
    return h + gate * dot(p, wpp_ref[...])


def _ffn_dense_kernel(h_ref, p_ref, gffn_ref, wg_ref, wu_ref, wd_ref, gple_ref, wpg_ref,
                      wpp_ref, ride_ref, out_ref, ride_out_ref):
    ride_out_ref[...] = ride_ref[...].astype(BF16)
    h = h_ref[...]
    xn = _rms(h, gffn_ref[...]).astype(BF16)
    a = jax.nn.silu(_dot(xn, wg_ref[...])) * _dot(xn, wu_ref[...])
    h = h + _dot(a, wd_ref[...])
    out_ref[...] = _ple(h, p_ref[...], gple_ref, wpg_ref, wpp_ref, _dot)


def _ffn_dense(h, lw, ride):
    n_tok = h.shape[0]
    tm = TOK_ROWS
    p, layer = lw['p_seq']
    weights = [lw[k] for k in ('g_ffn', 'w_ff_gate', 'w_ff_up', 'w_ff_down', 'g_ple',
                               'w_ple_gate', 'w_ple_proj')]
    ride_in_spec, ride_out_spec, ride_out_shape = _ride_specs(ride, n_tok // tm)
    return pl.pallas_call(
        _ffn_dense_kernel,
        grid=(n_tok // tm,),
        in_specs=[pl.BlockSpec((tm, D_MODEL), lambda g: (g, 0)),
                  pl.BlockSpec((None, tm, D_PLE), lambda g: (layer, g, 0))]
        + [_layer_spec(*w) for w in weights] + [ride_in_spec],
        out_specs=[pl.BlockSpec((tm, D_MODEL), lambda g: (g, 0)), ride_out_spec],
        out_shape=[jax.ShapeDtypeStruct(h.shape, F32), ride_out_shape],
        compiler_params=_tc_params(),
        name="ffn_dense",
    )(h, p, *[w for w, _ in weights], ride)


def _ffn_row_kernel(h_ref, p_ref, gffn_ref, wg_ref, wu_ref, wd_ref, gple_ref, wpg_ref,
                    wpp_ref, out_ref, wg_bf_ref, wu_bf_ref, wd_bf_ref, xn_ref, acc_ref):
    j = pl.program_id(0)

    @pl.when(j == 0)
    def _():
        h = h_ref[...]
        xn_ref[...] = _rms(h, gffn_ref[...])
        acc_ref[...] = h

    for src, dst in ((wg_ref, wg_bf_ref), (wu_ref, wu_bf_ref), (wd_ref, wd_bf_ref)):
        dst[...] = src[...].astype(BF16)

    xn = xn_ref[...]
    a = jax.nn.silu(_dot_full(xn, wg_ref[...])) * _dot_full(xn, wu_ref[...])
    acc_ref[...] += _dot_full(a, wd_ref[...])

    @pl.when(j == pl.num_programs(0) - 1)
    def _():
        out_ref[...] = _ple(acc_ref[...], p_ref[...], gple_ref, wpg_ref, wpp_ref, _dot_full)


def _ffn_row(h, lw):
    n_dec = h.shape[0]
    d_ff = lw['w_ff_gate_f32'][0].shape[-1]
    ff = lw['w_ff_gate_f32'][1]
    return pl.pallas_call(
        _ffn_row_kernel,
        grid=(d_ff // FF_COLS,),
        in_specs=[_const_spec(h.shape), _layer_spec(*lw['p_row']), _layer_spec(*lw['g_ffn']),
                  pl.BlockSpec((None, D_MODEL, FF_COLS), lambda j: (ff, 0, j)),
                  pl.BlockSpec((None, D_MODEL, FF_COLS), lambda j: (ff, 0, j)),
                  pl.BlockSpec((None, FF_COLS, D_MODEL), lambda j: (ff, j, 0)),
                  _layer_spec(*lw['g_ple']), _layer_spec(*lw['w_ple_gate_f32']),
                  _layer_spec(*lw['w_ple_proj_f32'])],
        out_specs=[_whole_out_spec(h.shape),
                   pl.BlockSpec((D_MODEL, FF_COLS), lambda j: (0, j)),
                   pl.BlockSpec((D_MODEL, FF_COLS), lambda j: (0, j)),
                   pl.BlockSpec((FF_COLS, D_MODEL), lambda j: (j, 0))],
        out_shape=[jax.ShapeDtypeStruct(h.shape, F32),
                   jax.ShapeDtypeStruct((D_MODEL, d_ff), BF16),
                   jax.ShapeDtypeStruct((D_MODEL, d_ff), BF16),
                   jax.ShapeDtypeStruct((d_ff, D_MODEL), BF16)],
        scratch_shapes=[pltpu.VMEM((n_dec, D_MODEL), F32), pltpu.VMEM((n_dec, D_MODEL), F32)],
        compiler_params=_tc_params(),
        name="ffn_row",
    )(h, *[lw[k][0] for k in ('p_row', 'g_ffn', 'w_ff_gate_f32', 'w_ff_up_f32',
                              'w_ff_down_f32', 'g_ple', 'w_ple_gate_f32', 'w_ple_proj_f32')])


def _router_kernel(h_ref, gffn_ref, wr_ref, mi_ref, mf_ref, cnt_ref, carry_ref, *, n_tok):
    g = pl.program_id(0)
    tm = h_ref.shape[0]

    @pl.when(g == 0)
    def _():
        carry_ref[...] = jnp.zeros(carry_ref.shape, F32)

    valid_row = (g * tm + lax.broadcasted_iota(I32, (tm, 1), 0)) < n_tok
    xn = _rms(jnp.where(valid_row, h_ref[...], 0.0), gffn_ref[...])
    logits = _dot_full(xn, wr_ref[...]).T[:N_EXPERTS]
    valid = (g * tm + lax.broadcasted_iota(I32, (1, tm), 1)) < n_tok
    e = lax.broadcasted_iota(I32, logits.shape, 0)
    m1 = jnp.max(logits, axis=0, keepdims=True)
    i1 = jnp.min(jnp.where(logits == m1, e, N_EXPERTS), axis=0, keepdims=True)
    rest = jnp.where(e == i1, -jnp.inf, logits)
    m2 = jnp.max(rest, axis=0, keepdims=True)
    i2 = jnp.min(jnp.where(rest == m2, e, N_EXPERTS), axis=0, keepdims=True)
    e2 = jnp.exp(m2 - m1)
    denom = 1.0 + e2
    w1 = 1.0 / denom
    w2 = e2 / denom

    oh1 = jnp.where((e == i1) & valid, 1.0, 0.0)
    oh2 = jnp.where((e == i2) & valid, 1.0, 0.0)
    member = oh1 + oh2
    r = lax.broadcasted_iota(I32, (tm, tm), 0)
    c = lax.broadcasted_iota(I32, (tm, tm), 1)
    earlier = jnp.where(r < c, 1.0, 0.0).astype(BF16)
    pos = _dot(member, earlier) + carry_ref[...]
    pos1 = jnp.sum(oh1 * pos, axis=0, keepdims=True).astype(I32)
    pos2 = jnp.sum(oh2 * pos, axis=0, keepdims=True).astype(I32)
    carry_ref[...] = carry_ref[...] + jnp.sum(member, axis=1, keepdims=True)
    cnt_ref[...] = carry_ref[...]

    mi_ref[...] = jnp.where(e == 0, i1, jnp.where(e == 1, i2,
                            jnp.where(e == 2, pos1, jnp.where(e == 3, pos2, 0))))
    gates = jnp.where(e == 0, w1, jnp.where(e == 1, w2, 0.0))
    lanes = wr_ref.shape[1]
    gates = jnp.concatenate([gates, jnp.zeros((lanes - N_EXPERTS, tm), F32)], axis=0)
    mf_ref[...] = gates.T[:, :N_EXPERTS]


def _router(h, tile0, n_tok, lw):
    tm = TOK_ROWS
    return pl.pallas_call(
        functools.partial(_router_kernel, n_tok=n_tok),
        grid=(pl.cdiv(n_tok, tm),),
        in_specs=[pl.BlockSpec((tm, D_MODEL), lambda g: (tile0 + g, 0)),
                  _layer_spec(*lw['g_ffn']), _layer_spec(*lw['w_router'])],
        out_specs=[pl.BlockSpec((N_EXPERTS, tm), lambda g: (0, g)),
                   pl.BlockSpec((tm, N_EXPERTS), lambda g: (g, 0)),
                   _whole_out_spec((N_EXPERTS, 1))],
        out_shape=[jax.ShapeDtypeStruct((N_EXPERTS, n_tok), I32),
                   jax.ShapeDtypeStruct((n_tok, N_EXPERTS), F32),
                   jax.ShapeDtypeStruct((N_EXPERTS, 1), F32)],
        scratch_shapes=[pltpu.VMEM((N_EXPERTS, 1), F32)],
        compiler_params=_tc_params(),
        name="router",
    )(h, lw['g_ffn'][0], lw['w_router'][0])


def _sc_chunk(n_rows):
    per_worker = pl.cdiv(n_rows, SC_WORKERS)
    return pl.cdiv(per_worker, SC_WINDOW) * SC_WINDOW


def _sc_worker_base(n_rows, chunk):
    wid = lax.axis_index("s") * SC_CORES + lax.axis_index("c")
    return jnp.minimum(wid * chunk, n_rows - chunk)


def _sc_scatter_rows(x, row0, dest, n_out):
    n = dest.shape[0] // TOP_K
    chunk = _sc_chunk(n)
    mesh = plsc.VectorSubcoreMesh(core_axis_name="c", subcore_axis_name="s")

    @functools.partial(
        pl.kernel, mesh=mesh,
        out_type=jax.ShapeDtypeStruct((n_out, D_MODEL), x.dtype),
        scratch_types=[pltpu.VMEM((SC_WINDOW,), I32) for _ in range(TOP_K)]
        + [pltpu.VMEM((SC_WINDOW, D_MODEL), x.dtype), pltpu.SemaphoreType.DMA],
        name="sc_scatter_rows",
    )
    def scatter(x_hbm, dest_hbm, out_hbm, idx0_v, idx1_v, rows_v, sem):
        base = _sc_worker_base(n, chunk)

        @pl.loop(0, chunk // SC_WINDOW)
        def _(j):
            off = pl.multiple_of(base + j * SC_WINDOW, 8)
            pltpu.sync_copy(dest_hbm.at[pl.ds(off, SC_WINDOW)], idx0_v)
            pltpu.sync_copy(dest_hbm.at[pl.ds(n + off, SC_WINDOW)], idx1_v)
            pltpu.sync_copy(x_hbm.at[pl.ds(row0 + off, SC_WINDOW)], rows_v)
            first = pltpu.async_copy(rows_v, out_hbm.at[idx0_v], sem)
            second = pltpu.async_copy(rows_v, out_hbm.at[idx1_v], sem)
            first.wait()
            second.wait()

    return scatter(x, dest)


def _sc_gather_rows(y, idx):
    n = idx.shape[0]
    chunk = _sc_chunk(n)
    mesh = plsc.VectorSubcoreMesh(core_axis_name="c", subcore_axis_name="s")

    @functools.partial(
        pl.kernel, mesh=mesh,
        out_type=jax.ShapeDtypeStruct((n, D_MODEL), y.dtype),
        scratch_types=[pltpu.VMEM((SC_WINDOW,), I32),
                       pltpu.VMEM((SC_WINDOW, D_MODEL), y.dtype), pltpu.SemaphoreType.DMA],
        name="sc_gather_rows",
    )
    def gather(y_hbm, idx_hbm, out_hbm, idx_v, rows_v, sem):
        base = _sc_worker_base(n, chunk)

        @pl.loop(0, chunk // SC_WINDOW)
        def _(j):
            off = pl.multiple_of(base + j * SC_WINDOW, 8)
            pltpu.sync_copy(idx_hbm.at[pl.ds(off, SC_WINDOW)], idx_v)
            pltpu.async_copy(y_hbm.at[idx_v], rows_v, sem).wait()
            pltpu.sync_copy(rows_v, out_hbm.at[pl.ds(off, SC_WINDOW)])

    return gather(y, idx)


def _expert_kernel(tile_expert_ref, n_valid_ref, xs_ref, gffn_ref, wg_ref, wu_ref, wd_ref,
                   y_ref):
    @pl.when(pl.program_id(0) < n_valid_ref[0])
    def _():
        xn = _rms(xs_ref[...], gffn_ref[...]).astype(BF16)
        a = jax.nn.silu(_dot(xn, wg_ref[...])) * _dot(xn, wu_ref[...])
        y_ref[...] = _dot(a, wd_ref[...])


def _experts(xs, tile_expert, n_valid, lw):
    n_slots = xs.shape[0]
    te = EXPERT_ROWS
    d_exp = lw['w_ex_gate'][0].shape[-1]
    moe = lw['w_ex_gate'][1]
    row_block = lambda g, tex, nv: (jnp.minimum(g, nv[0] - 1), 0)
    w_block = lambda g, tex, nv: (moe, tex[g], 0, 0)
    return pl.pallas_call(
        _expert_kernel,
        grid_spec=pltpu.PrefetchScalarGridSpec(
            num_scalar_prefetch=2,
            grid=(n_slots // te,),
            in_specs=[pl.BlockSpec((te, D_MODEL), row_block),
                      _layer_spec(*lw['g_ffn']),
                      pl.BlockSpec((None, None, D_MODEL, d_exp), w_block),
                      pl.BlockSpec((None, None, D_MODEL, d_exp), w_block),
                      pl.BlockSpec((None, None, d_exp, D_MODEL), w_block)],
            out_specs=pl.BlockSpec((te, D_MODEL), row_block)),
        out_shape=jax.ShapeDtypeStruct((n_slots, D_MODEL), F32),
        compiler_params=_tc_params(),
        name="experts",
    )(tile_expert, n_valid, xs, *[lw[k][0] for k in ('g_ffn', 'w_ex_gate', 'w_ex_up',
                                                        'w_ex_down')])


def _combine_math(h_ref, yg_ref, mf_ref, p, gple_ref, wpg_ref, wpp_ref, gfin_ref):
    gates = mf_ref[...]
    h = h_ref[...] + (gates[:, 0:1] * yg_ref[0] + gates[:, 1:2] * yg_ref[1])
    return _rms(_ple(h, p, gple_ref, wpg_ref, wpp_ref, _dot), gfin_ref[...])


def _combine_seq_kernel(h_ref, yg_ref, mf_ref, pp_ref, gple_ref, wpg_ref, wpp_ref, gfin_ref,
                        ybuf_ref, yp_ref):
    del ybuf_ref
    yp_ref[...] = _combine_math(h_ref, yg_ref, mf_ref, pp_ref[...], gple_ref, wpg_ref,
                                wpp_ref, gfin_ref)


def _combine_tail_kernel(h_ref, yg_ref, mf_ref, pp_ref, ps_ref, gple_ref, wpg_ref, wpp_ref,
                         gfin_ref, ybuf_ref, yp_ref, ys_ref, pbuf_ref):
    del ybuf_ref
    g = pl.program_id(0)
    last = pl.num_programs(0) - 1
    n_dec = ps_ref.shape[0]
    pbuf_ref[...] = pp_ref[...]

    @pl.when(g == last)
    def _():
        pbuf_ref[0:n_dec, :] = ps_ref[...]

    out = _combine_math(h_ref, yg_ref, mf_ref, pbuf_ref[...], gple_ref, wpg_ref, wpp_ref,
                        gfin_ref)

    @pl.when(g < last)
    def _():
        yp_ref[...] = out

    @pl.when(g == last)
    def _():
        ys_ref[...] = out[0:n_dec, :]


def _combine(h, tile0, yg, mf, y_seq, lw, with_rows):
    tm = TOK_ROWS
    n_tiles = pl.cdiv(yg.shape[1], tm)
    last_seq_tile = y_seq.shape[0] // tm - 1
    pp, layer = lw['p_seq']
    n_dec = lw['p_row'][0].shape[-2]
    weights = [lw[k] for k in (('p_row',) if with_rows else ())
               + ('g_ple', 'w_ple_gate', 'w_ple_proj', 'g_final')]
    seq_tile = lambda g: (jnp.minimum(tile0 + g, last_seq_tile), 0)
    n_in = 4 + len(weights)
    out = pl.pallas_call(
        _combine_tail_kernel if with_rows else _combine_seq_kernel,
        grid=(n_tiles,),
        in_specs=[pl.BlockSpec((tm, D_MODEL), lambda g: (tile0 + g, 0)),
                  pl.BlockSpec((TOP_K, tm, D_MODEL), lambda g: (0, g, 0)),
                  pl.BlockSpec((tm, N_EXPERTS), lambda g: (g, 0)),
                  pl.BlockSpec((None, tm, D_PLE), lambda g: (layer,) + seq_tile(g))]
        + [_layer_spec(*w) for w in weights]
        + [pl.BlockSpec(memory_space=pl.ANY)],
        out_specs=[pl.BlockSpec((tm, D_MODEL), seq_tile)]
        + ([_whole_out_spec((n_dec, D_MODEL))] if with_rows else []),
        out_shape=[jax.ShapeDtypeStruct(y_seq.shape, F32)]
        + ([jax.ShapeDtypeStruct((n_dec, D_MODEL), F32)] if with_rows else []),
        scratch_shapes=[pltpu.VMEM((tm, D_PLE), F32)] if with_rows else [],
        input_output_aliases={n_in: 0},
        compiler_params=_tc_params(),
        name="combine",
    )(h, yg, mf, pp, *[w for w, _ in weights], y_seq)
    return out if with_rows else (out[0], None)


def _moe_layer(h, y_seq, lw):
    n_tok = h.shape[0]
    tm = TOK_ROWS
    te = EXPERT_ROWS
    tiles = pl.cdiv(n_tok, tm)
    bounds = [tiles * c // MOE_CHUNKS for c in range(MOE_CHUNKS + 1)]
    y_rows = None
    for c in range(MOE_CHUNKS):
        tile0 = bounds[c]
        n = min(bounds[c + 1] * tm, n_tok) - tile0 * tm
        mi, mf, counts = _router(h, tile0, n, lw)

        cnt = counts[:, 0].astype(I32)
        padded = (cnt + te - 1) // te * te
        ends = jnp.cumsum(padded)
        starts = ends - padded
        experts = jnp.arange(N_EXPERTS, dtype=I32)
        start_of = lambda e: jnp.sum(
            jnp.where(e[None, :] == experts[:, None], starts[:, None], 0), axis=0)
        dest = jnp.concatenate([start_of(mi[0]) + mi[2], start_of(mi[1]) + mi[3]])
        n_tiles = pl.cdiv(TOP_K * n + N_EXPERTS * (te - 1), te)
        tile_start = jnp.arange(n_tiles, dtype=I32) * te
        last_used = jnp.max(jnp.where(padded > 0, experts, 0))
        tile_expert = jnp.minimum(
            jnp.sum(tile_start[:, None] >= ends[None, :], axis=-1).astype(I32), last_used)
        n_valid = (ends[-1:] // te).astype(I32)

        xs = _sc_scatter_rows(h, tile0 * tm, dest, n_tiles * te)
        y = _experts(xs, tile_expert, n_valid, lw)
        yg = _sc_gather_rows(y, dest).reshape(TOP_K, n, D_MODEL)
        y_seq, rows = _combine(h, tile0, yg, mf, y_seq, lw, with_rows=c == MOE_CHUNKS - 1)
        y_rows = rows if rows is not None else y_rows
    return y_seq, y_rows


def kernel(x_prompt, x_sample, state_conv_a, state_conv_c, p_prompt, p_sample, g_mix, w_in, w_conv_a, w_s, b_s, g_ln_b, b_ln_b, w_conv_c, b_conv_c, g_ln_c, b_ln_c, g_out, w_o, g_ffn, w_ff_gate, w_ff_up, w_ff_down, w_router, w_ex_gate, w_ex_up, w_ex_down, g_ple, w_ple_gate, w_ple_proj, g_final):
    depth = g_mix.shape[0]
    n_seq, seq, _ = x_prompt.shape
    n_dec = x_sample.shape[0]
    n_prompt = n_seq * seq
    assert depth == 2 and x_sample.shape[1] == 1
    assert seq % TOK_ROWS == 0 and TOK_ROWS % n_dec == 0 and w_ff_gate.shape[-1] % FF_COLS == 0

    vec = lambda x: x.reshape(x.shape[0], 1, -1)
    per_layer = {
        'g_mix': vec(g_mix), 'w_in_f32': w_in, 'w_conv_a': w_conv_a,
        'w_s_cat': jnp.transpose(w_s, (0, 2, 1, 3)).reshape(depth, CHUNK, N_HEADS_B * CHUNK),
        'b_s_full': jnp.repeat(jnp.swapaxes(b_s, 1, 2), HEAD_DIM, axis=2),
        'g_ln_b': vec(g_ln_b), 'b_ln_b': vec(b_ln_b), 'w_conv_c': w_conv_c,
        'b_conv_c': vec(b_conv_c), 'g_ln_c': vec(g_ln_c), 'b_ln_c': vec(b_ln_c),
        'g_out': vec(g_out), 'w_o_f32': w_o, 'g_ffn': vec(g_ffn), 'g_ple': vec(g_ple),
        'w_ple_gate_f32': w_ple_gate, 'w_ple_proj_f32': w_ple_proj,
        'state_a': jnp.swapaxes(state_conv_a, 1, 2), 'state_c': jnp.swapaxes(state_conv_c, 1, 2),
        'p_seq': p_prompt.reshape(depth, n_prompt, D_PLE),
        'p_row': p_sample.reshape(depth, n_dec, D_PLE),
    }
    per_dense = {'w_ff_gate_f32': w_ff_gate, 'w_ff_up_f32': w_ff_up, 'w_ff_down_f32': w_ff_down}
    one_layer = lambda w: (w.reshape((1,) + w.shape), 0)
    per_moe = {'w_router': jnp.pad(w_router, ((0, 0), (0, 0), (0, LANES - N_EXPERTS)))}
    to_convert = [('w_ex_gate', w_ex_gate), ('w_ex_up', w_ex_up), ('w_ex_down', w_ex_down)]

    def with_ride(call):
        name, w = to_convert.pop(0)
        *outputs, w_bf16 = call(w.reshape(-1, w.shape[-1]))
        per_moe[name] = w_bf16.reshape(w.shape)
        return outputs

    hp = x_prompt.reshape(n_prompt, D_MODEL)
    hs = x_sample.reshape(n_dec, D_MODEL)
    outs = {k: [] for k in ('a_p', 'a_s', 'c_p', 'c_s', 'v_s')}
    for i in range(depth):
        is_expert_layer = i % 2 == 1
        lw = {k: (v, i) for k, v in per_layer.items()}
        lw.update({k: (v, i // 2) for k, v in (per_dense if not is_expert_layer else {}).items()})
        lw['g_final'] = (g_final.reshape(1, 1, -1), 0)

        hs, a_s, c_s, v_s, *converted = _mixer_row(hs, 0, lw)
        lw.update({k: one_layer(w) for k, w in zip(CONVERTED_BY_MIXER_ROW, converted)})
        mixer_in = hp
        hp, a_p, c_p = with_ride(functools.partial(
            _mixer_seq, hp, hs if is_expert_layer else None, lw, n_seq, seq))
        for k, v in zip(('a_p', 'c_p', 'a_s', 'c_s', 'v_s'), (a_p, c_p, a_s, c_s, v_s)):
            outs[k].append(v)
        if not is_expert_layer:
            hs, *converted = _ffn_row(hs, lw)
            lw.update({k: one_layer(w) for k, w in
                       zip(('w_ff_gate', 'w_ff_up', 'w_ff_down'), converted)})
            hp, = with_ride(functools.partial(_ffn_dense, hp, lw))
        else:
            lw.update({k: (v, i // 2) for k, v in per_moe.items()})
            y_prompt, y_sample = _moe_layer(hp, mixer_in, lw)

    return (y_prompt.reshape(x_prompt.shape), y_sample.reshape(x_sample.shape),
            jnp.stack(outs['a_p']), jnp.stack(outs['a_s']),
            jnp.stack(outs['c_p']), jnp.stack(outs['c_s']),
            jnp.stack(outs['v_s']).reshape(depth, n_dec, 1, D_B))
```

```python
import functools

import jax
import jax.numpy as jnp
from jax import lax
from jax.experimental import pallas as pl
from jax.experimental.pallas import tpu as pltpu
from jax.experimental.pallas import tpu_sc as plsc

D_MODEL = 1024
HEAD_DIM = 64
D_A = 384
D_B = 256
D_C = 384
N_HEADS_B = D_B // HEAD_DIM
CONV_A = 3
CONV_C = 31
CHUNK = 128
D_IN = 3 * D_A + 2 * D_B + 2 * D_C
D_PLE = 256
N_EXPERTS = 8
TOP_K = 2
EPS = 1e-6

O1, O2, O3 = D_A, 2 * D_A, 3 * D_A
O4 = O3 + 2 * D_B

F32 = jnp.float32
BF16 = jnp.bfloat16
I32 = jnp.int32

VMEM_LIMIT_BYTES = 56 * 1024 * 1024
SUBLANES = 8
LANES = 128

TOK_ROWS = 512
CONV_ROWS = 64
A_HALO = 8
C_HALO = 32
EXPERT_ROWS = 512
FF_COLS = 256
MOE_CHUNKS = 2

SC_CORES = 2
SC_WORKERS = 32
SC_WINDOW = 48


def _rms(x, g):
    return x * lax.rsqrt(jnp.mean(x * x, axis=-1, keepdims=True) + EPS) * g


def _ln(x, g, b):
    mu = jnp.mean(x, axis=-1, keepdims=True)
    xc = x - mu
    var = jnp.mean(xc * xc, axis=-1, keepdims=True)
    return xc * lax.rsqrt(var + EPS) * g + b


def _dot(a, b):
    return jnp.dot(a.astype(BF16), b, preferred_element_type=F32)


def _split(x):
    hi = x.astype(BF16)
    return hi, (x - hi.astype(F32)).astype(BF16)


def _dot_full(a, b):
    a_hi, a_lo = _split(a)
    b_hi, b_lo = _split(b)
    m = a.shape[0]
    by_hi = jnp.dot(jnp.concatenate([a_hi, a_lo], axis=0), b_hi, preferred_element_type=F32)
    return by_hi[:m] + by_hi[m:] + jnp.dot(a_hi, b_lo, preferred_element_type=F32)


def _const_spec(shape):
    return pl.BlockSpec(shape, lambda *_: (0,) * len(shape), pipeline_mode=pl.Buffered(1))


def _layer_spec(stacked, layer):
    idx = (layer,) if isinstance(layer, int) else tuple(layer)
    rest = stacked.shape[len(idx):]
    return pl.BlockSpec((None,) * len(idx) + rest, lambda *_: idx + (0,) * len(rest),
                        pipeline_mode=pl.Buffered(1))


def _whole_out_spec(shape):
    return pl.BlockSpec(shape, lambda *_: (0,) * len(shape))


def _tc_params():
    return pltpu.CompilerParams(dimension_semantics=("arbitrary",),
                                vmem_limit_bytes=VMEM_LIMIT_BYTES)


def _normed_groups(y_a, y_b, y_c, gout_ref):
    return jnp.concatenate([_rms(y_a, gout_ref[:, :D_A]),
                            _rms(y_b, gout_ref[:, D_A:D_A + D_B]),
                            _rms(y_c, gout_ref[:, D_A + D_B:])], axis=-1)


def _mixer_seq_kernel(hp_ref, hs_ref, gmix_ref, win_ref, wca_ref, wscat_ref, bsfull_ref,
                      glnb_ref, blnb_ref, wcc_ref, bcc_ref, glnc_ref, blnc_ref, gout_ref,
                      wo_ref, ride_ref, hout_ref, newa_ref, newc_ref, ride_out_ref,
                      qext_ref, gext_ref, gshift_ref, conv_ref, wcc8_ref, *, tiles_per_seq,
                      n_seq_tiles):
    g = pl.program_id(0)
    tm = hp_ref.shape[0]

    @pl.when(g == 0)
    def _():
        for k in range(CONV_C):
            wcc8_ref[k] = jnp.broadcast_to(wcc_ref[k:k + 1, :], (SUBLANES, D_C))

    @pl.when(g < n_seq_tiles)
    def _sequence_tile():
        ride_out_ref[...] = ride_ref[...].astype(BF16)

        @pl.when(g % tiles_per_seq == 0)
        def _():
            qext_ref[0:A_HALO, :] = jnp.zeros((A_HALO, D_A), F32)
            gext_ref[0:C_HALO, :] = jnp.zeros((C_HALO, D_C), F32)

        h = hp_ref[...]
        z = _dot(_rms(h, gmix_ref[...]), win_ref[...])

        qext_ref[A_HALO:A_HALO + tm, :] = z[:, O1:O2] * z[:, O2:O3]
        conv_a = jnp.zeros((tm, D_A), F32)
        for k in range(CONV_A):
            off = A_HALO - (CONV_A - 1) + k
            conv_a = conv_a + wca_ref[k:k + 1, :] * qext_ref[off:off + tm, :]
        y_a = z[:, :O1] * conv_a
        last_q = qext_ref[A_HALO + tm - (CONV_A - 1):A_HALO + tm, :]
        newa_ref[...] = last_q
        qext_ref[A_HALO - (CONV_A - 1):A_HALO, :] = last_q

        zb = jax.nn.gelu(z[:, O3:O4])
        v = _ln(zb[:, D_B:], glnb_ref[...], blnb_ref[...])
        row = lax.broadcasted_iota(I32, (CHUNK, N_HEADS_B * CHUNK), 0)
        col = lax.broadcasted_iota(I32, (CHUNK, N_HEADS_B * CHUNK), 1)
        w_tril = jnp.where((col % CHUNK) <= row, wscat_ref[...], 0.0).astype(BF16)
        lane_head = lax.broadcasted_iota(I32, (CHUNK, D_B), 1) // HEAD_DIM
        s_chunks = []
        for c in range(tm // CHUNK):
            vc = v[c * CHUNK:(c + 1) * CHUNK, :]
            vstack = jnp.concatenate(
                [jnp.where(lane_head == hd, vc, 0.0) for hd in range(N_HEADS_B)], axis=0)
            s_chunks.append(_dot(w_tril, vstack.astype(BF16)) + bsfull_ref[...])
        y_b = zb[:, :D_B] * jnp.concatenate(s_chunks, axis=0)

        gext_ref[C_HALO:C_HALO + tm, :] = (z[:, O4:O4 + D_C]
                                           * jax.nn.sigmoid(z[:, O4 + D_C:]))
        n_shift = gshift_ref.shape[1]
        for s in range(1, SUBLANES):
            gshift_ref[s - 1] = gext_ref[s:s + n_shift, :]
        base = C_HALO - (CONV_C - 1)
        for r0 in range(0, tm, CONV_ROWS):
            acc = jnp.zeros((CONV_ROWS // SUBLANES, SUBLANES, D_C), F32)
            for k in range(CONV_C):
                lo = (base + k) // SUBLANES * SUBLANES + r0
                s = (base + k) % SUBLANES
                window = (gext_ref[lo:lo + CONV_ROWS, :] if s == 0
                          else gshift_ref[s - 1, lo:lo + CONV_ROWS, :])
                acc = acc + wcc8_ref[k][None] * window.reshape(acc.shape)
            conv_ref[r0:r0 + CONV_ROWS, :] = acc.reshape(CONV_ROWS, D_C)
        y_c = jax.nn.silu(_ln(conv_ref[...] + bcc_ref[...], glnc_ref[...], blnc_ref[...]))
        last_g = gext_ref[C_HALO + tm - (CONV_C - 1):C_HALO + tm, :]
        newc_ref[...] = last_g
        gext_ref[C_HALO - (CONV_C - 1):C_HALO, :] = last_g

        hout_ref[...] = h + _dot(_normed_groups(y_a, y_b, y_c, gout_ref), wo_ref[...])

    @pl.when(g == n_seq_tiles)
    def _append_sample_rows():
        hout_ref[0:hs_ref.shape[0], :] = hs_ref[...]


def _ride_specs(ride, n_steps):
    rows = ride.shape[0] // n_steps
    assert rows * n_steps == ride.shape[0]
    slab = lambda g: (jnp.minimum(g, n_steps - 1), 0)
    return (pl.BlockSpec((rows, ride.shape[1]), slab), pl.BlockSpec((rows, ride.shape[1]), slab),
            jax.ShapeDtypeStruct(ride.shape, BF16))


def _mixer_seq(hp, hs, lw, n_seq, seq, ride):
    tm = TOK_ROWS
    tiles_per_seq = seq // tm
    n_seq_tiles = n_seq * tiles_per_seq
    join = hs is not None
    if not join:
        hs = jnp.zeros((SUBLANES, D_MODEL), F32)
    n_out = n_seq * seq + (hs.shape[0] if join else 0)
    weights = [lw[k] for k in ('g_mix', 'w_in', 'w_conv_a', 'w_s_cat', 'b_s_full', 'g_ln_b',
                               'b_ln_b', 'w_conv_c', 'b_conv_c', 'g_ln_c', 'b_ln_c', 'g_out',
                               'w_o')]
    seq_of = lambda g: jnp.minimum(g // tiles_per_seq, n_seq - 1)
    n_shift = tm + C_HALO - SUBLANES
    ride_in_spec, ride_out_spec, ride_out_shape = _ride_specs(ride, n_seq_tiles)
    return pl.pallas_call(
        functools.partial(_mixer_seq_kernel, tiles_per_seq=tiles_per_seq,
                          n_seq_tiles=n_seq_tiles),
        grid=(n_seq_tiles + int(join),),
        in_specs=[pl.BlockSpec((tm, D_MODEL), lambda g: (jnp.minimum(g, n_seq_tiles - 1), 0)),
                  _const_spec(hs.shape)]
        + [_layer_spec(*w) for w in weights] + [ride_in_spec],
        out_specs=[pl.BlockSpec((tm, D_MODEL), lambda g: (g, 0)),
                   pl.BlockSpec((None, CONV_A - 1, D_A), lambda g: (seq_of(g), 0, 0)),
                   pl.BlockSpec((None, CONV_C - 1, D_C), lambda g: (seq_of(g), 0, 0)),
                   ride_out_spec],
        out_shape=[jax.ShapeDtypeStruct((n_out, D_MODEL), F32),
                   jax.ShapeDtypeStruct((n_seq, CONV_A - 1, D_A), F32),
                   jax.ShapeDtypeStruct((n_seq, CONV_C - 1, D_C), F32),
                   ride_out_shape],
        scratch_shapes=[pltpu.VMEM((A_HALO + tm, D_A), F32),
                        pltpu.VMEM((C_HALO + tm, D_C), F32),
                        pltpu.VMEM((SUBLANES - 1, n_shift, D_C), F32),
                        pltpu.VMEM((tm, D_C), F32),
                        pltpu.VMEM((CONV_C, SUBLANES, D_C), F32)],
        compiler_params=_tc_params(),
        name="mixer_seq",
    )(hp, hs, *[w for w, _ in weights], ride)


def _mixer_row_kernel(h_ref, sa_ref, sc_ref, gmix_ref, win_ref, wca_ref, wscat_ref,
                      bsfull_ref, glnb_ref, blnb_ref, wcc_ref, bcc_ref, glnc_ref, blnc_ref,
                      gout_ref, wo_ref, wpg_ref, wpp_ref, hout_ref, newa_ref, newc_ref, v_ref,
                      win_bf_ref, wo_bf_ref, wpg_bf_ref, wpp_bf_ref):
    for src, dst in ((win_ref, win_bf_ref), (wo_ref, wo_bf_ref), (wpg_ref, wpg_bf_ref),
                     (wpp_ref, wpp_bf_ref)):
        dst[...] = src[...].astype(BF16)

    h = h_ref[...]
    z = _dot_full(_rms(h, gmix_ref[...]), win_ref[...])

    q = z[:, O1:O2] * z[:, O2:O3]
    conv_a = wca_ref[CONV_A - 1:CONV_A, :] * q
    for k in range(CONV_A - 1):
        conv_a = conv_a + wca_ref[k:k + 1, :] * sa_ref[k]
    y_a = z[:, :O1] * conv_a
    for k in range(CONV_A - 2):
        newa_ref[:, k, :] = sa_ref[k + 1]
    newa_ref[:, CONV_A - 2, :] = q

    zb = jax.nn.gelu(z[:, O3:O4])
    v = _ln(zb[:, D_B:], glnb_ref[...], blnb_ref[...])
    v_ref[...] = v
    w_diag0 = jnp.concatenate(
        [jnp.broadcast_to(wscat_ref[0:1, hd * CHUNK:hd * CHUNK + 1], (1, HEAD_DIM))
         for hd in range(N_HEADS_B)], axis=-1)
    y_b = zb[:, :D_B] * (w_diag0 * v + bsfull_ref[0:1, :])

    glu = z[:, O4:O4 + D_C] * jax.nn.sigmoid(z[:, O4 + D_C:])
    conv_c = wcc_ref[CONV_C - 1:CONV_C, :] * glu
    for k in range(CONV_C - 1):
        conv_c = conv_c + wcc_ref[k:k + 1, :] * sc_ref[k]
    y_c = jax.nn.silu(_ln(conv_c + bcc_ref[...], glnc_ref[...], blnc_ref[...]))
    for k in range(CONV_C - 2):
        newc_ref[:, k, :] = sc_ref[k + 1]
    newc_ref[:, CONV_C - 2, :] = glu

    hout_ref[...] = h + _dot_full(_normed_groups(y_a, y_b, y_c, gout_ref), wo_ref[...])


CONVERTED_BY_MIXER_ROW = ('w_in', 'w_o', 'w_ple_gate', 'w_ple_proj')


def _mixer_row(h, h_block, lw):
    n_dec = lw['state_a'][0].shape[-2]
    weights = [lw[k] for k in ('state_a', 'state_c', 'g_mix', 'w_in_f32', 'w_conv_a', 'w_s_cat',
                               'b_s_full', 'g_ln_b', 'b_ln_b', 'w_conv_c', 'b_conv_c', 'g_ln_c',
                               'b_ln_c', 'g_out', 'w_o_f32', 'w_ple_gate_f32', 'w_ple_proj_f32')]
    bf16_shapes = [lw[k + '_f32'][0].shape[1:] for k in CONVERTED_BY_MIXER_ROW]
    return pl.pallas_call(
        _mixer_row_kernel,
        grid=(1,),
        in_specs=[pl.BlockSpec((n_dec, D_MODEL), lambda g: (h_block, 0))]
        + [_layer_spec(*w) for w in weights],
        out_specs=[_whole_out_spec((n_dec, D_MODEL)), _whole_out_spec((n_dec, CONV_A - 1, D_A)),
                   _whole_out_spec((n_dec, CONV_C - 1, D_C)), _whole_out_spec((n_dec, D_B))]
        + [_whole_out_spec(s) for s in bf16_shapes],
        out_shape=[jax.ShapeDtypeStruct((n_dec, D_MODEL), F32),
                   jax.ShapeDtypeStruct((n_dec, CONV_A - 1, D_A), F32),
                   jax.ShapeDtypeStruct((n_dec, CONV_C - 1, D_C), F32),
                   jax.ShapeDtypeStruct((n_dec, D_B), F32)]
        + [jax.ShapeDtypeStruct(s, BF16) for s in bf16_shapes],
        compiler_params=_tc_params(),
        name="mixer_row",
    )(h, *[w for w, _ in weights])


def _ple(h, p, gple_ref, wpg_ref, wpp_ref, dot):
    gate = jax.nn.sigmoid(dot(_rms(h, gple_ref[...]), wpg_ref[...]))
    return h + gate * dot(p, wpp_ref[...])


def _ffn_dense_kernel(h_ref, p_ref, gffn_ref, wg_ref, wu_ref, wd_ref, gple_ref, wpg_ref,
                      wpp_ref, ride_ref, out_ref, ride_out_ref):
    ride_out_ref[...] = ride_ref[...].astype(BF16)
    h = h_ref[...]
    xn = _rms(h, gffn_ref[...]).astype(BF16)
    a = jax.nn.silu(_dot(xn, wg_ref[...])) * _dot(xn, wu_ref[...])
    h = h + _dot(a, wd_ref[...])
    out_ref[...] = _ple(h, p_ref[...], gple_ref, wpg_ref, wpp_ref, _dot)


def _ffn_dense(h, lw, ride):
    n_tok = h.shape[0]
    tm = TOK_ROWS
    p, layer = lw['p_seq']
    weights = [lw[k] for k in ('g_ffn', 'w_ff_gate', 'w_ff_up', 'w_ff_down', 'g_ple',
                               'w_ple_gate', 'w_ple_proj')]
    ride_in_spec, ride_out_spec, ride_out_shape = _ride_specs(ride, n_tok // tm)
    return pl.pallas_call(
        _ffn_dense_kernel,
        grid=(n_tok // tm,),
        in_specs=[pl.BlockSpec((tm, D_MODEL), lambda g: (g, 0)),
                  pl.BlockSpec((None, tm, D_PLE), lambda g: (layer, g, 0))]
        + [_layer_spec(*w) for w in weights] + [ride_in_spec],
        out_specs=[pl.BlockSpec((tm, D_MODEL), lambda g: (g, 0)), ride_out_spec],
        out_shape=[jax.ShapeDtypeStruct(h.shape, F32), ride_out_shape],
        compiler_params=_tc_params(),
        name="ffn_dense",
    )(h, p, *[w for w, _ in weights], ride)


def _ffn_row_kernel(h_ref, p_ref, gffn_ref, wg_ref, wu_ref, wd_ref, gple_ref, wpg_ref,
                    wpp_ref, out_ref, wg_bf_ref, wu_bf_ref, wd_bf_ref, xn_ref, acc_ref):
    j = pl.program_id(0)

    @pl.when(j == 0)
    def _():
        h = h_ref[...]
        xn_ref[...] = _rms(h, gffn_ref[...])
        acc_ref[...] = h

    for src, dst in ((wg_ref, wg_bf_ref), (wu_ref, wu_bf_ref), (wd_ref, wd_bf_ref)):
        dst[...] = src[...].astype(BF16)

    xn = xn_ref[...]
    a = jax.nn.silu(_dot_full(xn, wg_ref[...])) * _dot_full(xn, wu_ref[...])
    acc_ref[...] += _dot_full(a, wd_ref[...])

    @pl.when(j == pl.num_programs(0) - 1)
    def _():
        out_ref[...] = _ple(acc_ref[...], p_ref[...], gple_ref, wpg_ref, wpp_ref, _dot_full)


def _ffn_row(h, lw):
    n_dec = h.shape[0]
    d_ff = lw['w_ff_gate_f32'][0].shape[-1]
    ff = lw['w_ff_gate_f32'][1]
    return pl.pallas_call(
        _ffn_row_kernel,
        grid=(d_ff // FF_COLS,),
        in_specs=[_const_spec(h.shape), _layer_spec(*lw['p_row']), _layer_spec(*lw['g_ffn']),
                  pl.BlockSpec((None, D_MODEL, FF_COLS), lambda j: (ff, 0, j)),
                  pl.BlockSpec((None, D_MODEL, FF_COLS), lambda j: (ff, 0, j)),
                  pl.BlockSpec((None, FF_COLS, D_MODEL), lambda j: (ff, j, 0)),
                  _layer_spec(*lw['g_ple']), _layer_spec(*lw['w_ple_gate_f32']),
                  _layer_spec(*lw['w_ple_proj_f32'])],
        out_specs=[_whole_out_spec(h.shape),
                   pl.BlockSpec((D_MODEL, FF_COLS), lambda j: (0, j)),
                   pl.BlockSpec((D_MODEL, FF_COLS), lambda j: (0, j)),
                   pl.BlockSpec((FF_COLS, D_MODEL), lambda j: (j, 0))],
        out_shape=[jax.ShapeDtypeStruct(h.shape, F32),
                   jax.ShapeDtypeStruct((D_MODEL, d_ff), BF16),
                   jax.ShapeDtypeStruct((D_MODEL, d_ff), BF16),
                   jax.ShapeDtypeStruct((d_ff, D_MODEL), BF16)],
        scratch_shapes=[pltpu.VMEM((n_dec, D_MODEL), F32), pltpu.VMEM((n_dec, D_MODEL), F32)],
        compiler_params=_tc_params(),
        name="ffn_row",
    )(h, *[lw[k][0] for k in ('p_row', 'g_ffn', 'w_ff_gate_f32', 'w_ff_up_f32',
                              'w_ff_down_f32', 'g_ple', 'w_ple_gate_f32', 'w_ple_proj_f32')])


def _router_kernel(h_ref, gffn_ref, wr_ref, mi_ref, mf_ref, cnt_ref, carry_ref, *, n_tok):
    g = pl.program_id(0)
    tm = h_ref.shape[0]

    @pl.when(g == 0)
    def _():
        carry_ref[...] = jnp.zeros(carry_ref.shape, F32)

    valid_row = (g * tm + lax.broadcasted_iota(I32, (tm, 1), 0)) < n_tok
    xn = _rms(jnp.where(valid_row, h_ref[...], 0.0), gffn_ref[...])
    logits = _dot_full(xn, wr_ref[...]).T[:N_EXPERTS]
    valid = (g * tm + lax.broadcasted_iota(I32, (1, tm), 1)) < n_tok
    e = lax.broadcasted_iota(I32, logits.shape, 0)
    m1 = jnp.max(logits, axis=0, keepdims=True)
    i1 = jnp.min(jnp.where(logits == m1, e, N_EXPERTS), axis=0, keepdims=True)
    rest = jnp.where(e == i1, -jnp.inf, logits)
    m2 = jnp.max(rest, axis=0, keepdims=True)
    i2 = jnp.min(jnp.where(rest == m2, e, N_EXPERTS), axis=0, keepdims=True)
    e2 = jnp.exp(m2 - m1)
    denom = 1.0 + e2
    w1 = 1.0 / denom
    w2 = e2 / denom

    oh1 = jnp.where((e == i1) & valid, 1.0, 0.0)
    oh2 = jnp.where((e == i2) & valid, 1.0, 0.0)
    member = oh1 + oh2
    r = lax.broadcasted_iota(I32, (tm, tm), 0)
    c = lax.broadcasted_iota(I32, (tm, tm), 1)
    earlier = jnp.where(r < c, 1.0, 0.0).astype(BF16)
    pos = _dot(member, earlier) + carry_ref[...]
    pos1 = jnp.sum(oh1 * pos, axis=0, keepdims=True).astype(I32)
    pos2 = jnp.sum(oh2 * pos, axis=0, keepdims=True).astype(I32)
    carry_ref[...] = carry_ref[...] + jnp.sum(member, axis=1, keepdims=True)
    cnt_ref[...] = carry_ref[...]

    mi_ref[...] = jnp.where(e == 0, i1, jnp.where(e == 1, i2,
                            jnp.where(e == 2, pos1, jnp.where(e == 3, pos2, 0))))
    gates = jnp.where(e == 0, w1, jnp.where(e == 1, w2, 0.0))
    lanes = wr_ref.shape[1]
    gates = jnp.concatenate([gates, jnp.zeros((lanes - N_EXPERTS, tm), F32)], axis=0)
    mf_ref[...] = gates.T[:, :N_EXPERTS]


def _router(h, tile0, n_tok, lw):
    tm = TOK_ROWS
    return pl.pallas_call(
        functools.partial(_router_kernel, n_tok=n_tok),
        grid=(pl.cdiv(n_tok, tm),),
        in_specs=[pl.BlockSpec((tm, D_MODEL), lambda g: (tile0 + g, 0)),
                  _layer_spec(*lw['g_ffn']), _layer_spec(*lw['w_router'])],
        out_specs=[pl.BlockSpec((N_EXPERTS, tm), lambda g: (0, g)),
                   pl.BlockSpec((tm, N_EXPERTS), lambda g: (g, 0)),
                   _whole_out_spec((N_EXPERTS, 1))],
        out_shape=[jax.ShapeDtypeStruct((N_EXPERTS, n_tok), I32),
                   jax.ShapeDtypeStruct((n_tok, N_EXPERTS), F32),
                   jax.ShapeDtypeStruct((N_EXPERTS, 1), F32)],
        scratch_shapes=[pltpu.VMEM((N_EXPERTS, 1), F32)],
        compiler_params=_tc_params(),
        name="router",
    )(h, lw['g_ffn'][0], lw['w_router'][0])


def _sc_chunk(n_rows):
    per_worker = pl.cdiv(n_rows, SC_WORKERS)
    return pl.cdiv(per_worker, SC_WINDOW) * SC_WINDOW


def _sc_worker_base(n_rows, chunk):
    wid = lax.axis_index("s") * SC_CORES + lax.axis_index("c")
    return jnp.minimum(wid * chunk, n_rows - chunk)


def _sc_scatter_rows(x, row0, dest, n_out):
    n = dest.shape[0] // TOP_K
    chunk = _sc_chunk(n)
    mesh = plsc.VectorSubcoreMesh(core_axis_name="c", subcore_axis_name="s")

    @functools.partial(
        pl.kernel, mesh=mesh,
        out_type=jax.ShapeDtypeStruct((n_out, D_MODEL), x.dtype),
        scratch_types=[pltpu.VMEM((SC_WINDOW,), I32) for _ in range(TOP_K)]
        + [pltpu.VMEM((SC_WINDOW, D_MODEL), x.dtype), pltpu.SemaphoreType.DMA],
        name="sc_scatter_rows",
    )
    def scatter(x_hbm, dest_hbm, out_hbm, idx0_v, idx1_v, rows_v, sem):
        base = _sc_worker_base(n, chunk)

        @pl.loop(0, chunk // SC_WINDOW)
        def _(j):
            off = pl.multiple_of(base + j * SC_WINDOW, 8)
            pltpu.sync_copy(dest_hbm.at[pl.ds(off, SC_WINDOW)], idx0_v)
            pltpu.sync_copy(dest_hbm.at[pl.ds(n + off, SC_WINDOW)], idx1_v)
            pltpu.sync_copy(x_hbm.at[pl.ds(row0 + off, SC_WINDOW)], rows_v)
            first = pltpu.async_copy(rows_v, out_hbm.at[idx0_v], sem)
            second = pltpu.async_copy(rows_v, out_hbm.at[idx1_v], sem)
            first.wait()
            second.wait()

    return scatter(x, dest)


def _sc_gather_rows(y, idx):
    n = idx.shape[0]
    chunk = _sc_chunk(n)
    mesh = plsc.VectorSubcoreMesh(core_axis_name="c", subcore_axis_name="s")

    @functools.partial(
        pl.kernel, mesh=mesh,
        out_type=jax.ShapeDtypeStruct((n, D_MODEL), y.dtype),
        scratch_types=[pltpu.VMEM((SC_WINDOW,), I32),
                       pltpu.VMEM((SC_WINDOW, D_MODEL), y.dtype), pltpu.SemaphoreType.DMA],
        name="sc_gather_rows",
    )
    def gather(y_hbm, idx_hbm, out_hbm, idx_v, rows_v, sem):
        base = _sc_worker_base(n, chunk)

        @pl.loop(0, chunk // SC_WINDOW)
        def _(j):
            off = pl.multiple_of(base + j * SC_WINDOW, 8)
            pltpu.sync_copy(idx_hbm.at[pl.ds(off, SC_WINDOW)], idx_v)
            pltpu.async_copy(y_hbm.at[idx_v], rows_v, sem).wait()
            pltpu.sync_copy(rows_v, out_hbm.at[pl.ds(off, SC_WINDOW)])

    return gather(y, idx)


def _expert_kernel(tile_expert_ref, n_valid_ref, xs_ref, gffn_ref, wg_ref, wu_ref, wd_ref,
                   y_ref):
    @pl.when(pl.program_id(0) < n_valid_ref[0])
    def _():
        xn = _rms(xs_ref[...], gffn_ref[...]).astype(BF16)
        a = jax.nn.silu(_dot(xn, wg_ref[...])) * _dot(xn, wu_ref[...])
        y_ref[...] = _dot(a, wd_ref[...])


def _experts(xs, tile_expert, n_valid, lw):
    n_slots = xs.shape[0]
    te = EXPERT_ROWS
    d_exp = lw['w_ex_gate'][0].shape[-1]
    moe = lw['w_ex_gate'][1]
    row_block = lambda g, tex, nv: (jnp.minimum(g, nv[0] - 1), 0)
    w_block = lambda g, tex, nv: (moe, tex[g], 0, 0)
    return pl.pallas_call(
        _expert_kernel,
        grid_spec=pltpu.PrefetchScalarGridSpec(
            num_scalar_prefetch=2,
            grid=(n_slots // te,),
            in_specs=[pl.BlockSpec((te, D_MODEL), row_block),
                      _layer_spec(*lw['g_ffn']),
                      pl.BlockSpec((None, None, D_MODEL, d_exp), w_block),
                      pl.BlockSpec((None, None, D_MODEL, d_exp), w_block),
                      pl.BlockSpec((None, None, d_exp, D_MODEL), w_block)],
            out_specs=pl.BlockSpec((te, D_MODEL), row_block)),
        out_shape=jax.ShapeDtypeStruct((n_slots, D_MODEL), F32),
        compiler_params=_tc_params(),
        name="experts",
    )(tile_expert, n_valid, xs, *[lw[k][0] for k in ('g_ffn', 'w_ex_gate', 'w_ex_up',
                                                        'w_ex_down')])


def _combine_math(h_ref, yg_ref, mf_ref, p, gple_ref, wpg_ref, wpp_ref, gfin_ref):
    gates = mf_ref[...]
    h = h_ref[...] + (gates[:, 0:1] * yg_ref[0] + gates[:, 1:2] * yg_ref[1])
    return _rms(_ple(h, p, gple_ref, wpg_ref, wpp_ref, _dot), gfin_ref[...])


def _combine_seq_kernel(h_ref, yg_ref, mf_ref, pp_ref, gple_ref, wpg_ref, wpp_ref, gfin_ref,
                        ybuf_ref, yp_ref):
    del ybuf_ref
    yp_ref[...] = _combine_math(h_ref, yg_ref, mf_ref, pp_ref[...], gple_ref, wpg_ref,
                                wpp_ref, gfin_ref)


def _combine_tail_kernel(h_ref, yg_ref, mf_ref, pp_ref, ps_ref, gple_ref, wpg_ref, wpp_ref,
                         gfin_ref, ybuf_ref, yp_ref, ys_ref, pbuf_ref):
    del ybuf_ref
    g = pl.program_id(0)
    last = pl.num_programs(0) - 1
    n_dec = ps_ref.shape[0]
    pbuf_ref[...] = pp_ref[...]

    @pl.when(g == last)
    def _():
        pbuf_ref[0:n_dec, :] = ps_ref[...]

    out = _combine_math(h_ref, yg_ref, mf_ref, pbuf_ref[...], gple_ref, wpg_ref, wpp_ref,
                        gfin_ref)

    @pl.when(g < last)
    def _():
        yp_ref[...] = out

    @pl.when(g == last)
    def _():
        ys_ref[...] = out[0:n_dec, :]


def _combine(h, tile0, yg, mf, y_seq, lw, with_rows):
    tm = TOK_ROWS
    n_tiles = pl.cdiv(yg.shape[1], tm)
    last_seq_tile = y_seq.shape[0] // tm - 1
    pp, layer = lw['p_seq']
    n_dec = lw['p_row'][0].shape[-2]
    weights = [lw[k] for k in (('p_row',) if with_rows else ())
               + ('g_ple', 'w_ple_gate', 'w_ple_proj', 'g_final')]
    seq_tile = lambda g: (jnp.minimum(tile0 + g, last_seq_tile), 0)
    n_in = 4 + len(weights)
    out = pl.pallas_call(
        _combine_tail_kernel if with_rows else _combine_seq_kernel,
        grid=(n_tiles,),
        in_specs=[pl.BlockSpec((tm, D_MODEL), lambda g: (tile0 + g, 0)),
                  pl.BlockSpec((TOP_K, tm, D_MODEL), lambda g: (0, g, 0)),
                  pl.BlockSpec((tm, N_EXPERTS), lambda g: (g, 0)),
                  pl.BlockSpec((None, tm, D_PLE), lambda g: (layer,) + seq_tile(g))]
        + [_layer_spec(*w) for w in weights]
        + [pl.BlockSpec(memory_space=pl.ANY)],
        out_specs=[pl.BlockSpec((tm, D_MODEL), seq_tile)]
        + ([_whole_out_spec((n_dec, D_MODEL))] if with_rows else []),
        out_shape=[jax.ShapeDtypeStruct(y_seq.shape, F32)]
        + ([jax.ShapeDtypeStruct((n_dec, D_MODEL), F32)] if with_rows else []),
        scratch_shapes=[pltpu.VMEM((tm, D_PLE), F32)] if with_rows else [],
        input_output_aliases={n_in: 0},
        compiler_params=_tc_params(),
        name="combine",
    )(h, yg, mf, pp, *[w for w, _ in weights], y_seq)
    return out if with_rows else (out[0], None)


def _moe_layer(h, y_seq, lw):
    n_tok = h.shape[0]
    tm = TOK_ROWS
    te = EXPERT_ROWS
    tiles = pl.cdiv(n_tok, tm)
    bounds = [tiles * c // MOE_CHUNKS for c in range(MOE_CHUNKS + 1)]
    y_rows = None
    for c in range(MOE_CHUNKS):
        tile0 = bounds[c]
        n = min(bounds[c + 1] * tm, n_tok) - tile0 * tm
        mi, mf, counts = _router(h, tile0, n, lw)

        cnt = counts[:, 0].astype(I32)
        padded = (cnt + te - 1) // te * te
        ends = jnp.cumsum(padded)
        starts = ends - padded
        experts = jnp.arange(N_EXPERTS, dtype=I32)
        start_of = lambda e: jnp.sum(
            jnp.where(e[None, :] == experts[:, None], starts[:, None], 0), axis=0)
        dest = jnp.concatenate([start_of(mi[0]) + mi[2], start_of(mi[1]) + mi[3]])
        dest, h = lax.optimization_barrier((dest, h))
        n_tiles = pl.cdiv(TOP_K * n + N_EXPERTS * (te - 1), te)
        tile_start = jnp.arange(n_tiles, dtype=I32) * te
        last_used = jnp.max(jnp.where(padded > 0, experts, 0))
        tile_expert = jnp.minimum(
            jnp.sum(tile_start[:, None] >= ends[None, :], axis=-1).astype(I32), last_used)
        n_valid = (ends[-1:] // te).astype(I32)

        xs = _sc_scatter_rows(h, tile0 * tm, dest, n_tiles * te)
        y = _experts(xs, tile_expert, n_valid, lw)
        yg = _sc_gather_rows(y, dest).reshape(TOP_K, n, D_MODEL)
        y_seq, rows = _combine(h, tile0, yg, mf, y_seq, lw, with_rows=c == MOE_CHUNKS - 1)
        y_rows = rows if rows is not None else y_rows
    return y_seq, y_rows


def kernel(x_prompt, x_sample, state_conv_a, state_conv_c, p_prompt, p_sample, g_mix, w_in, w_conv_a, w_s, b_s, g_ln_b, b_ln_b, w_conv_c, b_conv_c, g_ln_c, b_ln_c, g_out, w_o, g_ffn, w_ff_gate, w_ff_up, w_ff_down, w_router, w_ex_gate, w_ex_up, w_ex_down, g_ple, w_ple_gate, w_ple_proj, g_final):
    depth = g_mix.shape[0]
    n_seq, seq, _ = x_prompt.shape
    n_dec = x_sample.shape[0]
    n_prompt = n_seq * seq
    assert depth == 2 and x_sample.shape[1] == 1
    assert seq % TOK_ROWS == 0 and TOK_ROWS % n_dec == 0 and w_ff_gate.shape[-1] % FF_COLS == 0

    vec = lambda x: x.reshape(x.shape[0], 1, -1)
    per_layer = {
        'g_mix': vec(g_mix), 'w_in_f32': w_in, 'w_conv_a': w_conv_a,
        'w_s_cat': jnp.transpose(w_s, (0, 2, 1, 3)).reshape(depth, CHUNK, N_HEADS_B * CHUNK),
        'b_s_full': jnp.repeat(jnp.swapaxes(b_s, 1, 2), HEAD_DIM, axis=2),
        'g_ln_b': vec(g_ln_b), 'b_ln_b': vec(b_ln_b), 'w_conv_c': w_conv_c,
        'b_conv_c': vec(b_conv_c), 'g_ln_c': vec(g_ln_c), 'b_ln_c': vec(b_ln_c),
        'g_out': vec(g_out), 'w_o_f32': w_o, 'g_ffn': vec(g_ffn), 'g_ple': vec(g_ple),
        'w_ple_gate_f32': w_ple_gate, 'w_ple_proj_f32': w_ple_proj,
        'state_a': jnp.swapaxes(state_conv_a, 1, 2), 'state_c': jnp.swapaxes(state_conv_c, 1, 2),
        'p_seq': p_prompt.reshape(depth, n_prompt, D_PLE),
        'p_row': p_sample.reshape(depth, n_dec, D_PLE),
    }
    per_dense = {'w_ff_gate_f32': w_ff_gate, 'w_ff_up_f32': w_ff_up, 'w_ff_down_f32': w_ff_down}
    one_layer = lambda w: (w.reshape((1,) + w.shape), 0)
    per_moe = {'w_router': jnp.pad(w_router, ((0, 0), (0, 0), (0, LANES - N_EXPERTS)))}
    to_convert = [('w_ex_gate', w_ex_gate), ('w_ex_up', w_ex_up), ('w_ex_down', w_ex_down)]

    def with_ride(call):
        name, w = to_convert.pop(0)
        *outputs, w_bf16 = call(w.reshape(-1, w.shape[-1]))
        per_moe[name] = w_bf16.reshape(w.shape)
        return outputs

    hp = x_prompt.reshape(n_prompt, D_MODEL)
    hs = x_sample.reshape(n_dec, D_MODEL)
    outs = {k: [] for k in ('a_p', 'a_s', 'c_p', 'c_s', 'v_s')}
    for i in range(depth):
        is_expert_layer = i % 2 == 1
        lw = {k: (v, i) for k, v in per_layer.items()}
        lw.update({k: (v, i // 2) for k, v in (per_dense if not is_expert_layer else {}).items()})
        lw['g_final'] = (g_final.reshape(1, 1, -1), 0)

        hs, a_s, c_s, v_s, *converted = _mixer_row(hs, 0, lw)
        lw.update({k: one_layer(w) for k, w in zip(CONVERTED_BY_MIXER_ROW, converted)})
        mixer_in = hp
        hp, a_p, c_p = with_ride(functools.partial(
            _mixer_seq, hp, hs if is_expert_layer else None, lw, n_seq, seq))
        for k, v in zip(('a_p', 'c_p', 'a_s', 'c_s', 'v_s'), (a_p, c_p, a_s, c_s, v_s)):
            outs[k].append(v)
        if not is_expert_layer:
            hs, *converted = _ffn_row(hs, lw)
            lw.update({k: one_layer(w) for k, w in
                       zip(('w_ff_gate', 'w_ff_up', 'w_ff_down'), converted)})
            hp, = with_ride(functools.partial(_ffn_dense, hp, lw))
        else:
            lw.update({k: (v, i // 2) for k, v in per_moe.items()})
            y_prompt, y_sample = _moe_layer(hp, mixer_in, lw)

    return (y_prompt.reshape(x_prompt.shape), y_sample.reshape(x_sample.shape),
            jnp.stack(outs['a_p']), jnp.stack(outs['a_s']),
            jnp.stack(outs['c_p']), jnp.stack(outs['c_s']),
            jnp.stack(outs['v_s']).reshape(depth, n_dec, 1, D_B))
```

```python
import functools

import jax
import jax.numpy as jnp
from jax import lax
from jax.experimental import pallas as pl
from jax.experimental.pallas import tpu as pltpu
from jax.experimental.pallas import tpu_sc as plsc

D_MODEL = 1024
HEAD_DIM = 64
D_A = 384
D_B = 256
D_C = 384
N_HEADS_B = D_B // HEAD_DIM
CONV_A = 3
CONV_C = 31
CHUNK = 128
D_IN = 3 * D_A + 2 * D_B + 2 * D_C
D_PLE = 256
N_EXPERTS = 8
TOP_K = 2
EPS = 1e-6

O1, O2, O3 = D_A, 2 * D_A, 3 * D_A
O4 = O3 + 2 * D_B

F32 = jnp.float32
BF16 = jnp.bfloat16
I32 = jnp.int32

VMEM_LIMIT_BYTES = 56 * 1024 * 1024
SUBLANES = 8
LANES = 128

TOK_ROWS = 512
CONV_ROWS = 64
A_HALO = 8
C_HALO = 32
EXPERT_ROWS = 512
FF_COLS = 256
MOE_CHUNKS = 2

SC_CORES = 2
SC_WORKERS = 32
SC_WINDOW = 48


def _rms(x, g):
    return x * lax.rsqrt(jnp.mean(x * x, axis=-1, keepdims=True) + EPS) * g


def _ln(x, g, b):
    mu = jnp.mean(x, axis=-1, keepdims=True)
    xc = x - mu
    var = jnp.mean(xc * xc, axis=-1, keepdims=True)
    return xc * lax.rsqrt(var + EPS) * g + b


def _dot(a, b):
    return jnp.dot(a.astype(BF16), b, preferred_element_type=F32)


def _split(x):
    hi = x.astype(BF16)
    return hi, (x - hi.astype(F32)).astype(BF16)


def _dot_full(a, b):
    a_hi, a_lo = _split(a)
    b_hi, b_lo = _split(b)
    m = a.shape[0]
    by_hi = jnp.dot(jnp.concatenate([a_hi, a_lo], axis=0), b_hi, preferred_element_type=F32)
    return by_hi[:m] + by_hi[m:] + jnp.dot(a_hi, b_lo, preferred_element_type=F32)


def _const_spec(shape):
    return pl.BlockSpec(shape, lambda *_: (0,) * len(shape), pipeline_mode=pl.Buffered(1))


def _layer_spec(stacked, layer):
    idx = (layer,) if isinstance(layer, int) else tuple(layer)
    rest = stacked.shape[len(idx):]
    return pl.BlockSpec((None,) * len(idx) + rest, lambda *_: idx + (0,) * len(rest),
                        pipeline_mode=pl.Buffered(1))


def _whole_out_spec(shape):
    return pl.BlockSpec(shape, lambda *_: (0,) * len(shape))


def _tc_params():
    return pltpu.CompilerParams(dimension_semantics=("arbitrary",),
                                vmem_limit_bytes=VMEM_LIMIT_BYTES)


def _normed_groups(y_a, y_b, y_c, gout_ref):
    return jnp.concatenate([_rms(y_a, gout_ref[:, :D_A]),
                            _rms(y_b, gout_ref[:, D_A:D_A + D_B]),
                            _rms(y_c, gout_ref[:, D_A + D_B:])], axis=-1)


def _mixer_seq_kernel(hp_ref, hs_ref, gmix_ref, win_ref, wca_ref, wscat_ref, bsfull_ref,
                      glnb_ref, blnb_ref, wcc_ref, bcc_ref, glnc_ref, blnc_ref, gout_ref,
                      wo_ref, ride_ref, hout_ref, newa_ref, newc_ref, ride_out_ref,
                      qext_ref, gext_ref, gshift_ref, conv_ref, wcc8_ref, *, tiles_per_seq,
                      n_seq_tiles):
    g = pl.program_id(0)
    tm = hp_ref.shape[0]

    @pl.when(g == 0)
    def _():
        for k in range(CONV_C):
            wcc8_ref[k] = jnp.broadcast_to(wcc_ref[k:k + 1, :], (SUBLANES, D_C))

    @pl.when(g < n_seq_tiles)
    def _sequence_tile():
        ride_out_ref[...] = ride_ref[...].astype(BF16)

        @pl.when(g % tiles_per_seq == 0)
        def _():
            qext_ref[0:A_HALO, :] = jnp.zeros((A_HALO, D_A), F32)
            gext_ref[0:C_HALO, :] = jnp.zeros((C_HALO, D_C), F32)

        h = hp_ref[...]
        z = _dot(_rms(h, gmix_ref[...]), win_ref[...])

        qext_ref[A_HALO:A_HALO + tm, :] = z[:, O1:O2] * z[:, O2:O3]
        conv_a = jnp.zeros((tm, D_A), F32)
        for k in range(CONV_A):
            off = A_HALO - (CONV_A - 1) + k
            conv_a = conv_a + wca_ref[k:k + 1, :] * qext_ref[off:off + tm, :]
        y_a = z[:, :O1] * conv_a
        last_q = qext_ref[A_HALO + tm - (CONV_A - 1):A_HALO + tm, :]
        newa_ref[...] = last_q
        qext_ref[A_HALO - (CONV_A - 1):A_HALO, :] = last_q

        zb = jax.nn.gelu(z[:, O3:O4])
        v = _ln(zb[:, D_B:], glnb_ref[...], blnb_ref[...])
        row = lax.broadcasted_iota(I32, (CHUNK, N_HEADS_B * CHUNK), 0)
        col = lax.broadcasted_iota(I32, (CHUNK, N_HEADS_B * CHUNK), 1)
        w_tril = jnp.where((col % CHUNK) <= row, wscat_ref[...], 0.0).astype(BF16)
        lane_head = lax.broadcasted_iota(I32, (CHUNK, D_B), 1) // HEAD_DIM
        s_chunks = []
        for c in range(tm // CHUNK):
            vc = v[c * CHUNK:(c + 1) * CHUNK, :]
            vstack = jnp.concatenate(
                [jnp.where(lane_head == hd, vc, 0.0) for hd in range(N_HEADS_B)], axis=0)
            s_chunks.append(_dot(w_tril, vstack.astype(BF16)) + bsfull_ref[...])
        y_b = zb[:, :D_B] * jnp.concatenate(s_chunks, axis=0)

        gext_ref[C_HALO:C_HALO + tm, :] = (z[:, O4:O4 + D_C]
                                           * jax.nn.sigmoid(z[:, O4 + D_C:]))
        n_shift = gshift_ref.shape[1]
        for s in range(1, SUBLANES):
            gshift_ref[s - 1] = gext_ref[s:s + n_shift, :]
        base = C_HALO - (CONV_C - 1)
        for r0 in range(0, tm, CONV_ROWS):
            acc = jnp.zeros((CONV_ROWS // SUBLANES, SUBLANES, D_C), F32)
            for k in range(CONV_C):
                lo = (base + k) // SUBLANES * SUBLANES + r0
                s = (base + k) % SUBLANES
                window = (gext_ref[lo:lo + CONV_ROWS, :] if s == 0
                          else gshift_ref[s - 1, lo:lo + CONV_ROWS, :])
                acc = acc + wcc8_ref[k][None] * window.reshape(acc.shape)
            conv_ref[r0:r0 + CONV_ROWS, :] = acc.reshape(CONV_ROWS, D_C)
        y_c = jax.nn.silu(_ln(conv_ref[...] + bcc_ref[...], glnc_ref[...], blnc_ref[...]))
        last_g = gext_ref[C_HALO + tm - (CONV_C - 1):C_HALO + tm, :]
        newc_ref[...] = last_g
        gext_ref[C_HALO - (CONV_C - 1):C_HALO, :] = last_g

        hout_ref[...] = h + _dot(_normed_groups(y_a, y_b, y_c, gout_ref), wo_ref[...])

    @pl.when(g == n_seq_tiles)
    def _append_sample_rows():
        hout_ref[0:hs_ref.shape[0], :] = hs_ref[...]


def _ride_specs(ride, n_steps):
    rows = ride.shape[0] // n_steps
    assert rows * n_steps == ride.shape[0]
    slab = lambda g: (jnp.minimum(g, n_steps - 1), 0)
    return (pl.BlockSpec((rows, ride.shape[1]), slab), pl.BlockSpec((rows, ride.shape[1]), slab),
            jax.ShapeDtypeStruct(ride.shape, BF16))


def _mixer_seq(hp, hs, lw, n_seq, seq, ride):
    tm = TOK_ROWS
    tiles_per_seq = seq // tm
    n_seq_tiles = n_seq * tiles_per_seq
    join = hs is not None
    if not join:
        hs = jnp.zeros((SUBLANES, D_MODEL), F32)
    n_out = n_seq * seq + (hs.shape[0] if join else 0)
    weights = [lw[k] for k in ('g_mix', 'w_in', 'w_conv_a', 'w_s_cat', 'b_s_full', 'g_ln_b',
                               'b_ln_b', 'w_conv_c', 'b_conv_c', 'g_ln_c', 'b_ln_c', 'g_out',
                               'w_o')]
    seq_of = lambda g: jnp.minimum(g // tiles_per_seq, n_seq - 1)
    n_shift = tm + C_HALO - SUBLANES
    ride_in_spec, ride_out_spec, ride_out_shape = _ride_specs(ride, n_seq_tiles)
    return pl.pallas_call(
        functools.partial(_mixer_seq_kernel, tiles_per_seq=tiles_per_seq,
                          n_seq_tiles=n_seq_tiles),
        grid=(n_seq_tiles + int(join),),
        in_specs=[pl.BlockSpec((tm, D_MODEL), lambda g: (jnp.minimum(g, n_seq_tiles - 1), 0)),
                  _const_spec(hs.shape)]
        + [_layer_spec(*w) for w in weights] + [ride_in_spec],
        out_specs=[pl.BlockSpec((tm, D_MODEL), lambda g: (g, 0)),
                   pl.BlockSpec((None, CONV_A - 1, D_A), lambda g: (seq_of(g), 0, 0)),
                   pl.BlockSpec((None, CONV_C - 1, D_C), lambda g: (seq_of(g), 0, 0)),
                   ride_out_spec],
        out_shape=[jax.ShapeDtypeStruct((n_out, D_MODEL), F32),
                   jax.ShapeDtypeStruct((n_seq, CONV_A - 1, D_A), F32),
                   jax.ShapeDtypeStruct((n_seq, CONV_C - 1, D_C), F32),
                   ride_out_shape],
        scratch_shapes=[pltpu.VMEM((A_HALO + tm, D_A), F32),
                        pltpu.VMEM((C_HALO + tm, D_C), F32),
                        pltpu.VMEM((SUBLANES - 1, n_shift, D_C), F32),
                        pltpu.VMEM((tm, D_C), F32),
                        pltpu.VMEM((CONV_C, SUBLANES, D_C), F32)],
        compiler_params=_tc_params(),
        name="mixer_seq",
    )(hp, hs, *[w for w, _ in weights], ride)


def _mixer_row_kernel(*refs, n_carried):
    (h_ref, sa_ref, sc_ref, gmix_ref, win_ref, wca_ref, wscat_ref, bsfull_ref, glnb_ref,
     blnb_ref, wcc_ref, bcc_ref, glnc_ref, blnc_ref, gout_ref, wo_ref, wpg_ref,
     wpp_ref) = refs[:18]
    (hout_ref, newa_ref, newc_ref, v_ref, win_bf_ref, wo_bf_ref, wpg_bf_ref,
     wpp_bf_ref) = refs[18 + n_carried:]

    @pl.when(pl.program_id(0) > 0)
    def _():
        for ref in (newa_ref, newc_ref, v_ref):
            ref[...] = jnp.zeros(ref.shape, F32)

    @pl.when(pl.program_id(0) == 0)
    def _():
        _mixer_row_body(h_ref, sa_ref, sc_ref, gmix_ref, win_ref, wca_ref, wscat_ref,
                        bsfull_ref, glnb_ref, blnb_ref, wcc_ref, bcc_ref, glnc_ref, blnc_ref,
                        gout_ref, wo_ref, wpg_ref, wpp_ref, hout_ref, newa_ref, newc_ref,
                        v_ref, win_bf_ref, wo_bf_ref, wpg_bf_ref, wpp_bf_ref)


def _mixer_row_body(h_ref, sa_ref, sc_ref, gmix_ref, win_ref, wca_ref, wscat_ref,
                    bsfull_ref, glnb_ref, blnb_ref, wcc_ref, bcc_ref, glnc_ref, blnc_ref,
                    gout_ref, wo_ref, wpg_ref, wpp_ref, hout_ref, newa_ref, newc_ref, v_ref,
                    win_bf_ref, wo_bf_ref, wpg_bf_ref, wpp_bf_ref):
    for src, dst in ((win_ref, win_bf_ref), (wo_ref, wo_bf_ref), (wpg_ref, wpg_bf_ref),
                     (wpp_ref, wpp_bf_ref)):
        dst[...] = src[...].astype(BF16)

    h = h_ref[...]
    z = _dot_full(_rms(h, gmix_ref[...]), win_ref[...])

    q = z[:, O1:O2] * z[:, O2:O3]
    conv_a = wca_ref[CONV_A - 1:CONV_A, :] * q
    for k in range(CONV_A - 1):
        conv_a = conv_a + wca_ref[k:k + 1, :] * sa_ref[k]
    y_a = z[:, :O1] * conv_a
    for k in range(CONV_A - 2):
        newa_ref[:, k, :] = sa_ref[k + 1]
    newa_ref[:, CONV_A - 2, :] = q

    zb = jax.nn.gelu(z[:, O3:O4])
    v = _ln(zb[:, D_B:], glnb_ref[...], blnb_ref[...])
    v_ref[...] = v
    w_diag0 = jnp.concatenate(
        [jnp.broadcast_to(wscat_ref[0:1, hd * CHUNK:hd * CHUNK + 1], (1, HEAD_DIM))
         for hd in range(N_HEADS_B)], axis=-1)
    y_b = zb[:, :D_B] * (w_diag0 * v + bsfull_ref[0:1, :])

    glu = z[:, O4:O4 + D_C] * jax.nn.sigmoid(z[:, O4 + D_C:])
    conv_c = wcc_ref[CONV_C - 1:CONV_C, :] * glu
    for k in range(CONV_C - 1):
        conv_c = conv_c + wcc_ref[k:k + 1, :] * sc_ref[k]
    y_c = jax.nn.silu(_ln(conv_c + bcc_ref[...], glnc_ref[...], blnc_ref[...]))
    for k in range(CONV_C - 2):
        newc_ref[:, k, :] = sc_ref[k + 1]
    newc_ref[:, CONV_C - 2, :] = glu

    hout_ref[...] = h + _dot_full(_normed_groups(y_a, y_b, y_c, gout_ref), wo_ref[...])


CONVERTED_BY_MIXER_ROW = ('w_in', 'w_o', 'w_ple_gate', 'w_ple_proj')


def _mixer_row(h, h_block, lw, states):
    depth, _, n_dec, _ = lw['state_a'][0].shape
    layer = lw['state_a'][1]
    weights = [lw[k] for k in ('state_a', 'state_c', 'g_mix', 'w_in_f32', 'w_conv_a', 'w_s_cat',
                               'b_s_full', 'g_ln_b', 'b_ln_b', 'w_conv_c', 'b_conv_c', 'g_ln_c',
                               'b_ln_c', 'g_out', 'w_o_f32', 'w_ple_gate_f32', 'w_ple_proj_f32')]
    bf16_shapes = [lw[k + '_f32'][0].shape[1:] for k in CONVERTED_BY_MIXER_ROW]
    state_shapes = [(n_dec, CONV_A - 1, D_A), (n_dec, CONV_C - 1, D_C), (n_dec, D_B)]
    carried = list(states or ())
    first = 1 + len(weights)
    block_of = (lambda g: g) if states is None else (lambda g: layer)
    return pl.pallas_call(
        functools.partial(_mixer_row_kernel, n_carried=len(carried)),
        grid=(depth if states is None else 1,),
        in_specs=[pl.BlockSpec((n_dec, D_MODEL), lambda g: (h_block, 0))]
        + [_layer_spec(*w) for w in weights]
        + [pl.BlockSpec(memory_space=pl.ANY) for _ in carried],
        out_specs=[_whole_out_spec((n_dec, D_MODEL))]
        + [pl.BlockSpec((None,) + s, lambda g, s=s: (block_of(g),) + (0,) * len(s))
           for s in state_shapes]
        + [_whole_out_spec(s) for s in bf16_shapes],
        out_shape=[jax.ShapeDtypeStruct((n_dec, D_MODEL), F32)]
        + [jax.ShapeDtypeStruct((depth,) + s, F32) for s in state_shapes]
        + [jax.ShapeDtypeStruct(s, BF16) for s in bf16_shapes],
        input_output_aliases={first + j: 1 + j for j in range(len(carried))},
        compiler_params=_tc_params(),
        name="mixer_row",
    )(h, *[w for w, _ in weights], *carried)


def _ple(h, p, gple_ref, wpg_ref, wpp_ref, dot):
    gate = jax.nn.sigmoid(dot(_rms(h, gple_ref[...]), wpg_ref[...]))
    return h + gate * dot(p, wpp_ref[...])


def _ffn_dense_kernel(h_ref, p_ref, gffn_ref, wg_ref, wu_ref, wd_ref, gple_ref, wpg_ref,
                      wpp_ref, ride_ref, out_ref, ride_out_ref):
    ride_out_ref[...] = ride_ref[...].astype(BF16)
    h = h_ref[...]
    xn = _rms(h, gffn_ref[...]).astype(BF16)
    a = jax.nn.silu(_dot(xn, wg_ref[...])) * _dot(xn, wu_ref[...])
    h = h + _dot(a, wd_ref[...])
    out_ref[...] = _ple(h, p_ref[...], gple_ref, wpg_ref, wpp_ref, _dot)


def _ffn_dense(h, lw, ride):
    n_tok = h.shape[0]
    tm = TOK_ROWS
    p, layer = lw['p_seq']
    weights = [lw[k] for k in ('g_ffn', 'w_ff_gate', 'w_ff_up', 'w_ff_down', 'g_ple',
                               'w_ple_gate', 'w_ple_proj')]
    ride_in_spec, ride_out_spec, ride_out_shape = _ride_specs(ride, n_tok // tm)
    return pl.pallas_call(
        _ffn_dense_kernel,
        grid=(n_tok // tm,),
        in_specs=[pl.BlockSpec((tm, D_MODEL), lambda g: (g, 0)),
                  pl.BlockSpec((None, tm, D_PLE), lambda g: (layer, g, 0))]
        + [_layer_spec(*w) for w in weights] + [ride_in_spec],
        out_specs=[pl.BlockSpec((tm, D_MODEL), lambda g: (g, 0)), ride_out_spec],
        out_shape=[jax.ShapeDtypeStruct(h.shape, F32), ride_out_shape],
        compiler_params=_tc_params(),
        name="ffn_dense",
    )(h, p, *[w for w, _ in weights], ride)


def _ffn_row_kernel(h_ref, p_ref, gffn_ref, wg_ref, wu_ref, wd_ref, gple_ref, wpg_ref,
                    wpp_ref, out_ref, wg_bf_ref, wu_bf_ref, wd_bf_ref, xn_ref, acc_ref):
    j = pl.program_id(0)

    @pl.when(j == 0)
    def _():
        h = h_ref[...]
        xn_ref[...] = _rms(h, gffn_ref[...])
        acc_ref[...] = h

    for src, dst in ((wg_ref, wg_bf_ref), (wu_ref, wu_bf_ref), (wd_ref, wd_bf_ref)):
        dst[...] = src[...].astype(BF16)

    xn = xn_ref[...]
    a = jax.nn.silu(_dot_full(xn, wg_ref[...])) * _dot_full(xn, wu_ref[...])
    acc_ref[...] += _dot_full(a, wd_ref[...])

    @pl.when(j == pl.num_programs(0) - 1)
    def _():
        out_ref[...] = _ple(acc_ref[...], p_ref[...], gple_ref, wpg_ref, wpp_ref, _dot_full)


def _ffn_row(h, lw):
    n_dec = h.shape[0]
    d_ff = lw['w_ff_gate_f32'][0].shape[-1]
    ff = lw['w_ff_gate_f32'][1]
    return pl.pallas_call(
        _ffn_row_kernel,
        grid=(d_ff // FF_COLS,),
        in_specs=[_const_spec(h.shape), _layer_spec(*lw['p_row']), _layer_spec(*lw['g_ffn']),
                  pl.BlockSpec((None, D_MODEL, FF_COLS), lambda j: (ff, 0, j)),
                  pl.BlockSpec((None, D_MODEL, FF_COLS), lambda j: (ff, 0, j)),
                  pl.BlockSpec((None, FF_COLS, D_MODEL), lambda j: (ff, j, 0)),
                  _layer_spec(*lw['g_ple']), _layer_spec(*lw['w_ple_gate_f32']),
                  _layer_spec(*lw['w_ple_proj_f32'])],
        out_specs=[_whole_out_spec(h.shape),
                   pl.BlockSpec((D_MODEL, FF_COLS), lambda j: (0, j)),
                   pl.BlockSpec((D_MODEL, FF_COLS), lambda j: (0, j)),
                   pl.BlockSpec((FF_COLS, D_MODEL), lambda j: (j, 0))],
        out_shape=[jax.ShapeDtypeStruct(h.shape, F32),
                   jax.ShapeDtypeStruct((D_MODEL, d_ff), BF16),
                   jax.ShapeDtypeStruct((D_MODEL, d_ff), BF16),
                   jax.ShapeDtypeStruct((d_ff, D_MODEL), BF16)],
        scratch_shapes=[pltpu.VMEM((n_dec, D_MODEL), F32), pltpu.VMEM((n_dec, D_MODEL), F32)],
        compiler_params=_tc_params(),
        name="ffn_row",
    )(h, *[lw[k][0] for k in ('p_row', 'g_ffn', 'w_ff_gate_f32', 'w_ff_up_f32',
                              'w_ff_down_f32', 'g_ple', 'w_ple_gate_f32', 'w_ple_proj_f32')])


def _router_kernel(h_ref, gffn_ref, wr_ref, mi_ref, mf_ref, cnt_ref, carry_ref, *, n_tok):
    g = pl.program_id(0)
    tm = h_ref.shape[0]

    @pl.when(g == 0)
    def _():
        carry_ref[...] = jnp.zeros(carry_ref.shape, F32)

    valid_row = (g * tm + lax.broadcasted_iota(I32, (tm, 1), 0)) < n_tok
    xn = _rms(jnp.where(valid_row, h_ref[...], 0.0), gffn_ref[...])
    logits = _dot_full(xn, wr_ref[...]).T[:N_EXPERTS]
    valid = (g * tm + lax.broadcasted_iota(I32, (1, tm), 1)) < n_tok
    e = lax.broadcasted_iota(I32, logits.shape, 0)
    m1 = jnp.max(logits, axis=0, keepdims=True)
    i1 = jnp.min(jnp.where(logits == m1, e, N_EXPERTS), axis=0, keepdims=True)
    rest = jnp.where(e == i1, -jnp.inf, logits)
    m2 = jnp.max(rest, axis=0, keepdims=True)
    i2 = jnp.min(jnp.where(rest == m2, e, N_EXPERTS), axis=0, keepdims=True)
    e2 = jnp.exp(m2 - m1)
    denom = 1.0 + e2
    w1 = 1.0 / denom
    w2 = e2 / denom

    oh1 = jnp.where((e == i1) & valid, 1.0, 0.0)
    oh2 = jnp.where((e == i2) & valid, 1.0, 0.0)
    member = oh1 + oh2
    r = lax.broadcasted_iota(I32, (tm, tm), 0)
    c = lax.broadcasted_iota(I32, (tm, tm), 1)
    earlier = jnp.where(r < c, 1.0, 0.0).astype(BF16)
    pos = _dot(member, earlier) + carry_ref[...]
    pos1 = jnp.sum(oh1 * pos, axis=0, keepdims=True).astype(I32)
    pos2 = jnp.sum(oh2 * pos, axis=0, keepdims=True).astype(I32)
    carry_ref[...] = carry_ref[...] + jnp.sum(member, axis=1, keepdims=True)
    cnt_ref[...] = carry_ref[...]

    mi_ref[...] = jnp.where(e == 0, i1, jnp.where(e == 1, i2,
                            jnp.where(e == 2, pos1, jnp.where(e == 3, pos2, 0))))
    gates = jnp.where(e == 0, w1, jnp.where(e == 1, w2, 0.0))
    lanes = wr_ref.shape[1]
    gates = jnp.concatenate([gates, jnp.zeros((lanes - N_EXPERTS, tm), F32)], axis=0)
    mf_ref[...] = gates.T[:, :N_EXPERTS]


def _router(h, tile0, n_tok, lw):
    tm = TOK_ROWS
    return pl.pallas_call(
        functools.partial(_router_kernel, n_tok=n_tok),
        grid=(pl.cdiv(n_tok, tm),),
        in_specs=[pl.BlockSpec((tm, D_MODEL), lambda g: (tile0 + g, 0)),
                  _layer_spec(*lw['g_ffn']), _layer_spec(*lw['w_router'])],
        out_specs=[pl.BlockSpec((N_EXPERTS, tm), lambda g: (0, g)),
                   pl.BlockSpec((tm, N_EXPERTS), lambda g: (g, 0)),
                   _whole_out_spec((N_EXPERTS, 1))],
        out_shape=[jax.ShapeDtypeStruct((N_EXPERTS, n_tok), I32),
                   jax.ShapeDtypeStruct((n_tok, N_EXPERTS), F32),
                   jax.ShapeDtypeStruct((N_EXPERTS, 1), F32)],
        scratch_shapes=[pltpu.VMEM((N_EXPERTS, 1), F32)],
        compiler_params=_tc_params(),
        name="router",
    )(h, lw['g_ffn'][0], lw['w_router'][0])


def _sc_chunk(n_rows):
    per_worker = pl.cdiv(n_rows, SC_WORKERS)
    return pl.cdiv(per_worker, SC_WINDOW) * SC_WINDOW


def _sc_worker_base(n_rows, chunk):
    wid = lax.axis_index("s") * SC_CORES + lax.axis_index("c")
    return jnp.minimum(wid * chunk, n_rows - chunk)


def _sc_scatter_rows(x, row0, dest, n_out):
    n = dest.shape[0] // TOP_K
    chunk = _sc_chunk(n)
    mesh = plsc.VectorSubcoreMesh(core_axis_name="c", subcore_axis_name="s")

    @functools.partial(
        pl.kernel, mesh=mesh,
        out_type=jax.ShapeDtypeStruct((n_out, D_MODEL), x.dtype),
        scratch_types=[pltpu.VMEM((SC_WINDOW,), I32) for _ in range(TOP_K)]
        + [pltpu.VMEM((SC_WINDOW, D_MODEL), x.dtype), pltpu.SemaphoreType.DMA],
        name="sc_scatter_rows",
    )
    def scatter(x_hbm, dest_hbm, out_hbm, idx0_v, idx1_v, rows_v, sem):
        base = _sc_worker_base(n, chunk)

        @pl.loop(0, chunk // SC_WINDOW)
        def _(j):
            off = pl.multiple_of(base + j * SC_WINDOW, 8)
            pltpu.sync_copy(dest_hbm.at[pl.ds(off, SC_WINDOW)], idx0_v)
            pltpu.sync_copy(dest_hbm.at[pl.ds(n + off, SC_WINDOW)], idx1_v)
            pltpu.sync_copy(x_hbm.at[pl.ds(row0 + off, SC_WINDOW)], rows_v)
            first = pltpu.async_copy(rows_v, out_hbm.at[idx0_v], sem)
            second = pltpu.async_copy(rows_v, out_hbm.at[idx1_v], sem)
            first.wait()
            second.wait()

    return scatter(x, dest)


def _sc_gather_rows(y, idx):
    n = idx.shape[0]
    width = y.shape[1]
    chunk = _sc_chunk(n)
    mesh = plsc.VectorSubcoreMesh(core_axis_name="c", subcore_axis_name="s")

    @functools.partial(
        pl.kernel, mesh=mesh,
        out_type=jax.ShapeDtypeStruct((n, width), y.dtype),
        scratch_types=[pltpu.VMEM((SC_WINDOW,), I32),
                       pltpu.VMEM((SC_WINDOW, width), y.dtype), pltpu.SemaphoreType.DMA],
        name="sc_gather_rows",
    )
    def gather(y_hbm, idx_hbm, out_hbm, idx_v, rows_v, sem):
        base = _sc_worker_base(n, chunk)

        @pl.loop(0, chunk // SC_WINDOW)
        def _(j):
            off = pl.multiple_of(base + j * SC_WINDOW, 8)
            pltpu.sync_copy(idx_hbm.at[pl.ds(off, SC_WINDOW)], idx_v)
            pltpu.async_copy(y_hbm.at[idx_v], rows_v, sem).wait()
            pltpu.sync_copy(rows_v, out_hbm.at[pl.ds(off, SC_WINDOW)])

    return gather(y, idx)


def _pack_bf16_halves(y):
    half = y.shape[1] // 2
    rounded = y.astype(BF16).astype(F32)
    hi = lax.bitcast_convert_type(rounded[:, :half], jnp.uint32)
    lo = lax.bitcast_convert_type(rounded[:, half:], jnp.uint32)
    return hi | (lo >> 16)


def _unpack_bf16_halves(packed):
    hi = lax.bitcast_convert_type(packed & jnp.uint32(0xFFFF0000), F32)
    lo = lax.bitcast_convert_type(packed << 16, F32)
    return jnp.concatenate([hi, lo], axis=1)


def _expert_kernel(tile_expert_ref, n_valid_ref, xs_ref, gffn_ref, wg_ref, wu_ref, wd_ref,
                   y_ref):
    @pl.when(pl.program_id(0) < n_valid_ref[0])
    def _():
        xn = _rms(xs_ref[...], gffn_ref[...]).astype(BF16)
        a = jax.nn.silu(_dot(xn, wg_ref[...])) * _dot(xn, wu_ref[...])
        y_ref[...] = _pack_bf16_halves(_dot(a, wd_ref[...]))


def _experts(xs, tile_expert, n_valid, lw):
    n_slots = xs.shape[0]
    te = EXPERT_ROWS
    d_exp = lw['w_ex_gate'][0].shape[-1]
    moe = lw['w_ex_gate'][1]
    row_block = lambda g, tex, nv: (jnp.minimum(g, nv[0] - 1), 0)
    w_block = lambda g, tex, nv: (moe, tex[g], 0, 0)
    return pl.pallas_call(
        _expert_kernel,
        grid_spec=pltpu.PrefetchScalarGridSpec(
            num_scalar_prefetch=2,
            grid=(n_slots // te,),
            in_specs=[pl.BlockSpec((te, D_MODEL), row_block),
                      _layer_spec(*lw['g_ffn']),
                      pl.BlockSpec((None, None, D_MODEL, d_exp), w_block),
                      pl.BlockSpec((None, None, D_MODEL, d_exp), w_block),
                      pl.BlockSpec((None, None, d_exp, D_MODEL), w_block)],
            out_specs=pl.BlockSpec((te, D_MODEL // 2), row_block)),
        out_shape=jax.ShapeDtypeStruct((n_slots, D_MODEL // 2), jnp.uint32),
        compiler_params=_tc_params(),
        name="experts",
    )(tile_expert, n_valid, xs, *[lw[k][0] for k in ('g_ffn', 'w_ex_gate', 'w_ex_up',
                                                        'w_ex_down')])


def _combine_math(h_ref, yg_ref, mf_ref, p, gple_ref, wpg_ref, wpp_ref, gfin_ref):
    gates = mf_ref[...]
    h = h_ref[...] + (gates[:, 0:1] * _unpack_bf16_halves(yg_ref[0])
                      + gates[:, 1:2] * _unpack_bf16_halves(yg_ref[1]))
    return _rms(_ple(h, p, gple_ref, wpg_ref, wpp_ref, _dot), gfin_ref[...])


def _combine_seq_kernel(h_ref, yg_ref, mf_ref, pp_ref, gple_ref, wpg_ref, wpp_ref, gfin_ref,
                        ybuf_ref, yp_ref):
    del ybuf_ref
    yp_ref[...] = _combine_math(h_ref, yg_ref, mf_ref, pp_ref[...], gple_ref, wpg_ref,
                                wpp_ref, gfin_ref)


def _combine_tail_kernel(h_ref, yg_ref, mf_ref, pp_ref, ps_ref, gple_ref, wpg_ref, wpp_ref,
                         gfin_ref, ybuf_ref, yp_ref, ys_ref, pbuf_ref):
    del ybuf_ref
    g = pl.program_id(0)
    last = pl.num_programs(0) - 1
    n_dec = ps_ref.shape[0]
    pbuf_ref[...] = pp_ref[...]

    @pl.when(g == last)
    def _():
        pbuf_ref[0:n_dec, :] = ps_ref[...]

    out = _combine_math(h_ref, yg_ref, mf_ref, pbuf_ref[...], gple_ref, wpg_ref, wpp_ref,
                        gfin_ref)

    @pl.when(g < last)
    def _():
        yp_ref[...] = out

    @pl.when(g == last)
    def _():
        ys_ref[...] = out[0:n_dec, :]


def _combine(h, tile0, yg, mf, y_seq, lw, with_rows):
    tm = TOK_ROWS
    n_tiles = pl.cdiv(yg.shape[1], tm)
    last_seq_tile = y_seq.shape[0] // tm - 1
    pp, layer = lw['p_seq']
    n_dec = lw['p_row'][0].shape[-2]
    weights = [lw[k] for k in (('p_row',) if with_rows else ())
               + ('g_ple', 'w_ple_gate', 'w_ple_proj', 'g_final')]
    seq_tile = lambda g: (jnp.minimum(tile0 + g, last_seq_tile), 0)
    n_in = 4 + len(weights)
    out = pl.pallas_call(
        _combine_tail_kernel if with_rows else _combine_seq_kernel,
        grid=(n_tiles,),
        in_specs=[pl.BlockSpec((tm, D_MODEL), lambda g: (tile0 + g, 0)),
                  pl.BlockSpec((TOP_K, tm, yg.shape[2]), lambda g: (0, g, 0)),
                  pl.BlockSpec((tm, N_EXPERTS), lambda g: (g, 0)),
                  pl.BlockSpec((None, tm, D_PLE), lambda g: (layer,) + seq_tile(g))]
        + [_layer_spec(*w) for w in weights]
        + [pl.BlockSpec(memory_space=pl.ANY)],
        out_specs=[pl.BlockSpec((tm, D_MODEL), seq_tile)]
        + ([_whole_out_spec((n_dec, D_MODEL))] if with_rows else []),
        out_shape=[jax.ShapeDtypeStruct(y_seq.shape, F32)]
        + ([jax.ShapeDtypeStruct((n_dec, D_MODEL), F32)] if with_rows else []),
        scratch_shapes=[pltpu.VMEM((tm, D_PLE), F32)] if with_rows else [],
        input_output_aliases={n_in: 0},
        compiler_params=_tc_params(),
        name="combine",
    )(h, yg, mf, pp, *[w for w, _ in weights], y_seq)
    return out if with_rows else (out[0], None)


def _moe_layer(h, y_seq, lw):
    n_tok = h.shape[0]
    tm = TOK_ROWS
    te = EXPERT_ROWS
    tiles = pl.cdiv(n_tok, tm)
    bounds = [tiles * c // MOE_CHUNKS for c in range(MOE_CHUNKS + 1)]
    y_rows = None
    for c in range(MOE_CHUNKS):
        tile0 = bounds[c]
        n = min(bounds[c + 1] * tm, n_tok) - tile0 * tm
        mi, mf, counts = _router(h, tile0, n, lw)

        cnt = counts[:, 0].astype(I32)
        padded = (cnt + te - 1) // te * te
        ends = jnp.cumsum(padded)
        starts = ends - padded
        experts = jnp.arange(N_EXPERTS, dtype=I32)
        start_of = lambda e: jnp.sum(
            jnp.where(e[None, :] == experts[:, None], starts[:, None], 0), axis=0)
        dest = jnp.concatenate([start_of(mi[0]) + mi[2], start_of(mi[1]) + mi[3]])
        dest, h = lax.optimization_barrier((dest, h))
        n_tiles = pl.cdiv(TOP_K * n + N_EXPERTS * (te - 1), te)
        tile_start = jnp.arange(n_tiles, dtype=I32) * te
        last_used = jnp.max(jnp.where(padded > 0, experts, 0))
        tile_expert = jnp.minimum(
            jnp.sum(tile_start[:, None] >= ends[None, :], axis=-1).astype(I32), last_used)
        n_valid = (ends[-1:] // te).astype(I32)

        xs = _sc_scatter_rows(h, tile0 * tm, dest, n_tiles * te)
        y = _experts(xs, tile_expert, n_valid, lw)
        yg = _sc_gather_rows(y, dest).reshape(TOP_K, n, y.shape[1])
        y_seq, rows = _combine(h, tile0, yg, mf, y_seq, lw, with_rows=c == MOE_CHUNKS - 1)
        y_rows = rows if rows is not None else y_rows
    return y_seq, y_rows


def kernel(x_prompt, x_sample, state_conv_a, state_conv_c, p_prompt, p_sample, g_mix, w_in, w_conv_a, w_s, b_s, g_ln_b, b_ln_b, w_conv_c, b_conv_c, g_ln_c, b_ln_c, g_out, w_o, g_ffn, w_ff_gate, w_ff_up, w_ff_down, w_router, w_ex_gate, w_ex_up, w_ex_down, g_ple, w_ple_gate, w_ple_proj, g_final):
    depth = g_mix.shape[0]
    n_seq, seq, _ = x_prompt.shape
    n_dec = x_sample.shape[0]
    n_prompt = n_seq * seq
    assert depth == 2 and x_sample.shape[1] == 1
    assert seq % TOK_ROWS == 0 and TOK_ROWS % n_dec == 0 and w_ff_gate.shape[-1] % FF_COLS == 0

    vec = lambda x: x.reshape(x.shape[0], 1, -1)
    per_layer = {
        'g_mix': vec(g_mix), 'w_in_f32': w_in, 'w_conv_a': w_conv_a,
        'w_s_cat': jnp.transpose(w_s, (0, 2, 1, 3)).reshape(depth, CHUNK, N_HEADS_B * CHUNK),
        'b_s_full': jnp.repeat(jnp.swapaxes(b_s, 1, 2), HEAD_DIM, axis=2),
        'g_ln_b': vec(g_ln_b), 'b_ln_b': vec(b_ln_b), 'w_conv_c': w_conv_c,
        'b_conv_c': vec(b_conv_c), 'g_ln_c': vec(g_ln_c), 'b_ln_c': vec(b_ln_c),
        'g_out': vec(g_out), 'w_o_f32': w_o, 'g_ffn': vec(g_ffn), 'g_ple': vec(g_ple),
        'w_ple_gate_f32': w_ple_gate, 'w_ple_proj_f32': w_ple_proj,
        'state_a': jnp.swapaxes(state_conv_a, 1, 2), 'state_c': jnp.swapaxes(state_conv_c, 1, 2),
        'p_seq': p_prompt.reshape(depth, n_prompt, D_PLE),
        'p_row': p_sample.reshape(depth, n_dec, D_PLE),
    }
    per_dense = {'w_ff_gate_f32': w_ff_gate, 'w_ff_up_f32': w_ff_up, 'w_ff_down_f32': w_ff_down}
    one_layer = lambda w: (w.reshape((1,) + w.shape), 0)
    per_moe = {'w_router': jnp.pad(w_router, ((0, 0), (0, 0), (0, LANES - N_EXPERTS)))}
    to_convert = [('w_ex_gate', w_ex_gate), ('w_ex_up', w_ex_up), ('w_ex_down', w_ex_down)]

    def with_ride(call):
        name, w = to_convert.pop(0)
        *outputs, w_bf16 = call(w.reshape(-1, w.shape[-1]))
        per_moe[name] = w_bf16.reshape(w.shape)
        return outputs

    hp = x_prompt.reshape(n_prompt, D_MODEL)
    hs = x_sample.reshape(n_dec, D_MODEL)
    outs = {'a_p': [], 'c_p': []}
    row_states = None
    for i in range(depth):
        is_expert_layer = i % 2 == 1
        lw = {k: (v, i) for k, v in per_layer.items()}
        lw.update({k: (v, i // 2) for k, v in (per_dense if not is_expert_layer else {}).items()})
        lw['g_final'] = (g_final.reshape(1, 1, -1), 0)

        hs, *row_states, w0, w1, w2, w3 = _mixer_row(hs, 0, lw, row_states)
        lw.update({k: one_layer(w) for k, w in zip(CONVERTED_BY_MIXER_ROW, (w0, w1, w2, w3))})
        mixer_in = hp
        hp, a_p, c_p = with_ride(functools.partial(
            _mixer_seq, hp, hs if is_expert_layer else None, lw, n_seq, seq))
        outs['a_p'].append(a_p)
        outs['c_p'].append(c_p)
        if not is_expert_layer:
            hs, *converted = _ffn_row(hs, lw)
            lw.update({k: one_layer(w) for k, w in
                       zip(('w_ff_gate', 'w_ff_up', 'w_ff_down'), converted)})
            hp, = with_ride(functools.partial(_ffn_dense, hp, lw))
        else:
            lw.update({k: (v, i // 2) for k, v in per_moe.items()})
            y_prompt, y_sample = _moe_layer(hp, mixer_in, lw)

    a_s, c_s, v_s = row_states
    return (y_prompt.reshape(x_prompt.shape), y_sample.reshape(x_sample.shape),
            jnp.stack(outs['a_p']), a_s, jnp.stack(outs['c_p']), c_s,
            v_s.reshape(depth, n_dec, 1, D_B))
```

```python
import functools

import jax
import jax.numpy as jnp
from jax import lax
from jax.experimental import pallas as pl
from jax.experimental.pallas import tpu as pltpu
from jax.experimental.pallas import tpu_sc as plsc

D_MODEL = 1024
HEAD_DIM = 64
D_A = 384
D_B = 256
D_C = 384
N_HEADS_B = D_B // HEAD_DIM
CONV_A = 3
CONV_C = 31
CHUNK = 128
D_IN = 3 * D_A + 2 * D_B + 2 * D_C
D_PLE = 256
N_EXPERTS = 8
TOP_K = 2
EPS = 1e-6

O1, O2, O3 = D_A, 2 * D_A, 3 * D_A
O4 = O3 + 2 * D_B

F32 = jnp.float32
BF16 = jnp.bfloat16
I32 = jnp.int32

VMEM_LIMIT_BYTES = 56 * 1024 * 1024
SUBLANES = 8
LANES = 128

TOK_ROWS = 512
MIXER_ROWS = 1024
CONV_ROWS = 64
A_HALO = 8
C_HALO = 32
EXPERT_ROWS = 512
FF_COLS = 256
MOE_CHUNKS = 2

SC_CORES = 2
SC_WORKERS = 32
SC_WINDOW = 48


def _rms(x, g):
    return x * lax.rsqrt(jnp.mean(x * x, axis=-1, keepdims=True) + EPS) * g


def _ln(x, g, b):
    mu = jnp.mean(x, axis=-1, keepdims=True)
    xc = x - mu
    var = jnp.mean(xc * xc, axis=-1, keepdims=True)
    return xc * lax.rsqrt(var + EPS) * g + b


def _dot(a, b):
    return jnp.dot(a.astype(BF16), b, preferred_element_type=F32)


def _split(x):
    hi = x.astype(BF16)
    return hi, (x - hi.astype(F32)).astype(BF16)


def _dot_full(a, b):
    a_hi, a_lo = _split(a)
    b_hi, b_lo = _split(b)
    m = a.shape[0]
    by_hi = jnp.dot(jnp.concatenate([a_hi, a_lo], axis=0), b_hi, preferred_element_type=F32)
    return by_hi[:m] + by_hi[m:] + jnp.dot(a_hi, b_lo, preferred_element_type=F32)


def _const_spec(shape):
    return pl.BlockSpec(shape, lambda *_: (0,) * len(shape), pipeline_mode=pl.Buffered(1))


def _layer_spec(stacked, layer):
    idx = (layer,) if isinstance(layer, int) else tuple(layer)
    rest = stacked.shape[len(idx):]
    return pl.BlockSpec((None,) * len(idx) + rest, lambda *_: idx + (0,) * len(rest),
                        pipeline_mode=pl.Buffered(1))


def _whole_out_spec(shape):
    return pl.BlockSpec(shape, lambda *_: (0,) * len(shape))


def _tc_params():
    return pltpu.CompilerParams(dimension_semantics=("arbitrary",),
                                vmem_limit_bytes=VMEM_LIMIT_BYTES)


def _normed_groups(y_a, y_b, y_c, gout_ref):
    return jnp.concatenate([_rms(y_a, gout_ref[:, :D_A]),
                            _rms(y_b, gout_ref[:, D_A:D_A + D_B]),
                            _rms(y_c, gout_ref[:, D_A + D_B:])], axis=-1)


def _mixer_seq_kernel(hp_ref, hs_ref, gmix_ref, win_ref, wca_ref, wscat_ref, bsfull_ref,
                      glnb_ref, blnb_ref, wcc_ref, bcc_ref, glnc_ref, blnc_ref, gout_ref,
                      wo_ref, ride_ref, hout_ref, newa_ref, newc_ref, ride_out_ref,
                      qext_ref, gext_ref, gshift_ref, conv_ref, wcc8_ref, *, tiles_per_seq,
                      n_seq_tiles):
    g = pl.program_id(0)
    tm = hp_ref.shape[0]

    @pl.when(g == 0)
    def _():
        for k in range(CONV_C):
            wcc8_ref[k] = jnp.broadcast_to(wcc_ref[k:k + 1, :], (SUBLANES, D_C))

    @pl.when(g < n_seq_tiles)
    def _sequence_tile():
        ride_out_ref[...] = ride_ref[...].astype(BF16)

        @pl.when(g % tiles_per_seq == 0)
        def _():
            qext_ref[0:A_HALO, :] = jnp.zeros((A_HALO, D_A), F32)
            gext_ref[0:C_HALO, :] = jnp.zeros((C_HALO, D_C), F32)

        h = hp_ref[...]
        z = _dot(_rms(h, gmix_ref[...]), win_ref[...])

        qext_ref[A_HALO:A_HALO + tm, :] = z[:, O1:O2] * z[:, O2:O3]
        conv_a = jnp.zeros((tm, D_A), F32)
        for k in range(CONV_A):
            off = A_HALO - (CONV_A - 1) + k
            conv_a = conv_a + wca_ref[k:k + 1, :] * qext_ref[off:off + tm, :]
        y_a = z[:, :O1] * conv_a
        last_q = qext_ref[A_HALO + tm - (CONV_A - 1):A_HALO + tm, :]
        newa_ref[...] = last_q
        qext_ref[A_HALO - (CONV_A - 1):A_HALO, :] = last_q

        zb = jax.nn.gelu(z[:, O3:O4])
        v = _ln(zb[:, D_B:], glnb_ref[...], blnb_ref[...])
        row = lax.broadcasted_iota(I32, (CHUNK, N_HEADS_B * CHUNK), 0)
        col = lax.broadcasted_iota(I32, (CHUNK, N_HEADS_B * CHUNK), 1)
        w_tril = jnp.where((col % CHUNK) <= row, wscat_ref[...], 0.0).astype(BF16)
        lane_head = lax.broadcasted_iota(I32, (CHUNK, D_B), 1) // HEAD_DIM
        s_chunks = []
        for c in range(tm // CHUNK):
            vc = v[c * CHUNK:(c + 1) * CHUNK, :]
            vstack = jnp.concatenate(
                [jnp.where(lane_head == hd, vc, 0.0) for hd in range(N_HEADS_B)], axis=0)
            s_chunks.append(_dot(w_tril, vstack.astype(BF16)) + bsfull_ref[...])
        y_b = zb[:, :D_B] * jnp.concatenate(s_chunks, axis=0)

        gext_ref[C_HALO:C_HALO + tm, :] = (z[:, O4:O4 + D_C]
                                           * jax.nn.sigmoid(z[:, O4 + D_C:]))
        n_shift = gshift_ref.shape[1]
        for s in range(1, SUBLANES):
            gshift_ref[s - 1] = gext_ref[s:s + n_shift, :]
        base = C_HALO - (CONV_C - 1)
        for r0 in range(0, tm, CONV_ROWS):
            acc = jnp.zeros((CONV_ROWS // SUBLANES, SUBLANES, D_C), F32)
            for k in range(CONV_C):
                lo = (base + k) // SUBLANES * SUBLANES + r0
                s = (base + k) % SUBLANES
                window = (gext_ref[lo:lo + CONV_ROWS, :] if s == 0
                          else gshift_ref[s - 1, lo:lo + CONV_ROWS, :])
                acc = acc + wcc8_ref[k][None] * window.reshape(acc.shape)
            conv_ref[r0:r0 + CONV_ROWS, :] = acc.reshape(CONV_ROWS, D_C)
        y_c = jax.nn.silu(_ln(conv_ref[...] + bcc_ref[...], glnc_ref[...], blnc_ref[...]))
        last_g = gext_ref[C_HALO + tm - (CONV_C - 1):C_HALO + tm, :]
        newc_ref[...] = last_g
        gext_ref[C_HALO - (CONV_C - 1):C_HALO, :] = last_g

        hout_ref[...] = h + _dot(_normed_groups(y_a, y_b, y_c, gout_ref), wo_ref[...])

    @pl.when(g == n_seq_tiles)
    def _append_sample_rows():
        hout_ref[0:hs_ref.shape[0], :] = hs_ref[...]


def _ride_specs(ride, n_steps):
    rows = ride.shape[0] // n_steps
    assert rows * n_steps == ride.shape[0]
    slab = lambda g: (jnp.minimum(g, n_steps - 1), 0)
    return (pl.BlockSpec((rows, ride.shape[1]), slab), pl.BlockSpec((rows, ride.shape[1]), slab),
            jax.ShapeDtypeStruct(ride.shape, BF16))


def _mixer_seq(hp, hs, lw, n_seq, seq, ride):
    tm = MIXER_ROWS
    tiles_per_seq = seq // tm
    n_seq_tiles = n_seq * tiles_per_seq
    join = hs is not None
    if not join:
        hs = jnp.zeros((SUBLANES, D_MODEL), F32)
    n_out = n_seq * seq + (hs.shape[0] if join else 0)
    weights = [lw[k] for k in ('g_mix', 'w_in', 'w_conv_a', 'w_s_cat', 'b_s_full', 'g_ln_b',
                               'b_ln_b', 'w_conv_c', 'b_conv_c', 'g_ln_c', 'b_ln_c', 'g_out',
                               'w_o')]
    seq_of = lambda g: jnp.minimum(g // tiles_per_seq, n_seq - 1)
    n_shift = tm + C_HALO - SUBLANES
    ride_in_spec, ride_out_spec, ride_out_shape = _ride_specs(ride, n_seq_tiles)
    return pl.pallas_call(
        functools.partial(_mixer_seq_kernel, tiles_per_seq=tiles_per_seq,
                          n_seq_tiles=n_seq_tiles),
        grid=(n_seq_tiles + int(join),),
        in_specs=[pl.BlockSpec((tm, D_MODEL), lambda g: (jnp.minimum(g, n_seq_tiles - 1), 0)),
                  _const_spec(hs.shape)]
        + [_layer_spec(*w) for w in weights] + [ride_in_spec],
        out_specs=[pl.BlockSpec((tm, D_MODEL), lambda g: (g, 0)),
                   pl.BlockSpec((None, CONV_A - 1, D_A), lambda g: (seq_of(g), 0, 0)),
                   pl.BlockSpec((None, CONV_C - 1, D_C), lambda g: (seq_of(g), 0, 0)),
                   ride_out_spec],
        out_shape=[jax.ShapeDtypeStruct((n_out, D_MODEL), F32),
                   jax.ShapeDtypeStruct((n_seq, CONV_A - 1, D_A), F32),
                   jax.ShapeDtypeStruct((n_seq, CONV_C - 1, D_C), F32),
                   ride_out_shape],
        scratch_shapes=[pltpu.VMEM((A_HALO + tm, D_A), F32),
                        pltpu.VMEM((C_HALO + tm, D_C), F32),
                        pltpu.VMEM((SUBLANES - 1, n_shift, D_C), F32),
                        pltpu.VMEM((tm, D_C), F32),
                        pltpu.VMEM((CONV_C, SUBLANES, D_C), F32)],
        compiler_params=_tc_params(),
        name="mixer_seq",
    )(hp, hs, *[w for w, _ in weights], ride)


def _mixer_row_kernel(*refs, n_carried):
    (h_ref, sa_ref, sc_ref, gmix_ref, win_ref, wca_ref, wscat_ref, bsfull_ref, glnb_ref,
     blnb_ref, wcc_ref, bcc_ref, glnc_ref, blnc_ref, gout_ref, wo_ref, wpg_ref,
     wpp_ref) = refs[:18]
    (hout_ref, newa_ref, newc_ref, v_ref, win_bf_ref, wo_bf_ref, wpg_bf_ref,
     wpp_bf_ref) = refs[18 + n_carried:]

    @pl.when(pl.program_id(0) > 0)
    def _():
        for ref in (newa_ref, newc_ref, v_ref):
            ref[...] = jnp.zeros(ref.shape, F32)

    @pl.when(pl.program_id(0) == 0)
    def _():
        _mixer_row_body(h_ref, sa_ref, sc_ref, gmix_ref, win_ref, wca_ref, wscat_ref,
                        bsfull_ref, glnb_ref, blnb_ref, wcc_ref, bcc_ref, glnc_ref, blnc_ref,
                        gout_ref, wo_ref, wpg_ref, wpp_ref, hout_ref, newa_ref, newc_ref,
                        v_ref, win_bf_ref, wo_bf_ref, wpg_bf_ref, wpp_bf_ref)


def _mixer_row_body(h_ref, sa_ref, sc_ref, gmix_ref, win_ref, wca_ref, wscat_ref,
                    bsfull_ref, glnb_ref, blnb_ref, wcc_ref, bcc_ref, glnc_ref, blnc_ref,
                    gout_ref, wo_ref, wpg_ref, wpp_ref, hout_ref, newa_ref, newc_ref, v_ref,
                    win_bf_ref, wo_bf_ref, wpg_bf_ref, wpp_bf_ref):
    for src, dst in ((win_ref, win_bf_ref), (wo_ref, wo_bf_ref), (wpg_ref, wpg_bf_ref),
                     (wpp_ref, wpp_bf_ref)):
        dst[...] = src[...].astype(BF16)

    h = h_ref[...]
    z = _dot_full(_rms(h, gmix_ref[...]), win_ref[...])

    q = z[:, O1:O2] * z[:, O2:O3]
    conv_a = wca_ref[CONV_A - 1:CONV_A, :] * q
    for k in range(CONV_A - 1):
        conv_a = conv_a + wca_ref[k:k + 1, :] * sa_ref[k]
    y_a = z[:, :O1] * conv_a
    for k in range(CONV_A - 2):
        newa_ref[:, k, :] = sa_ref[k + 1]
    newa_ref[:, CONV_A - 2, :] = q

    zb = jax.nn.gelu(z[:, O3:O4])
    v = _ln(zb[:, D_B:], glnb_ref[...], blnb_ref[...])
    v_ref[...] = v
    w_diag0 = jnp.concatenate(
        [jnp.broadcast_to(wscat_ref[0:1, hd * CHUNK:hd * CHUNK + 1], (1, HEAD_DIM))
         for hd in range(N_HEADS_B)], axis=-1)
    y_b = zb[:, :D_B] * (w_diag0 * v + bsfull_ref[0:1, :])

    glu = z[:, O4:O4 + D_C] * jax.nn.sigmoid(z[:, O4 + D_C:])
    conv_c = wcc_ref[CONV_C - 1:CONV_C, :] * glu
    for k in range(CONV_C - 1):
        conv_c = conv_c + wcc_ref[k:k + 1, :] * sc_ref[k]
    y_c = jax.nn.silu(_ln(conv_c + bcc_ref[...], glnc_ref[...], blnc_ref[...]))
    for k in range(CONV_C - 2):
        newc_ref[:, k, :] = sc_ref[k + 1]
    newc_ref[:, CONV_C - 2, :] = glu

    hout_ref[...] = h + _dot_full(_normed_groups(y_a, y_b, y_c, gout_ref), wo_ref[...])


CONVERTED_BY_MIXER_ROW = ('w_in', 'w_o', 'w_ple_gate', 'w_ple_proj')


def _mixer_row(h, h_block, lw, states):
    depth, _, n_dec, _ = lw['state_a'][0].shape
    layer = lw['state_a'][1]
    weights = [lw[k] for k in ('state_a', 'state_c', 'g_mix', 'w_in_f32', 'w_conv_a', 'w_s_cat',
                               'b_s_full', 'g_ln_b', 'b_ln_b', 'w_conv_c', 'b_conv_c', 'g_ln_c',
                               'b_ln_c', 'g_out', 'w_o_f32', 'w_ple_gate_f32', 'w_ple_proj_f32')]
    bf16_shapes = [lw[k + '_f32'][0].shape[1:] for k in CONVERTED_BY_MIXER_ROW]
    state_shapes = [(n_dec, CONV_A - 1, D_A), (n_dec, CONV_C - 1, D_C), (n_dec, D_B)]
    carried = list(states or ())
    first = 1 + len(weights)
    block_of = (lambda g: g) if states is None else (lambda g: layer)
    return pl.pallas_call(
        functools.partial(_mixer_row_kernel, n_carried=len(carried)),
        grid=(depth if states is None else 1,),
        in_specs=[pl.BlockSpec((n_dec, D_MODEL), lambda g: (h_block, 0))]
        + [_layer_spec(*w) for w in weights]
        + [pl.BlockSpec(memory_space=pl.ANY) for _ in carried],
        out_specs=[_whole_out_spec((n_dec, D_MODEL))]
        + [pl.BlockSpec((None,) + s, lambda g, s=s: (block_of(g),) + (0,) * len(s))
           for s in state_shapes]
        + [_whole_out_spec(s) for s in bf16_shapes],
        out_shape=[jax.ShapeDtypeStruct((n_dec, D_MODEL), F32)]
        + [jax.ShapeDtypeStruct((depth,) + s, F32) for s in state_shapes]
        + [jax.ShapeDtypeStruct(s, BF16) for s in bf16_shapes],
        input_output_aliases={first + j: 1 + j for j in range(len(carried))},
        compiler_params=_tc_params(),
        name="mixer_row",
    )(h, *[w for w, _ in weights], *carried)


def _ple(h, p, gple_ref, wpg_ref, wpp_ref, dot):
    gate = jax.nn.sigmoid(dot(_rms(h, gple_ref[...]), wpg_ref[...]))
    return h + gate * dot(p, wpp_ref[...])


def _ffn_dense_kernel(h_ref, p_ref, gffn_ref, wg_ref, wu_ref, wd_ref, gple_ref, wpg_ref,
                      wpp_ref, ride_ref, out_ref, ride_out_ref):
    ride_out_ref[...] = ride_ref[...].astype(BF16)
    h = h_ref[...]
    xn = _rms(h, gffn_ref[...]).astype(BF16)
    a = jax.nn.silu(_dot(xn, wg_ref[...])) * _dot(xn, wu_ref[...])
    h = h + _dot(a, wd_ref[...])
    out_ref[...] = _ple(h, p_ref[...], gple_ref, wpg_ref, wpp_ref, _dot)


def _ffn_dense(h, lw, ride):
    n_tok = h.shape[0]
    tm = TOK_ROWS
    p, layer = lw['p_seq']
    weights = [lw[k] for k in ('g_ffn', 'w_ff_gate', 'w_ff_up', 'w_ff_down', 'g_ple',
                               'w_ple_gate', 'w_ple_proj')]
    ride_in_spec, ride_out_spec, ride_out_shape = _ride_specs(ride, n_tok // tm)
    return pl.pallas_call(
        _ffn_dense_kernel,
        grid=(n_tok // tm,),
        in_specs=[pl.BlockSpec((tm, D_MODEL), lambda g: (g, 0)),
                  pl.BlockSpec((None, tm, D_PLE), lambda g: (layer, g, 0))]
        + [_layer_spec(*w) for w in weights] + [ride_in_spec],
        out_specs=[pl.BlockSpec((tm, D_MODEL), lambda g: (g, 0)), ride_out_spec],
        out_shape=[jax.ShapeDtypeStruct(h.shape, F32), ride_out_shape],
        compiler_params=_tc_params(),
        name="ffn_dense",
    )(h, p, *[w for w, _ in weights], ride)


def _ffn_row_kernel(h_ref, p_ref, gffn_ref, wg_ref, wu_ref, wd_ref, gple_ref, wpg_ref,
                    wpp_ref, out_ref, wg_bf_ref, wu_bf_ref, wd_bf_ref, xn_ref, acc_ref):
    j = pl.program_id(0)

    @pl.when(j == 0)
    def _():
        h = h_ref[...]
        xn_ref[...] = _rms(h, gffn_ref[...])
        acc_ref[...] = h

    for src, dst in ((wg_ref, wg_bf_ref), (wu_ref, wu_bf_ref), (wd_ref, wd_bf_ref)):
        dst[...] = src[...].astype(BF16)

    xn = xn_ref[...]
    a = jax.nn.silu(_dot_full(xn, wg_ref[...])) * _dot_full(xn, wu_ref[...])
    acc_ref[...] += _dot_full(a, wd_ref[...])

    @pl.when(j == pl.num_programs(0) - 1)
    def _():
        out_ref[...] = _ple(acc_ref[...], p_ref[...], gple_ref, wpg_ref, wpp_ref, _dot_full)


def _ffn_row(h, lw):
    n_dec = h.shape[0]
    d_ff = lw['w_ff_gate_f32'][0].shape[-1]
    ff = lw['w_ff_gate_f32'][1]
    return pl.pallas_call(
        _ffn_row_kernel,
        grid=(d_ff // FF_COLS,),
        in_specs=[_const_spec(h.shape), _layer_spec(*lw['p_row']), _layer_spec(*lw['g_ffn']),
                  pl.BlockSpec((None, D_MODEL, FF_COLS), lambda j: (ff, 0, j)),
                  pl.BlockSpec((None, D_MODEL, FF_COLS), lambda j: (ff, 0, j)),
                  pl.BlockSpec((None, FF_COLS, D_MODEL), lambda j: (ff, j, 0)),
                  _layer_spec(*lw['g_ple']), _layer_spec(*lw['w_ple_gate_f32']),
                  _layer_spec(*lw['w_ple_proj_f32'])],
        out_specs=[_whole_out_spec(h.shape),
                   pl.BlockSpec((D_MODEL, FF_COLS), lambda j: (0, j)),
                   pl.BlockSpec((D_MODEL, FF_COLS), lambda j: (0, j)),
                   pl.BlockSpec((FF_COLS, D_MODEL), lambda j: (j, 0))],
        out_shape=[jax.ShapeDtypeStruct(h.shape, F32),
                   jax.ShapeDtypeStruct((D_MODEL, d_ff), BF16),
                   jax.ShapeDtypeStruct((D_MODEL, d_ff), BF16),
                   jax.ShapeDtypeStruct((d_ff, D_MODEL), BF16)],
        scratch_shapes=[pltpu.VMEM((n_dec, D_MODEL), F32), pltpu.VMEM((n_dec, D_MODEL), F32)],
        compiler_params=_tc_params(),
        name="ffn_row",
    )(h, *[lw[k][0] for k in ('p_row', 'g_ffn', 'w_ff_gate_f32', 'w_ff_up_f32',
                              'w_ff_down_f32', 'g_ple', 'w_ple_gate_f32', 'w_ple_proj_f32')])


def _router_kernel(h_ref, gffn_ref, wr_ref, mi_ref, mf_ref, cnt_ref, carry_ref, *, n_tok):
    g = pl.program_id(0)
    tm = h_ref.shape[0]

    @pl.when(g == 0)
    def _():
        carry_ref[...] = jnp.zeros(carry_ref.shape, F32)

    valid_row = (g * tm + lax.broadcasted_iota(I32, (tm, 1), 0)) < n_tok
    xn = _rms(jnp.where(valid_row, h_ref[...], 0.0), gffn_ref[...])
    logits = _dot_full(xn, wr_ref[...]).T[:N_EXPERTS]
    valid = (g * tm + lax.broadcasted_iota(I32, (1, tm), 1)) < n_tok
    e = lax.broadcasted_iota(I32, logits.shape, 0)
    m1 = jnp.max(logits, axis=0, keepdims=True)
    i1 = jnp.min(jnp.where(logits == m1, e, N_EXPERTS), axis=0, keepdims=True)
    rest = jnp.where(e == i1, -jnp.inf, logits)
    m2 = jnp.max(rest, axis=0, keepdims=True)
    i2 = jnp.min(jnp.where(rest == m2, e, N_EXPERTS), axis=0, keepdims=True)
    e2 = jnp.exp(m2 - m1)
    denom = 1.0 + e2
    w1 = 1.0 / denom
    w2 = e2 / denom

    oh1 = jnp.where((e == i1) & valid, 1.0, 0.0)
    oh2 = jnp.where((e == i2) & valid, 1.0, 0.0)
    member = oh1 + oh2
    r = lax.broadcasted_iota(I32, (tm, tm), 0)
    c = lax.broadcasted_iota(I32, (tm, tm), 1)
    earlier = jnp.where(r < c, 1.0, 0.0).astype(BF16)
    pos = _dot(member, earlier) + carry_ref[...]
    pos1 = jnp.sum(oh1 * pos, axis=0, keepdims=True).astype(I32)
    pos2 = jnp.sum(oh2 * pos, axis=0, keepdims=True).astype(I32)
    carry_ref[...] = carry_ref[...] + jnp.sum(member, axis=1, keepdims=True)
    cnt_ref[...] = carry_ref[...]

    mi_ref[...] = jnp.where(e == 0, i1, jnp.where(e == 1, i2,
                            jnp.where(e == 2, pos1, jnp.where(e == 3, pos2, 0))))
    gates = jnp.where(e == 0, w1, jnp.where(e == 1, w2, 0.0))
    lanes = wr_ref.shape[1]
    gates = jnp.concatenate([gates, jnp.zeros((lanes - N_EXPERTS, tm), F32)], axis=0)
    mf_ref[...] = gates.T[:, :N_EXPERTS]


def _router(h, tile0, n_tok, lw):
    tm = TOK_ROWS
    return pl.pallas_call(
        functools.partial(_router_kernel, n_tok=n_tok),
        grid=(pl.cdiv(n_tok, tm),),
        in_specs=[pl.BlockSpec((tm, D_MODEL), lambda g: (tile0 + g, 0)),
                  _layer_spec(*lw['g_ffn']), _layer_spec(*lw['w_router'])],
        out_specs=[pl.BlockSpec((N_EXPERTS, tm), lambda g: (0, g)),
                   pl.BlockSpec((tm, N_EXPERTS), lambda g: (g, 0)),
                   _whole_out_spec((N_EXPERTS, 1))],
        out_shape=[jax.ShapeDtypeStruct((N_EXPERTS, n_tok), I32),
                   jax.ShapeDtypeStruct((n_tok, N_EXPERTS), F32),
                   jax.ShapeDtypeStruct((N_EXPERTS, 1), F32)],
        scratch_shapes=[pltpu.VMEM((N_EXPERTS, 1), F32)],
        compiler_params=_tc_params(),
        name="router",
    )(h, lw['g_ffn'][0], lw['w_router'][0])


def _sc_chunk(n_rows):
    per_worker = pl.cdiv(n_rows, SC_WORKERS)
    return pl.cdiv(per_worker, SC_WINDOW) * SC_WINDOW


def _sc_worker_base(n_rows, chunk):
    wid = lax.axis_index("s") * SC_CORES + lax.axis_index("c")
    return jnp.minimum(wid * chunk, n_rows - chunk)


def _sc_scatter_rows(x, row0, dest, n_out):
    n = dest.shape[0] // TOP_K
    chunk = _sc_chunk(n)
    mesh = plsc.VectorSubcoreMesh(core_axis_name="c", subcore_axis_name="s")

    @functools.partial(
        pl.kernel, mesh=mesh,
        out_type=jax.ShapeDtypeStruct((n_out, D_MODEL), x.dtype),
        scratch_types=[pltpu.VMEM((SC_WINDOW,), I32) for _ in range(TOP_K)]
        + [pltpu.VMEM((SC_WINDOW, D_MODEL), x.dtype), pltpu.SemaphoreType.DMA],
        name="sc_scatter_rows",
    )
    def scatter(x_hbm, dest_hbm, out_hbm, idx0_v, idx1_v, rows_v, sem):
        base = _sc_worker_base(n, chunk)

        @pl.loop(0, chunk // SC_WINDOW)
        def _(j):
            off = pl.multiple_of(base + j * SC_WINDOW, 8)
            pltpu.sync_copy(dest_hbm.at[pl.ds(off, SC_WINDOW)], idx0_v)
            pltpu.sync_copy(dest_hbm.at[pl.ds(n + off, SC_WINDOW)], idx1_v)
            pltpu.sync_copy(x_hbm.at[pl.ds(row0 + off, SC_WINDOW)], rows_v)
            first = pltpu.async_copy(rows_v, out_hbm.at[idx0_v], sem)
            second = pltpu.async_copy(rows_v, out_hbm.at[idx1_v], sem)
            first.wait()
            second.wait()

    return scatter(x, dest)


def _sc_gather_rows(y, idx):
    n = idx.shape[0]
    width = y.shape[1]
    chunk = _sc_chunk(n)
    mesh = plsc.VectorSubcoreMesh(core_axis_name="c", subcore_axis_name="s")

    @functools.partial(
        pl.kernel, mesh=mesh,
        out_type=jax.ShapeDtypeStruct((n, width), y.dtype),
        scratch_types=[pltpu.VMEM((SC_WINDOW,), I32),
                       pltpu.VMEM((SC_WINDOW, width), y.dtype), pltpu.SemaphoreType.DMA],
        name="sc_gather_rows",
    )
    def gather(y_hbm, idx_hbm, out_hbm, idx_v, rows_v, sem):
        base = _sc_worker_base(n, chunk)

        @pl.loop(0, chunk // SC_WINDOW)
        def _(j):
            off = pl.multiple_of(base + j * SC_WINDOW, 8)
            pltpu.sync_copy(idx_hbm.at[pl.ds(off, SC_WINDOW)], idx_v)
            pltpu.async_copy(y_hbm.at[idx_v], rows_v, sem).wait()
            pltpu.sync_copy(rows_v, out_hbm.at[pl.ds(off, SC_WINDOW)])

    return gather(y, idx)


def _pack_bf16_halves(y):
    half = y.shape[1] // 2
    rounded = y.astype(BF16).astype(F32)
    hi = lax.bitcast_convert_type(rounded[:, :half], jnp.uint32)
    lo = lax.bitcast_convert_type(rounded[:, half:], jnp.uint32)
    return hi | (lo >> 16)


def _unpack_bf16_halves(packed):
    hi = lax.bitcast_convert_type(packed & jnp.uint32(0xFFFF0000), F32)
    lo = lax.bitcast_convert_type(packed << 16, F32)
    return jnp.concatenate([hi, lo], axis=1)


def _expert_kernel(tile_expert_ref, n_valid_ref, xs_ref, gffn_ref, wg_ref, wu_ref, wd_ref,
                   y_ref):
    @pl.when(pl.program_id(0) < n_valid_ref[0])
    def _():
        xn = _rms(xs_ref[...], gffn_ref[...]).astype(BF16)
        a = jax.nn.silu(_dot(xn, wg_ref[...])) * _dot(xn, wu_ref[...])
        y_ref[...] = _pack_bf16_halves(_dot(a, wd_ref[...]))


def _experts(xs, tile_expert, n_valid, lw):
    n_slots = xs.shape[0]
    te = EXPERT_ROWS
    d_exp = lw['w_ex_gate'][0].shape[-1]
    moe = lw['w_ex_gate'][1]
    row_block = lambda g, tex, nv: (jnp.minimum(g, nv[0] - 1), 0)
    w_block = lambda g, tex, nv: (moe, tex[g], 0, 0)
    return pl.pallas_call(
        _expert_kernel,
        grid_spec=pltpu.PrefetchScalarGridSpec(
            num_scalar_prefetch=2,
            grid=(n_slots // te,),
            in_specs=[pl.BlockSpec((te, D_MODEL), row_block),
                      _layer_spec(*lw['g_ffn']),
                      pl.BlockSpec((None, None, D_MODEL, d_exp), w_block),
                      pl.BlockSpec((None, None, D_MODEL, d_exp), w_block),
                      pl.BlockSpec((None, None, d_exp, D_MODEL), w_block)],
            out_specs=pl.BlockSpec((te, D_MODEL // 2), row_block)),
        out_shape=jax.ShapeDtypeStruct((n_slots, D_MODEL // 2), jnp.uint32),
        compiler_params=_tc_params(),
        name="experts",
    )(tile_expert, n_valid, xs, *[lw[k][0] for k in ('g_ffn', 'w_ex_gate', 'w_ex_up',
                                                        'w_ex_down')])


def _combine_math(h_ref, yg_ref, mf_ref, p, gple_ref, wpg_ref, wpp_ref, gfin_ref):
    gates = mf_ref[...]
    h = h_ref[...] + (gates[:, 0:1] * _unpack_bf16_halves(yg_ref[0])
                      + gates[:, 1:2] * _unpack_bf16_halves(yg_ref[1]))
    return _rms(_ple(h, p, gple_ref, wpg_ref, wpp_ref, _dot), gfin_ref[...])


def _combine_seq_kernel(h_ref, yg_ref, mf_ref, pp_ref, gple_ref, wpg_ref, wpp_ref, gfin_ref,
                        ybuf_ref, yp_ref):
    del ybuf_ref
    yp_ref[...] = _combine_math(h_ref, yg_ref, mf_ref, pp_ref[...], gple_ref, wpg_ref,
                                wpp_ref, gfin_ref)


def _combine_tail_kernel(h_ref, yg_ref, mf_ref, pp_ref, ps_ref, gple_ref, wpg_ref, wpp_ref,
                         gfin_ref, ybuf_ref, yp_ref, ys_ref, pbuf_ref):
    del ybuf_ref
    g = pl.program_id(0)
    last = pl.num_programs(0) - 1
    n_dec = ps_ref.shape[0]
    pbuf_ref[...] = pp_ref[...]

    @pl.when(g == last)
    def _():
        pbuf_ref[0:n_dec, :] = ps_ref[...]

    out = _combine_math(h_ref, yg_ref, mf_ref, pbuf_ref[...], gple_ref, wpg_ref, wpp_ref,
                        gfin_ref)

    @pl.when(g < last)
    def _():
        yp_ref[...] = out

    @pl.when(g == last)
    def _():
        ys_ref[...] = out[0:n_dec, :]


def _combine(h, tile0, yg, mf, y_seq, lw, with_rows):
    tm = TOK_ROWS
    n_tiles = pl.cdiv(yg.shape[1], tm)
    last_seq_tile = y_seq.shape[0] // tm - 1
    pp, layer = lw['p_seq']
    n_dec = lw['p_row'][0].shape[-2]
    weights = [lw[k] for k in (('p_row',) if with_rows else ())
               + ('g_ple', 'w_ple_gate', 'w_ple_proj', 'g_final')]
    seq_tile = lambda g: (jnp.minimum(tile0 + g, last_seq_tile), 0)
    n_in = 4 + len(weights)
    out = pl.pallas_call(
        _combine_tail_kernel if with_rows else _combine_seq_kernel,
        grid=(n_tiles,),
        in_specs=[pl.BlockSpec((tm, D_MODEL), lambda g: (tile0 + g, 0)),
                  pl.BlockSpec((TOP_K, tm, yg.shape[2]), lambda g: (0, g, 0)),
                  pl.BlockSpec((tm, N_EXPERTS), lambda g: (g, 0)),
                  pl.BlockSpec((None, tm, D_PLE), lambda g: (layer,) + seq_tile(g))]
        + [_layer_spec(*w) for w in weights]
        + [pl.BlockSpec(memory_space=pl.ANY)],
        out_specs=[pl.BlockSpec((tm, D_MODEL), seq_tile)]
        + ([_whole_out_spec((n_dec, D_MODEL))] if with_rows else []),
        out_shape=[jax.ShapeDtypeStruct(y_seq.shape, F32)]
        + ([jax.ShapeDtypeStruct((n_dec, D_MODEL), F32)] if with_rows else []),
        scratch_shapes=[pltpu.VMEM((tm, D_PLE), F32)] if with_rows else [],
        input_output_aliases={n_in: 0},
        compiler_params=_tc_params(),
        name="combine",
    )(h, yg, mf, pp, *[w for w, _ in weights], y_seq)
    return out if with_rows else (out[0], None)


def _moe_layer(h, y_seq, lw):
    n_tok = h.shape[0]
    tm = TOK_ROWS
    te = EXPERT_ROWS
    tiles = pl.cdiv(n_tok, tm)
    bounds = [tiles * c // MOE_CHUNKS for c in range(MOE_CHUNKS + 1)]
    y_rows = None
    for c in range(MOE_CHUNKS):
        tile0 = bounds[c]
        n = min(bounds[c + 1] * tm, n_tok) - tile0 * tm
        mi, mf, counts = _router(h, tile0, n, lw)

        cnt = counts[:, 0].astype(I32)
        padded = (cnt + te - 1) // te * te
        ends = jnp.cumsum(padded)
        starts = ends - padded
        experts = jnp.arange(N_EXPERTS, dtype=I32)
        start_of = lambda e: jnp.sum(
            jnp.where(e[None, :] == experts[:, None], starts[:, None], 0), axis=0)
        dest = jnp.concatenate([start_of(mi[0]) + mi[2], start_of(mi[1]) + mi[3]])
        dest, h = lax.optimization_barrier((dest, h))
        n_tiles = pl.cdiv(TOP_K * n + N_EXPERTS * (te - 1), te)
        tile_start = jnp.arange(n_tiles, dtype=I32) * te
        last_used = jnp.max(jnp.where(padded > 0, experts, 0))
        tile_expert = jnp.minimum(
            jnp.sum(tile_start[:, None] >= ends[None, :], axis=-1).astype(I32), last_used)
        n_valid = (ends[-1:] // te).astype(I32)

        xs = _sc_scatter_rows(h, tile0 * tm, dest, n_tiles * te)
        y = _experts(xs, tile_expert, n_valid, lw)
        yg = _sc_gather_rows(y, dest).reshape(TOP_K, n, y.shape[1])
        y_seq, rows = _combine(h, tile0, yg, mf, y_seq, lw, with_rows=c == MOE_CHUNKS - 1)
        y_rows = rows if rows is not None else y_rows
    return y_seq, y_rows


def kernel(x_prompt, x_sample, state_conv_a, state_conv_c, p_prompt, p_sample, g_mix, w_in, w_conv_a, w_s, b_s, g_ln_b, b_ln_b, w_conv_c, b_conv_c, g_ln_c, b_ln_c, g_out, w_o, g_ffn, w_ff_gate, w_ff_up, w_ff_down, w_router, w_ex_gate, w_ex_up, w_ex_down, g_ple, w_ple_gate, w_ple_proj, g_final):
    depth = g_mix.shape[0]
    n_seq, seq, _ = x_prompt.shape
    n_dec = x_sample.shape[0]
    n_prompt = n_seq * seq
    assert depth == 2 and x_sample.shape[1] == 1
    assert seq % MIXER_ROWS == 0 and TOK_ROWS % n_dec == 0 and w_ff_gate.shape[-1] % FF_COLS == 0

    vec = lambda x: x.reshape(x.shape[0], 1, -1)
    per_layer = {
        'g_mix': vec(g_mix), 'w_in_f32': w_in, 'w_conv_a': w_conv_a,
        'w_s_cat': jnp.transpose(w_s, (0, 2, 1, 3)).reshape(depth, CHUNK, N_HEADS_B * CHUNK),
        'b_s_full': jnp.repeat(jnp.swapaxes(b_s, 1, 2), HEAD_DIM, axis=2),
        'g_ln_b': vec(g_ln_b), 'b_ln_b': vec(b_ln_b), 'w_conv_c': w_conv_c,
        'b_conv_c': vec(b_conv_c), 'g_ln_c': vec(g_ln_c), 'b_ln_c': vec(b_ln_c),
        'g_out': vec(g_out), 'w_o_f32': w_o, 'g_ffn': vec(g_ffn), 'g_ple': vec(g_ple),
        'w_ple_gate_f32': w_ple_gate, 'w_ple_proj_f32': w_ple_proj,
        'state_a': jnp.swapaxes(state_conv_a, 1, 2), 'state_c': jnp.swapaxes(state_conv_c, 1, 2),
        'p_seq': p_prompt.reshape(depth, n_prompt, D_PLE),
        'p_row': p_sample.reshape(depth, n_dec, D_PLE),
    }
    per_dense = {'w_ff_gate_f32': w_ff_gate, 'w_ff_up_f32': w_ff_up, 'w_ff_down_f32': w_ff_down}
    one_layer = lambda w: (w.reshape((1,) + w.shape), 0)
    per_moe = {'w_router': jnp.pad(w_router, ((0, 0), (0, 0), (0, LANES - N_EXPERTS)))}
    to_convert = [('w_ex_gate', w_ex_gate), ('w_ex_up', w_ex_up), ('w_ex_down', w_ex_down)]

    def with_ride(call):
        name, w = to_convert.pop(0)
        *outputs, w_bf16 = call(w.reshape(-1, w.shape[-1]))
        per_moe[name] = w_bf16.reshape(w.shape)
        return outputs

    hp = x_prompt.reshape(n_prompt, D_MODEL)
    hs = x_sample.reshape(n_dec, D_MODEL)
    outs = {'a_p': [], 'c_p': []}
    row_states = None
    for i in range(depth):
        is_expert_layer = i % 2 == 1
        lw = {k: (v, i) for k, v in per_layer.items()}
        lw.update({k: (v, i // 2) for k, v in (per_dense if not is_expert_layer else {}).items()})
        lw['g_final'] = (g_final.reshape(1, 1, -1), 0)

        hs, *row_states, w0, w1, w2, w3 = _mixer_row(hs, 0, lw, row_states)
        lw.update({k: one_layer(w) for k, w in zip(CONVERTED_BY_MIXER_ROW, (w0, w1, w2, w3))})
        mixer_in = hp
        hp, a_p, c_p = with_ride(functools.partial(
            _mixer_seq, hp, hs if is_expert_layer else None, lw, n_seq, seq))
        outs['a_p'].append(a_p)
        outs['c_p'].append(c_p)
        if not is_expert_layer:
            hs, *converted = _ffn_row(hs, lw)
            lw.update({k: one_layer(w) for k, w in
                       zip(('w_ff_gate', 'w_ff_up', 'w_ff_down'), converted)})
            hp, = with_ride(functools.partial(_ffn_dense, hp, lw))
        else:
            lw.update({k: (v, i // 2) for k, v in per_moe.items()})
            y_prompt, y_sample = _moe_layer(hp, mixer_in, lw)

    a_s, c_s, v_s = row_states
    return (y_prompt.reshape(x_prompt.shape), y_sample.reshape(x_sample.shape),
            jnp.stack(outs['a_p']), a_s, jnp.stack(outs['c_p']), c_s,
            v_s.reshape(depth, n_dec, 1, D_B))
```

```python
import collections
import functools

import jax
import jax.numpy as jnp
from jax import lax
from jax.experimental import pallas as pl
from jax.experimental.pallas import tpu as pltpu
from jax.experimental.pallas import tpu_sc as plsc

D_MODEL = 1024
HEAD_DIM = 64
D_A = 384
D_B = 256
D_C = 384
N_HEADS_B = D_B // HEAD_DIM
CONV_A = 3
CONV_C = 31
CHUNK = 128
D_IN = 3 * D_A + 2 * D_B + 2 * D_C
D_PLE = 256
N_EXPERTS = 8
TOP_K = 2
EPS = 1e-6

O1, O2, O3 = D_A, 2 * D_A, 3 * D_A
O4 = O3 + 2 * D_B

F32 = jnp.float32
BF16 = jnp.bfloat16
I32 = jnp.int32

VMEM_LIMIT_BYTES = 56 * 1024 * 1024
SUBLANES = 8
LANES = 128

TOK_ROWS = 512
MIXER_ROWS = 1024
COMBINE_ROWS = 1024
CONV_ROWS = 64
A_HALO = 8
C_HALO = 32
EXPERT_ROWS = 512
FF_COLS = 256
MOE_CHUNKS = 2

SC_CORES = 2
SC_WORKERS = 32
SC_WINDOW = 48


def _rms(x, g):
    return x * lax.rsqrt(jnp.mean(x * x, axis=-1, keepdims=True) + EPS) * g


def _ln(x, g, b):
    mu = jnp.mean(x, axis=-1, keepdims=True)
    xc = x - mu
    var = jnp.mean(xc * xc, axis=-1, keepdims=True)
    return xc * lax.rsqrt(var + EPS) * g + b


def _dot(a, b):
    return jnp.dot(a.astype(BF16), b, preferred_element_type=F32)


def _split(x):
    hi = x.astype(BF16)
    return hi, (x - hi.astype(F32)).astype(BF16)


def _dot_full(a, b):
    a_hi, a_lo = _split(a)
    b_hi, b_lo = _split(b)
    m = a.shape[0]
    by_hi = jnp.dot(jnp.concatenate([a_hi, a_lo], axis=0), b_hi, preferred_element_type=F32)
    return by_hi[:m] + by_hi[m:] + jnp.dot(a_hi, b_lo, preferred_element_type=F32)


def _const_spec(shape):
    return pl.BlockSpec(shape, lambda *_: (0,) * len(shape), pipeline_mode=pl.Buffered(1))


_Lanes = collections.namedtuple('_Lanes', 'layer offset size')


def _layer_spec(stacked, layer):
    if isinstance(layer, _Lanes):
        where = (layer.layer, 0, layer.offset // layer.size)
        return pl.BlockSpec((None, 1, layer.size), lambda *_: where,
                            pipeline_mode=pl.Buffered(1))
    idx = (layer,) if isinstance(layer, int) else tuple(layer)
    rest = stacked.shape[len(idx):]
    return pl.BlockSpec((None,) * len(idx) + rest, lambda *_: idx + (0,) * len(rest),
                        pipeline_mode=pl.Buffered(1))


def _whole_out_spec(shape):
    return pl.BlockSpec(shape, lambda *_: (0,) * len(shape))


def _tc_params():
    return pltpu.CompilerParams(dimension_semantics=("arbitrary",),
                                vmem_limit_bytes=VMEM_LIMIT_BYTES)


def _normed_groups(y_a, y_b, y_c, gout_ref):
    return jnp.concatenate([_rms(y_a, gout_ref[:, :D_A]),
                            _rms(y_b, gout_ref[:, D_A:D_A + D_B]),
                            _rms(y_c, gout_ref[:, D_A + D_B:])], axis=-1)


def _mixer_seq_kernel(hp_ref, hs_ref, gmix_ref, win_ref, wca_ref, wscat_ref, bsfull_ref,
                      glnb_ref, blnb_ref, wcc_ref, bcc_ref, glnc_ref, blnc_ref, gout_ref,
                      wo_ref, ride_ref, hout_ref, newa_ref, newc_ref, ride_out_ref,
                      qext_ref, gext_ref, gshift_ref, conv_ref, wcc8_ref, *, tiles_per_seq,
                      n_seq_tiles):
    g = pl.program_id(0)
    tm = hp_ref.shape[0]

    @pl.when(g == 0)
    def _():
        for k in range(CONV_C):
            wcc8_ref[k] = jnp.broadcast_to(wcc_ref[k:k + 1, :], (SUBLANES, D_C))

    @pl.when(g < n_seq_tiles)
    def _sequence_tile():
        ride_out_ref[...] = ride_ref[...].astype(BF16)

        @pl.when(g % tiles_per_seq == 0)
        def _():
            qext_ref[0:A_HALO, :] = jnp.zeros((A_HALO, D_A), F32)
            gext_ref[0:C_HALO, :] = jnp.zeros((C_HALO, D_C), F32)

        h = hp_ref[...]
        z = _dot(_rms(h, gmix_ref[...]), win_ref[...])

        qext_ref[A_HALO:A_HALO + tm, :] = z[:, O1:O2] * z[:, O2:O3]
        conv_a = jnp.zeros((tm, D_A), F32)
        for k in range(CONV_A):
            off = A_HALO - (CONV_A - 1) + k
            conv_a = conv_a + wca_ref[k:k + 1, :] * qext_ref[off:off + tm, :]
        y_a = z[:, :O1] * conv_a
        last_q = qext_ref[A_HALO + tm - (CONV_A - 1):A_HALO + tm, :]
        newa_ref[...] = last_q
        qext_ref[A_HALO - (CONV_A - 1):A_HALO, :] = last_q

        zb = jax.nn.gelu(z[:, O3:O4])
        v = _ln(zb[:, D_B:], glnb_ref[...], blnb_ref[...])
        row = lax.broadcasted_iota(I32, (CHUNK, N_HEADS_B * CHUNK), 0)
        col = lax.broadcasted_iota(I32, (CHUNK, N_HEADS_B * CHUNK), 1)
        w_tril = jnp.where((col % CHUNK) <= row, wscat_ref[...], 0.0).astype(BF16)
        lane_head = lax.broadcasted_iota(I32, (CHUNK, D_B), 1) // HEAD_DIM
        s_chunks = []
        for c in range(tm // CHUNK):
            vc = v[c * CHUNK:(c + 1) * CHUNK, :]
            vstack = jnp.concatenate(
                [jnp.where(lane_head == hd, vc, 0.0) for hd in range(N_HEADS_B)], axis=0)
            s_chunks.append(_dot(w_tril, vstack.astype(BF16)) + bsfull_ref[...])
        y_b = zb[:, :D_B] * jnp.concatenate(s_chunks, axis=0)

        gext_ref[C_HALO:C_HALO + tm, :] = (z[:, O4:O4 + D_C]
                                           * jax.nn.sigmoid(z[:, O4 + D_C:]))
        n_shift = gshift_ref.shape[1]
        for s in range(1, SUBLANES):
            gshift_ref[s - 1] = gext_ref[s:s + n_shift, :]
        base = C_HALO - (CONV_C - 1)
        for r0 in range(0, tm, CONV_ROWS):
            acc = jnp.zeros((CONV_ROWS // SUBLANES, SUBLANES, D_C), F32)
            for k in range(CONV_C):
                lo = (base + k) // SUBLANES * SUBLANES + r0
                s = (base + k) % SUBLANES
                window = (gext_ref[lo:lo + CONV_ROWS, :] if s == 0
                          else gshift_ref[s - 1, lo:lo + CONV_ROWS, :])
                acc = acc + wcc8_ref[k][None] * window.reshape(acc.shape)
            conv_ref[r0:r0 + CONV_ROWS, :] = acc.reshape(CONV_ROWS, D_C)
        y_c = jax.nn.silu(_ln(conv_ref[...] + bcc_ref[...], glnc_ref[...], blnc_ref[...]))
        last_g = gext_ref[C_HALO + tm - (CONV_C - 1):C_HALO + tm, :]
        newc_ref[...] = last_g
        gext_ref[C_HALO - (CONV_C - 1):C_HALO, :] = last_g

        hout_ref[...] = h + _dot(_normed_groups(y_a, y_b, y_c, gout_ref), wo_ref[...])

    @pl.when(g == n_seq_tiles)
    def _append_sample_rows():
        hout_ref[0:hs_ref.shape[0], :] = hs_ref[...]


def _ride_specs(ride, n_steps):
    rows = ride.shape[0] // n_steps
    assert rows * n_steps == ride.shape[0]
    slab = lambda g: (jnp.minimum(g, n_steps - 1), 0)
    return (pl.BlockSpec((rows, ride.shape[1]), slab), pl.BlockSpec((rows, ride.shape[1]), slab),
            jax.ShapeDtypeStruct(ride.shape, BF16))


def _mixer_seq(hp, hs, lw, n_seq, seq, ride):
    tm = MIXER_ROWS
    tiles_per_seq = seq // tm
    n_seq_tiles = n_seq * tiles_per_seq
    join = hs is not None
    if not join:
        hs = jnp.zeros((SUBLANES, D_MODEL), F32)
    n_out = n_seq * seq + (hs.shape[0] if join else 0)
    weights = [lw[k] for k in ('g_mix', 'w_in', 'w_conv_a', 'w_s_cat', 'b_s_full', 'g_ln_b',
                               'b_ln_b', 'w_conv_c', 'b_conv_c', 'g_ln_c', 'b_ln_c', 'g_out',
                               'w_o')]
    seq_of = lambda g: jnp.minimum(g // tiles_per_seq, n_seq - 1)
    n_shift = tm + C_HALO - SUBLANES
    ride_in_spec, ride_out_spec, ride_out_shape = _ride_specs(ride, n_seq_tiles)
    return pl.pallas_call(
        functools.partial(_mixer_seq_kernel, tiles_per_seq=tiles_per_seq,
                          n_seq_tiles=n_seq_tiles),
        grid=(n_seq_tiles + int(join),),
        in_specs=[pl.BlockSpec((tm, D_MODEL), lambda g: (jnp.minimum(g, n_seq_tiles - 1), 0)),
                  _const_spec(hs.shape)]
        + [_layer_spec(*w) for w in weights] + [ride_in_spec],
        out_specs=[pl.BlockSpec((tm, D_MODEL), lambda g: (g, 0)),
                   pl.BlockSpec((None, CONV_A - 1, D_A), lambda g: (seq_of(g), 0, 0)),
                   pl.BlockSpec((None, CONV_C - 1, D_C), lambda g: (seq_of(g), 0, 0)),
                   ride_out_spec],
        out_shape=[jax.ShapeDtypeStruct((n_out, D_MODEL), F32),
                   jax.ShapeDtypeStruct((n_seq, CONV_A - 1, D_A), F32),
                   jax.ShapeDtypeStruct((n_seq, CONV_C - 1, D_C), F32),
                   ride_out_shape],
        scratch_shapes=[pltpu.VMEM((A_HALO + tm, D_A), F32),
                        pltpu.VMEM((C_HALO + tm, D_C), F32),
                        pltpu.VMEM((SUBLANES - 1, n_shift, D_C), F32),
                        pltpu.VMEM((tm, D_C), F32),
                        pltpu.VMEM((CONV_C, SUBLANES, D_C), F32)],
        compiler_params=_tc_params(),
        name="mixer_seq",
    )(hp, hs, *[w for w, _ in weights], ride)


def _mixer_row_kernel(*refs, n_carried):
    (h_ref, sa_ref, sc_ref, gmix_ref, win_ref, wca_ref, wscat_ref, bsfull_ref, glnb_ref,
     blnb_ref, wcc_ref, bcc_ref, glnc_ref, blnc_ref, gout_ref, wo_ref, wpg_ref,
     wpp_ref) = refs[:18]
    (hout_ref, newa_ref, newc_ref, v_ref, win_bf_ref, wo_bf_ref, wpg_bf_ref,
     wpp_bf_ref) = refs[18 + n_carried:]

    @pl.when(pl.program_id(0) > 0)
    def _():
        for ref in (newa_ref, newc_ref, v_ref):
            ref[...] = jnp.zeros(ref.shape, F32)

    @pl.when(pl.program_id(0) == 0)
    def _():
        _mixer_row_body(h_ref, sa_ref, sc_ref, gmix_ref, win_ref, wca_ref, wscat_ref,
                        bsfull_ref, glnb_ref, blnb_ref, wcc_ref, bcc_ref, glnc_ref, blnc_ref,
                        gout_ref, wo_ref, wpg_ref, wpp_ref, hout_ref, newa_ref, newc_ref,
                        v_ref, win_bf_ref, wo_bf_ref, wpg_bf_ref, wpp_bf_ref)


def _mixer_row_body(h_ref, sa_ref, sc_ref, gmix_ref, win_ref, wca_ref, wscat_ref,
                    bsfull_ref, glnb_ref, blnb_ref, wcc_ref, bcc_ref, glnc_ref, blnc_ref,
                    gout_ref, wo_ref, wpg_ref, wpp_ref, hout_ref, newa_ref, newc_ref, v_ref,
                    win_bf_ref, wo_bf_ref, wpg_bf_ref, wpp_bf_ref):
    for src, dst in ((win_ref, win_bf_ref), (wo_ref, wo_bf_ref), (wpg_ref, wpg_bf_ref),
                     (wpp_ref, wpp_bf_ref)):
        dst[...] = src[...].astype(BF16)

    h = h_ref[...]
    z = _dot_full(_rms(h, gmix_ref[...]), win_ref[...])

    q = z[:, O1:O2] * z[:, O2:O3]
    conv_a = wca_ref[CONV_A - 1:CONV_A, :] * q
    for k in range(CONV_A - 1):
        conv_a = conv_a + wca_ref[k:k + 1, :] * sa_ref[k]
    y_a = z[:, :O1] * conv_a
    for k in range(CONV_A - 2):
        newa_ref[:, k, :] = sa_ref[k + 1]
    newa_ref[:, CONV_A - 2, :] = q

    zb = jax.nn.gelu(z[:, O3:O4])
    v = _ln(zb[:, D_B:], glnb_ref[...], blnb_ref[...])
    v_ref[...] = v
    w_diag0 = jnp.concatenate(
        [jnp.broadcast_to(wscat_ref[0:1, hd * CHUNK:hd * CHUNK + 1], (1, HEAD_DIM))
         for hd in range(N_HEADS_B)], axis=-1)
    y_b = zb[:, :D_B] * (w_diag0 * v + bsfull_ref[0:1, :])

    glu = z[:, O4:O4 + D_C] * jax.nn.sigmoid(z[:, O4 + D_C:])
    conv_c = wcc_ref[CONV_C - 1:CONV_C, :] * glu
    for k in range(CONV_C - 1):
        conv_c = conv_c + wcc_ref[k:k + 1, :] * sc_ref[k]
    y_c = jax.nn.silu(_ln(conv_c + bcc_ref[...], glnc_ref[...], blnc_ref[...]))
    for k in range(CONV_C - 2):
        newc_ref[:, k, :] = sc_ref[k + 1]
    newc_ref[:, CONV_C - 2, :] = glu

    hout_ref[...] = h + _dot_full(_normed_groups(y_a, y_b, y_c, gout_ref), wo_ref[...])


CONVERTED_BY_MIXER_ROW = ('w_in', 'w_o', 'w_ple_gate', 'w_ple_proj')


def _mixer_row(h, h_block, lw, states):
    depth, _, n_dec, _ = lw['state_a'][0].shape
    layer = lw['state_a'][1]
    weights = [lw[k] for k in ('state_a', 'state_c', 'g_mix', 'w_in_f32', 'w_conv_a', 'w_s_cat',
                               'b_s_full', 'g_ln_b', 'b_ln_b', 'w_conv_c', 'b_conv_c', 'g_ln_c',
                               'b_ln_c', 'g_out', 'w_o_f32', 'w_ple_gate_f32', 'w_ple_proj_f32')]
    bf16_shapes = [lw[k + '_f32'][0].shape[1:] for k in CONVERTED_BY_MIXER_ROW]
    state_shapes = [(n_dec, CONV_A - 1, D_A), (n_dec, CONV_C - 1, D_C), (n_dec, D_B)]
    carried = list(states or ())
    first = 1 + len(weights)
    block_of = (lambda g: g) if states is None else (lambda g: layer)
    return pl.pallas_call(
        functools.partial(_mixer_row_kernel, n_carried=len(carried)),
        grid=(depth if states is None else 1,),
        in_specs=[pl.BlockSpec((n_dec, D_MODEL), lambda g: (h_block, 0))]
        + [_layer_spec(*w) for w in weights]
        + [pl.BlockSpec(memory_space=pl.ANY) for _ in carried],
        out_specs=[_whole_out_spec((n_dec, D_MODEL))]
        + [pl.BlockSpec((None,) + s, lambda g, s=s: (block_of(g),) + (0,) * len(s))
           for s in state_shapes]
        + [_whole_out_spec(s) for s in bf16_shapes],
        out_shape=[jax.ShapeDtypeStruct((n_dec, D_MODEL), F32)]
        + [jax.ShapeDtypeStruct((depth,) + s, F32) for s in state_shapes]
        + [jax.ShapeDtypeStruct(s, BF16) for s in bf16_shapes],
        input_output_aliases={first + j: 1 + j for j in range(len(carried))},
        compiler_params=_tc_params(),
        name="mixer_row",
    )(h, *[w for w, _ in weights], *carried)


def _ple(h, p, gple_ref, wpg_ref, wpp_ref, dot):
    gate = jax.nn.sigmoid(dot(_rms(h, gple_ref[...]), wpg_ref[...]))
    return h + gate * dot(p, wpp_ref[...])


def _ffn_dense_kernel(h_ref, p_ref, gffn_ref, wg_ref, wu_ref, wd_ref, gple_ref, wpg_ref,
                      wpp_ref, ride_ref, out_ref, ride_out_ref):
    ride_out_ref[...] = ride_ref[...].astype(BF16)
    h = h_ref[...]
    xn = _rms(h, gffn_ref[...]).astype(BF16)
    a = jax.nn.silu(_dot(xn, wg_ref[...])) * _dot(xn, wu_ref[...])
    h = h + _dot(a, wd_ref[...])
    out_ref[...] = _ple(h, p_ref[...], gple_ref, wpg_ref, wpp_ref, _dot)


def _ffn_dense(h, lw, ride):
    n_tok = h.shape[0]
    tm = TOK_ROWS
    p, layer = lw['p_seq']
    weights = [lw[k] for k in ('g_ffn', 'w_ff_gate', 'w_ff_up', 'w_ff_down', 'g_ple',
                               'w_ple_gate', 'w_ple_proj')]
    ride_in_spec, ride_out_spec, ride_out_shape = _ride_specs(ride, n_tok // tm)
    return pl.pallas_call(
        _ffn_dense_kernel,
        grid=(n_tok // tm,),
        in_specs=[pl.BlockSpec((tm, D_MODEL), lambda g: (g, 0)),
                  pl.BlockSpec((None, tm, D_PLE), lambda g: (layer, g, 0))]
        + [_layer_spec(*w) for w in weights] + [ride_in_spec],
        out_specs=[pl.BlockSpec((tm, D_MODEL), lambda g: (g, 0)), ride_out_spec],
        out_shape=[jax.ShapeDtypeStruct(h.shape, F32), ride_out_shape],
        compiler_params=_tc_params(),
        name="ffn_dense",
    )(h, p, *[w for w, _ in weights], ride)


def _ffn_row_kernel(h_ref, p_ref, gffn_ref, wg_ref, wu_ref, wd_ref, gple_ref, wpg_ref,
                    wpp_ref, out_ref, wg_bf_ref, wu_bf_ref, wd_bf_ref, xn_ref, acc_ref):
    j = pl.program_id(0)

    @pl.when(j == 0)
    def _():
        h = h_ref[...]
        xn_ref[...] = _rms(h, gffn_ref[...])
        acc_ref[...] = h

    for src, dst in ((wg_ref, wg_bf_ref), (wu_ref, wu_bf_ref), (wd_ref, wd_bf_ref)):
        dst[...] = src[...].astype(BF16)

    xn = xn_ref[...]
    a = jax.nn.silu(_dot_full(xn, wg_ref[...])) * _dot_full(xn, wu_ref[...])
    acc_ref[...] += _dot_full(a, wd_ref[...])

    @pl.when(j == pl.num_programs(0) - 1)
    def _():
        out_ref[...] = _ple(acc_ref[...], p_ref[...], gple_ref, wpg_ref, wpp_ref, _dot_full)


def _ffn_row(h, lw):
    n_dec = h.shape[0]
    d_ff = lw['w_ff_gate_f32'][0].shape[-1]
    ff = lw['w_ff_gate_f32'][1]
    return pl.pallas_call(
        _ffn_row_kernel,
        grid=(d_ff // FF_COLS,),
        in_specs=[_const_spec(h.shape), _layer_spec(*lw['p_row']), _layer_spec(*lw['g_ffn']),
                  pl.BlockSpec((None, D_MODEL, FF_COLS), lambda j: (ff, 0, j)),
                  pl.BlockSpec((None, D_MODEL, FF_COLS), lambda j: (ff, 0, j)),
                  pl.BlockSpec((None, FF_COLS, D_MODEL), lambda j: (ff, j, 0)),
                  _layer_spec(*lw['g_ple']), _layer_spec(*lw['w_ple_gate_f32']),
                  _layer_spec(*lw['w_ple_proj_f32'])],
        out_specs=[_whole_out_spec(h.shape),
                   pl.BlockSpec((D_MODEL, FF_COLS), lambda j: (0, j)),
                   pl.BlockSpec((D_MODEL, FF_COLS), lambda j: (0, j)),
                   pl.BlockSpec((FF_COLS, D_MODEL), lambda j: (j, 0))],
        out_shape=[jax.ShapeDtypeStruct(h.shape, F32),
                   jax.ShapeDtypeStruct((D_MODEL, d_ff), BF16),
                   jax.ShapeDtypeStruct((D_MODEL, d_ff), BF16),
                   jax.ShapeDtypeStruct((d_ff, D_MODEL), BF16)],
        scratch_shapes=[pltpu.VMEM((n_dec, D_MODEL), F32), pltpu.VMEM((n_dec, D_MODEL), F32)],
        compiler_params=_tc_params(),
        name="ffn_row",
    )(h, *[lw[k][0] for k in ('p_row', 'g_ffn', 'w_ff_gate_f32', 'w_ff_up_f32',
                              'w_ff_down_f32', 'g_ple', 'w_ple_gate_f32', 'w_ple_proj_f32')])


def _router_kernel(h_ref, gffn_ref, wr_ref, mi_ref, mf_ref, cnt_ref, carry_ref, logits_ref, *,
                   n_tok, split_from):
    g = pl.program_id(0)
    tm = h_ref.shape[0]

    @pl.when(g == 0)
    def _():
        carry_ref[...] = jnp.zeros(carry_ref.shape, F32)

    valid_row = (g * tm + lax.broadcasted_iota(I32, (tm, 1), 0)) < n_tok
    xn = _rms(jnp.where(valid_row, h_ref[...], 0.0), gffn_ref[...])

    @pl.when(g < split_from)
    def _():
        logits_ref[...] = _dot(xn, wr_ref[...].astype(BF16))

    @pl.when(g >= split_from)
    def _():
        logits_ref[...] = _dot_full(xn, wr_ref[...])

    logits = logits_ref[...].T[:N_EXPERTS]
    valid = (g * tm + lax.broadcasted_iota(I32, (1, tm), 1)) < n_tok
    e = lax.broadcasted_iota(I32, logits.shape, 0)
    m1 = jnp.max(logits, axis=0, keepdims=True)
    i1 = jnp.min(jnp.where(logits == m1, e, N_EXPERTS), axis=0, keepdims=True)
    rest = jnp.where(e == i1, -jnp.inf, logits)
    m2 = jnp.max(rest, axis=0, keepdims=True)
    i2 = jnp.min(jnp.where(rest == m2, e, N_EXPERTS), axis=0, keepdims=True)
    e2 = jnp.exp(m2 - m1)
    denom = 1.0 + e2
    w1 = 1.0 / denom
    w2 = e2 / denom

    oh1 = jnp.where((e == i1) & valid, 1.0, 0.0)
    oh2 = jnp.where((e == i2) & valid, 1.0, 0.0)
    member = oh1 + oh2
    r = lax.broadcasted_iota(I32, (tm, tm), 0)
    c = lax.broadcasted_iota(I32, (tm, tm), 1)
    earlier = jnp.where(r < c, 1.0, 0.0).astype(BF16)
    pos = _dot(member, earlier) + carry_ref[...]
    pos1 = jnp.sum(oh1 * pos, axis=0, keepdims=True).astype(I32)
    pos2 = jnp.sum(oh2 * pos, axis=0, keepdims=True).astype(I32)
    carry_ref[...] = carry_ref[...] + jnp.sum(member, axis=1, keepdims=True)
    cnt_ref[...] = carry_ref[...]

    mi_ref[...] = jnp.where(e == 0, i1, jnp.where(e == 1, i2,
                            jnp.where(e == 2, pos1, jnp.where(e == 3, pos2, 0))))
    gates = jnp.where(e == 0, w1, jnp.where(e == 1, w2, 0.0))
    lanes = wr_ref.shape[1]
    gates = jnp.concatenate([gates, jnp.zeros((lanes - N_EXPERTS, tm), F32)], axis=0)
    mf_ref[...] = gates.T[:, :N_EXPERTS]


def _router(h, tile0, n_tok, n_seq_rows, lw):
    tm = TOK_ROWS
    assert n_seq_rows % tm == 0
    return pl.pallas_call(
        functools.partial(_router_kernel, n_tok=n_tok, split_from=n_seq_rows // tm - tile0),
        grid=(pl.cdiv(n_tok, tm),),
        in_specs=[pl.BlockSpec((tm, D_MODEL), lambda g: (tile0 + g, 0)),
                  _layer_spec(*lw['g_ffn']), _layer_spec(*lw['w_router'])],
        out_specs=[pl.BlockSpec((N_EXPERTS, tm), lambda g: (0, g)),
                   pl.BlockSpec((tm, N_EXPERTS), lambda g: (g, 0)),
                   _whole_out_spec((N_EXPERTS, 1))],
        out_shape=[jax.ShapeDtypeStruct((N_EXPERTS, n_tok), I32),
                   jax.ShapeDtypeStruct((n_tok, N_EXPERTS), F32),
                   jax.ShapeDtypeStruct((N_EXPERTS, 1), F32)],
        scratch_shapes=[pltpu.VMEM((N_EXPERTS, 1), F32), pltpu.VMEM((tm, LANES), F32)],
        compiler_params=_tc_params(),
        name="router",
    )(h, lw['g_ffn'][0], lw['w_router'][0])


def _sc_chunk(n_rows):
    per_worker = pl.cdiv(n_rows, SC_WORKERS)
    return pl.cdiv(per_worker, SC_WINDOW) * SC_WINDOW


def _sc_worker_base(n_rows, chunk):
    wid = lax.axis_index("s") * SC_CORES + lax.axis_index("c")
    return jnp.minimum(wid * chunk, n_rows - chunk)


def _sc_scatter_rows(x, row0, dest, n_out):
    n = dest.shape[0] // TOP_K
    chunk = _sc_chunk(n)
    mesh = plsc.VectorSubcoreMesh(core_axis_name="c", subcore_axis_name="s")

    @functools.partial(
        pl.kernel, mesh=mesh,
        out_type=jax.ShapeDtypeStruct((n_out, D_MODEL), x.dtype),
        scratch_types=[pltpu.VMEM((SC_WINDOW,), I32) for _ in range(TOP_K)]
        + [pltpu.VMEM((SC_WINDOW, D_MODEL), x.dtype), pltpu.SemaphoreType.DMA],
        name="sc_scatter_rows",
    )
    def scatter(x_hbm, dest_hbm, out_hbm, idx0_v, idx1_v, rows_v, sem):
        base = _sc_worker_base(n, chunk)

        @pl.loop(0, chunk // SC_WINDOW)
        def _(j):
            off = pl.multiple_of(base + j * SC_WINDOW, 8)
            pltpu.sync_copy(dest_hbm.at[pl.ds(off, SC_WINDOW)], idx0_v)
            pltpu.sync_copy(dest_hbm.at[pl.ds(n + off, SC_WINDOW)], idx1_v)
            pltpu.sync_copy(x_hbm.at[pl.ds(row0 + off, SC_WINDOW)], rows_v)
            first = pltpu.async_copy(rows_v, out_hbm.at[idx0_v], sem)
            second = pltpu.async_copy(rows_v, out_hbm.at[idx1_v], sem)
            first.wait()
            second.wait()

    return scatter(x, dest)


def _sc_gather_rows(y, idx):
    n = idx.shape[0]
    width = y.shape[1]
    chunk = _sc_chunk(n)
    mesh = plsc.VectorSubcoreMesh(core_axis_name="c", subcore_axis_name="s")

    @functools.partial(
        pl.kernel, mesh=mesh,
        out_type=jax.ShapeDtypeStruct((n, width), y.dtype),
        scratch_types=[pltpu.VMEM((SC_WINDOW,), I32),
                       pltpu.VMEM((SC_WINDOW, width), y.dtype), pltpu.SemaphoreType.DMA],
        name="sc_gather_rows",
    )
    def gather(y_hbm, idx_hbm, out_hbm, idx_v, rows_v, sem):
        base = _sc_worker_base(n, chunk)

        @pl.loop(0, chunk // SC_WINDOW)
        def _(j):
            off = pl.multiple_of(base + j * SC_WINDOW, 8)
            pltpu.sync_copy(idx_hbm.at[pl.ds(off, SC_WINDOW)], idx_v)
            pltpu.async_copy(y_hbm.at[idx_v], rows_v, sem).wait()
            pltpu.sync_copy(rows_v, out_hbm.at[pl.ds(off, SC_WINDOW)])

    return gather(y, idx)


def _pack_bf16_halves(y):
    half = y.shape[1] // 2
    rounded = y.astype(BF16).astype(F32)
    hi = lax.bitcast_convert_type(rounded[:, :half], jnp.uint32)
    lo = lax.bitcast_convert_type(rounded[:, half:], jnp.uint32)
    return hi | (lo >> 16)


def _unpack_bf16_halves(packed):
    hi = lax.bitcast_convert_type(packed & jnp.uint32(0xFFFF0000), F32)
    lo = lax.bitcast_convert_type(packed << 16, F32)
    return jnp.concatenate([hi, lo], axis=1)


def _expert_kernel(tile_expert_ref, n_valid_ref, xs_ref, gffn_ref, wg_ref, wu_ref, wd_ref,
                   y_ref):
    @pl.when(pl.program_id(0) < n_valid_ref[0])
    def _():
        xn = _rms(xs_ref[...], gffn_ref[...]).astype(BF16)
        a = jax.nn.silu(_dot(xn, wg_ref[...])) * _dot(xn, wu_ref[...])
        y_ref[...] = _pack_bf16_halves(_dot(a, wd_ref[...]))


def _experts(xs, tile_expert, n_valid, lw):
    n_slots = xs.shape[0]
    te = EXPERT_ROWS
    d_exp = lw['w_ex_gate'][0].shape[-1]
    moe = lw['w_ex_gate'][1]
    row_block = lambda g, tex, nv: (jnp.minimum(g, nv[0] - 1), 0)
    w_block = lambda g, tex, nv: (moe, tex[g], 0, 0)
    return pl.pallas_call(
        _expert_kernel,
        grid_spec=pltpu.PrefetchScalarGridSpec(
            num_scalar_prefetch=2,
            grid=(n_slots // te,),
            in_specs=[pl.BlockSpec((te, D_MODEL), row_block),
                      _layer_spec(*lw['g_ffn']),
                      pl.BlockSpec((None, None, D_MODEL, d_exp), w_block),
                      pl.BlockSpec((None, None, D_MODEL, d_exp), w_block),
                      pl.BlockSpec((None, None, d_exp, D_MODEL), w_block)],
            out_specs=pl.BlockSpec((te, D_MODEL // 2), row_block)),
        out_shape=jax.ShapeDtypeStruct((n_slots, D_MODEL // 2), jnp.uint32),
        compiler_params=_tc_params(),
        name="experts",
    )(tile_expert, n_valid, xs, *[lw[k][0] for k in ('g_ffn', 'w_ex_gate', 'w_ex_up',
                                                        'w_ex_down')])


def _combine_math(h_ref, yg_ref, mf_ref, p, gple_ref, wpg_ref, wpp_ref, gfin_ref):
    gates = mf_ref[...]
    h = h_ref[...] + (gates[:, 0:1] * _unpack_bf16_halves(yg_ref[0])
                      + gates[:, 1:2] * _unpack_bf16_halves(yg_ref[1]))
    return _rms(_ple(h, p, gple_ref, wpg_ref, wpp_ref, _dot), gfin_ref[...])


def _combine_seq_kernel(h_ref, yg_ref, mf_ref, pp_ref, gple_ref, wpg_ref, wpp_ref, gfin_ref,
                        ybuf_ref, yp_ref):
    del ybuf_ref
    yp_ref[...] = _combine_math(h_ref, yg_ref, mf_ref, pp_ref[...], gple_ref, wpg_ref,
                                wpp_ref, gfin_ref)


def _combine_tail_kernel(h_ref, yg_ref, mf_ref, pp_ref, ps_ref, gple_ref, wpg_ref, wpp_ref,
                         gfin_ref, ybuf_ref, yp_ref, ys_ref, pbuf_ref):
    del ybuf_ref
    g = pl.program_id(0)
    last = pl.num_programs(0) - 1
    n_dec = ps_ref.shape[0]
    pbuf_ref[...] = pp_ref[...]

    @pl.when(g == last)
    def _():
        pbuf_ref[0:n_dec, :] = ps_ref[...]

    out = _combine_math(h_ref, yg_ref, mf_ref, pbuf_ref[...], gple_ref, wpg_ref, wpp_ref,
                        gfin_ref)

    @pl.when(g < last)
    def _():
        yp_ref[...] = out

    @pl.when(g == last)
    def _():
        ys_ref[...] = out[0:n_dec, :]


def _combine(h, tile0, yg, mf, y_seq, lw, with_rows):
    tm = COMBINE_ROWS
    assert tile0 * TOK_ROWS % tm == 0
    tile0 = tile0 * TOK_ROWS // tm
    n_tiles = pl.cdiv(yg.shape[1], tm)
    last_seq_tile = y_seq.shape[0] // tm - 1
    pp, layer = lw['p_seq']
    n_dec = lw['p_row'][0].shape[-2]
    weights = [lw[k] for k in (('p_row',) if with_rows else ())
               + ('g_ple', 'w_ple_gate', 'w_ple_proj', 'g_final')]
    seq_tile = lambda g: (jnp.minimum(tile0 + g, last_seq_tile), 0)
    n_in = 4 + len(weights)
    out = pl.pallas_call(
        _combine_tail_kernel if with_rows else _combine_seq_kernel,
        grid=(n_tiles,),
        in_specs=[pl.BlockSpec((tm, D_MODEL), lambda g: (tile0 + g, 0)),
                  pl.BlockSpec((TOP_K, tm, yg.shape[2]), lambda g: (0, g, 0)),
                  pl.BlockSpec((tm, N_EXPERTS), lambda g: (g, 0)),
                  pl.BlockSpec((None, tm, D_PLE), lambda g: (layer,) + seq_tile(g))]
        + [_layer_spec(*w) for w in weights]
        + [pl.BlockSpec(memory_space=pl.ANY)],
        out_specs=[pl.BlockSpec((tm, D_MODEL), seq_tile)]
        + ([_whole_out_spec((n_dec, D_MODEL))] if with_rows else []),
        out_shape=[jax.ShapeDtypeStruct(y_seq.shape, F32)]
        + ([jax.ShapeDtypeStruct((n_dec, D_MODEL), F32)] if with_rows else []),
        scratch_shapes=[pltpu.VMEM((tm, D_PLE), F32)] if with_rows else [],
        input_output_aliases={n_in: 0},
        compiler_params=_tc_params(),
        name="combine",
    )(h, yg, mf, pp, *[w for w, _ in weights], y_seq)
    return out if with_rows else (out[0], None)


def _moe_layer(h, y_seq, lw):
    n_tok = h.shape[0]
    tm = TOK_ROWS
    te = EXPERT_ROWS
    tiles = pl.cdiv(n_tok, tm)
    bounds = [tiles * c // MOE_CHUNKS for c in range(MOE_CHUNKS + 1)]
    y_rows = None
    for c in range(MOE_CHUNKS):
        tile0 = bounds[c]
        n = min(bounds[c + 1] * tm, n_tok) - tile0 * tm
        mi, mf, counts = _router(h, tile0, n, y_seq.shape[0], lw)

        cnt = counts[:, 0].astype(I32)
        padded = (cnt + te - 1) // te * te
        ends = jnp.cumsum(padded)
        starts = ends - padded
        experts = jnp.arange(N_EXPERTS, dtype=I32)
        start_of = lambda e: jnp.sum(
            jnp.where(e[None, :] == experts[:, None], starts[:, None], 0), axis=0)
        dest = jnp.concatenate([start_of(mi[0]) + mi[2], start_of(mi[1]) + mi[3]])
        dest, h = lax.optimization_barrier((dest, h))
        n_tiles = pl.cdiv(TOP_K * n + N_EXPERTS * (te - 1), te)
        tile_start = jnp.arange(n_tiles, dtype=I32) * te
        last_used = jnp.max(jnp.where(padded > 0, experts, 0))
        tile_expert = jnp.minimum(
            jnp.sum(tile_start[:, None] >= ends[None, :], axis=-1).astype(I32), last_used)
        n_valid = (ends[-1:] // te).astype(I32)

        xs = _sc_scatter_rows(h, tile0 * tm, dest, n_tiles * te)
        y = _experts(xs, tile_expert, n_valid, lw)
        yg = _sc_gather_rows(y, dest).reshape(TOP_K, n, y.shape[1])
        y_seq, rows = _combine(h, tile0, yg, mf, y_seq, lw, with_rows=c == MOE_CHUNKS - 1)
        y_rows = rows if rows is not None else y_rows
    return y_seq, y_rows


def kernel(x_prompt, x_sample, state_conv_a, state_conv_c, p_prompt, p_sample, g_mix, w_in, w_conv_a, w_s, b_s, g_ln_b, b_ln_b, w_conv_c, b_conv_c, g_ln_c, b_ln_c, g_out, w_o, g_ffn, w_ff_gate, w_ff_up, w_ff_down, w_router, w_ex_gate, w_ex_up, w_ex_down, g_ple, w_ple_gate, w_ple_proj, g_final):
    depth = g_mix.shape[0]
    n_seq, seq, _ = x_prompt.shape
    n_dec = x_sample.shape[0]
    n_prompt = n_seq * seq
    assert depth == 2 and x_sample.shape[1] == 1
    assert seq % MIXER_ROWS == 0 and TOK_ROWS % n_dec == 0 and w_ff_gate.shape[-1] % FF_COLS == 0

    vectors = {'g_mix': g_mix, 'g_out': g_out, 'g_ffn': g_ffn, 'g_ple': g_ple,
               'b_conv_c': b_conv_c, 'g_ln_c': g_ln_c, 'b_ln_c': b_ln_c,
               'g_ln_b': g_ln_b, 'b_ln_b': b_ln_b}
    pieces, lanes, total = [], {}, 0
    for name, x in sorted(vectors.items(), key=lambda kv: -kv[1].shape[1]):
        n = x.shape[1]
        gap = -total % n
        pieces += [jnp.zeros((depth, gap), F32)] * (gap > 0) + [x]
        lanes[name] = (total + gap, n)
        total += gap + n
    packed = jnp.concatenate(pieces, axis=1).reshape(depth, 1, total)
    per_layer = {
        'w_in_f32': w_in, 'w_conv_a': w_conv_a,
        'w_s_cat': jnp.transpose(w_s, (0, 2, 1, 3)).reshape(depth, CHUNK, N_HEADS_B * CHUNK),
        'b_s_full': jnp.repeat(jnp.swapaxes(b_s, 1, 2), HEAD_DIM, axis=2),
        'w_conv_c': w_conv_c, 'w_o_f32': w_o,
        'w_ple_gate_f32': w_ple_gate, 'w_ple_proj_f32': w_ple_proj,
        'state_a': jnp.swapaxes(state_conv_a, 1, 2), 'state_c': jnp.swapaxes(state_conv_c, 1, 2),
        'p_seq': p_prompt.reshape(depth, n_prompt, D_PLE),
        'p_row': p_sample.reshape(depth, n_dec, D_PLE),
    }
    per_dense = {'w_ff_gate_f32': w_ff_gate, 'w_ff_up_f32': w_ff_up, 'w_ff_down_f32': w_ff_down}
    one_layer = lambda w: (w.reshape((1,) + w.shape), 0)
    per_moe = {'w_router': jnp.pad(w_router, ((0, 0), (0, 0), (0, LANES - N_EXPERTS)))}
    to_convert = [('w_ex_gate', w_ex_gate), ('w_ex_up', w_ex_up), ('w_ex_down', w_ex_down)]

    def with_ride(call):
        name, w = to_convert.pop(0)
        *outputs, w_bf16 = call(w.reshape(-1, w.shape[-1]))
        per_moe[name] = w_bf16.reshape(w.shape)
        return outputs

    hp = x_prompt.reshape(n_prompt, D_MODEL)
    hs = x_sample.reshape(n_dec, D_MODEL)
    outs = {'a_p': [], 'c_p': []}
    row_states = None
    for i in range(depth):
        is_expert_layer = i % 2 == 1
        lw = {k: (v, i) for k, v in per_layer.items()}
        lw.update({k: (packed, _Lanes(i, off, n)) for k, (off, n) in lanes.items()})
        lw.update({k: (v, i // 2) for k, v in (per_dense if not is_expert_layer else {}).items()})
        lw['g_final'] = (g_final.reshape(1, 1, -1), 0)

        hs, *row_states, w0, w1, w2, w3 = _mixer_row(hs, 0, lw, row_states)
        lw.update({k: one_layer(w) for k, w in zip(CONVERTED_BY_MIXER_ROW, (w0, w1, w2, w3))})
        mixer_in = hp
        hp, a_p, c_p = with_ride(functools.partial(
            _mixer_seq, hp, hs if is_expert_layer else None, lw, n_seq, seq))
        outs['a_p'].append(a_p)
        outs['c_p'].append(c_p)
        if not is_expert_layer:
            hs, *converted = _ffn_row(hs, lw)
            lw.update({k: one_layer(w) for k, w in
                       zip(('w_ff_gate', 'w_ff_up', 'w_ff_down'), converted)})
            hp, = with_ride(functools.partial(_ffn_dense, hp, lw))
        else:
            lw.update({k: (v, i // 2) for k, v in per_moe.items()})
            y_prompt, y_sample = _moe_layer(hp, mixer_in, lw)

    a_s, c_s, v_s = row_states
    return (y_prompt.reshape(x_prompt.shape), y_sample.reshape(x_sample.shape),
            jnp.stack(outs['a_p']), a_s, jnp.stack(outs['c_p']), c_s,
            v_s.reshape(depth, n_dec, 1, D_B))
```

```python
import collections
import functools

import jax
import jax.numpy as jnp
from jax import lax
from jax.experimental import pallas as pl
from jax.experimental.pallas import tpu as pltpu
from jax.experimental.pallas import tpu_sc as plsc

D_MODEL = 1024
HEAD_DIM = 64
D_A = 384
D_B = 256
D_C = 384
N_HEADS_B = D_B // HEAD_DIM
CONV_A = 3
CONV_C = 31
CHUNK = 128
D_IN = 3 * D_A + 2 * D_B + 2 * D_C
D_PLE = 256
N_EXPERTS = 8
TOP_K = 2
EPS = 1e-6

O1, O2, O3 = D_A, 2 * D_A, 3 * D_A
O4 = O3 + 2 * D_B

F32 = jnp.float32
BF16 = jnp.bfloat16
I32 = jnp.int32

VMEM_LIMIT_BYTES = 56 * 1024 * 1024
SUBLANES = 8
LANES = 128

TOK_ROWS = 512
MIXER_ROWS = 1024
COMBINE_ROWS = 1024
CONV_ROWS = 64
A_HALO = 8
C_HALO = 32
EXPERT_ROWS = 512
FF_COLS = 256
MOE_CHUNKS = 2

SC_CORES = 2
SC_WORKERS = 32
SC_WINDOW = 48


def _rms(x, g):
    return x * lax.rsqrt(jnp.mean(x * x, axis=-1, keepdims=True) + EPS) * g


def _ln(x, g, b):
    mu = jnp.mean(x, axis=-1, keepdims=True)
    xc = x - mu
    var = jnp.mean(xc * xc, axis=-1, keepdims=True)
    return xc * lax.rsqrt(var + EPS) * g + b


def _dot(a, b):
    return jnp.dot(a.astype(BF16), b, preferred_element_type=F32)


def _split(x):
    hi = x.astype(BF16)
    return hi, (x - hi.astype(F32)).astype(BF16)


def _dot_full(a, b):
    a_hi, a_lo = _split(a)
    b_hi, b_lo = _split(b)
    m = a.shape[0]
    by_hi = jnp.dot(jnp.concatenate([a_hi, a_lo], axis=0), b_hi, preferred_element_type=F32)
    return by_hi[:m] + by_hi[m:] + jnp.dot(a_hi, b_lo, preferred_element_type=F32)


def _const_spec(shape):
    return pl.BlockSpec(shape, lambda *_: (0,) * len(shape), pipeline_mode=pl.Buffered(1))


_Lanes = collections.namedtuple('_Lanes', 'layer offset size')


def _layer_spec(stacked, layer):
    if isinstance(layer, _Lanes):
        where = (layer.layer, 0, layer.offset // layer.size)
        return pl.BlockSpec((None, 1, layer.size), lambda *_: where,
                            pipeline_mode=pl.Buffered(1))
    idx = (layer,) if isinstance(layer, int) else tuple(layer)
    rest = stacked.shape[len(idx):]
    return pl.BlockSpec((None,) * len(idx) + rest, lambda *_: idx + (0,) * len(rest),
                        pipeline_mode=pl.Buffered(1))


def _whole_out_spec(shape):
    return pl.BlockSpec(shape, lambda *_: (0,) * len(shape))


def _tc_params():
    return pltpu.CompilerParams(dimension_semantics=("arbitrary",),
                                vmem_limit_bytes=VMEM_LIMIT_BYTES)


def _normed_groups(y_a, y_b, y_c, gout_ref):
    return jnp.concatenate([_rms(y_a, gout_ref[:, :D_A]),
                            _rms(y_b, gout_ref[:, D_A:D_A + D_B]),
                            _rms(y_c, gout_ref[:, D_A + D_B:])], axis=-1)


def _mixer_seq_kernel(hp_ref, hs_ref, gmix_ref, win_ref, wca_ref, wscat_ref, bsfull_ref,
                      glnb_ref, blnb_ref, wcc_ref, bcc_ref, glnc_ref, blnc_ref, gout_ref,
                      wo_ref, ride_ref, hout_ref, newa_ref, newc_ref, ride_out_ref,
                      qext_ref, gext_ref, gshift_ref, conv_ref, wcc8_ref, *, tiles_per_seq,
                      n_seq_tiles):
    g = pl.program_id(0)
    tm = hp_ref.shape[0]

    @pl.when(g == 0)
    def _():
        for k in range(CONV_C):
            wcc8_ref[k] = jnp.broadcast_to(wcc_ref[k:k + 1, :], (SUBLANES, D_C))

    @pl.when(g < n_seq_tiles)
    def _sequence_tile():
        ride_out_ref[...] = ride_ref[...].astype(BF16)

        @pl.when(g % tiles_per_seq == 0)
        def _():
            qext_ref[0:A_HALO, :] = jnp.zeros((A_HALO, D_A), F32)
            gext_ref[0:C_HALO, :] = jnp.zeros((C_HALO, D_C), F32)

        h = hp_ref[...]
        z = _dot(_rms(h, gmix_ref[...]), win_ref[...])

        qext_ref[A_HALO:A_HALO + tm, :] = z[:, O1:O2] * z[:, O2:O3]
        conv_a = jnp.zeros((tm, D_A), F32)
        for k in range(CONV_A):
            off = A_HALO - (CONV_A - 1) + k
            conv_a = conv_a + wca_ref[k:k + 1, :] * qext_ref[off:off + tm, :]
        y_a = z[:, :O1] * conv_a
        last_q = qext_ref[A_HALO + tm - (CONV_A - 1):A_HALO + tm, :]
        newa_ref[...] = last_q
        qext_ref[A_HALO - (CONV_A - 1):A_HALO, :] = last_q

        zb = jax.nn.gelu(z[:, O3:O4])
        v = _ln(zb[:, D_B:], glnb_ref[...], blnb_ref[...])
        row = lax.broadcasted_iota(I32, (CHUNK, N_HEADS_B * CHUNK), 0)
        col = lax.broadcasted_iota(I32, (CHUNK, N_HEADS_B * CHUNK), 1)
        w_tril = jnp.where((col % CHUNK) <= row, wscat_ref[...], 0.0).astype(BF16)
        lane_head = lax.broadcasted_iota(I32, (CHUNK, D_B), 1) // HEAD_DIM
        s_chunks = []
        for c in range(tm // CHUNK):
            vc = v[c * CHUNK:(c + 1) * CHUNK, :]
            vstack = jnp.concatenate(
                [jnp.where(lane_head == hd, vc, 0.0) for hd in range(N_HEADS_B)], axis=0)
            s_chunks.append(_dot(w_tril, vstack.astype(BF16)) + bsfull_ref[...])
        y_b = zb[:, :D_B] * jnp.concatenate(s_chunks, axis=0)

        gext_ref[C_HALO:C_HALO + tm, :] = (z[:, O4:O4 + D_C]
                                           * jax.nn.sigmoid(z[:, O4 + D_C:]))
        n_shift = gshift_ref.shape[1]
        for s in range(1, SUBLANES):
            gshift_ref[s - 1] = gext_ref[s:s + n_shift, :]
        base = C_HALO - (CONV_C - 1)
        for r0 in range(0, tm, CONV_ROWS):
            acc = jnp.zeros((CONV_ROWS // SUBLANES, SUBLANES, D_C), F32)
            for k in range(CONV_C):
                lo = (base + k) // SUBLANES * SUBLANES + r0
                s = (base + k) % SUBLANES
                window = (gext_ref[lo:lo + CONV_ROWS, :] if s == 0
                          else gshift_ref[s - 1, lo:lo + CONV_ROWS, :])
                acc = acc + wcc8_ref[k][None] * window.reshape(acc.shape)
            conv_ref[r0:r0 + CONV_ROWS, :] = acc.reshape(CONV_ROWS, D_C)
        y_c = jax.nn.silu(_ln(conv_ref[...] + bcc_ref[...], glnc_ref[...], blnc_ref[...]))
        last_g = gext_ref[C_HALO + tm - (CONV_C - 1):C_HALO + tm, :]
        newc_ref[...] = last_g
        gext_ref[C_HALO - (CONV_C - 1):C_HALO, :] = last_g

        hout_ref[...] = h + _dot(_normed_groups(y_a, y_b, y_c, gout_ref), wo_ref[...])

    @pl.when(g == n_seq_tiles)
    def _append_sample_rows():
        hout_ref[0:hs_ref.shape[0], :] = hs_ref[...]


def _ride_specs(ride, n_steps):
    rows = ride.shape[0] // n_steps
    assert rows * n_steps == ride.shape[0]
    slab = lambda g: (jnp.minimum(g, n_steps - 1), 0)
    return (pl.BlockSpec((rows, ride.shape[1]), slab), pl.BlockSpec((rows, ride.shape[1]), slab),
            jax.ShapeDtypeStruct(ride.shape, BF16))


def _mixer_seq(hp, hs, lw, n_seq, seq, ride):
    tm = MIXER_ROWS
    tiles_per_seq = seq // tm
    n_seq_tiles = n_seq * tiles_per_seq
    join = hs is not None
    if not join:
        hs = jnp.zeros((SUBLANES, D_MODEL), F32)
    n_out = n_seq * seq + (hs.shape[0] if join else 0)
    weights = [lw[k] for k in ('g_mix', 'w_in', 'w_conv_a', 'w_s_cat', 'b_s_full', 'g_ln_b',
                               'b_ln_b', 'w_conv_c', 'b_conv_c', 'g_ln_c', 'b_ln_c', 'g_out',
                               'w_o')]
    seq_of = lambda g: jnp.minimum(g // tiles_per_seq, n_seq - 1)
    n_shift = tm + C_HALO - SUBLANES
    ride_in_spec, ride_out_spec, ride_out_shape = _ride_specs(ride, n_seq_tiles)
    return pl.pallas_call(
        functools.partial(_mixer_seq_kernel, tiles_per_seq=tiles_per_seq,
                          n_seq_tiles=n_seq_tiles),
        grid=(n_seq_tiles + int(join),),
        in_specs=[pl.BlockSpec((tm, D_MODEL), lambda g: (jnp.minimum(g, n_seq_tiles - 1), 0)),
                  _const_spec(hs.shape)]
        + [_layer_spec(*w) for w in weights] + [ride_in_spec],
        out_specs=[pl.BlockSpec((tm, D_MODEL), lambda g: (g, 0)),
                   pl.BlockSpec((None, CONV_A - 1, D_A), lambda g: (seq_of(g), 0, 0)),
                   pl.BlockSpec((None, CONV_C - 1, D_C), lambda g: (seq_of(g), 0, 0)),
                   ride_out_spec],
        out_shape=[jax.ShapeDtypeStruct((n_out, D_MODEL), F32),
                   jax.ShapeDtypeStruct((n_seq, CONV_A - 1, D_A), F32),
                   jax.ShapeDtypeStruct((n_seq, CONV_C - 1, D_C), F32),
                   ride_out_shape],
        scratch_shapes=[pltpu.VMEM((A_HALO + tm, D_A), F32),
                        pltpu.VMEM((C_HALO + tm, D_C), F32),
                        pltpu.VMEM((SUBLANES - 1, n_shift, D_C), F32),
                        pltpu.VMEM((tm, D_C), F32),
                        pltpu.VMEM((CONV_C, SUBLANES, D_C), F32)],
        compiler_params=_tc_params(),
        name="mixer_seq",
    )(hp, hs, *[w for w, _ in weights], ride)


def _mixer_row_kernel(*refs, n_carried):
    (h_ref, sa_ref, sc_ref, gmix_ref, win_ref, wca_ref, wscat_ref, bsfull_ref, glnb_ref,
     blnb_ref, wcc_ref, bcc_ref, glnc_ref, blnc_ref, gout_ref, wo_ref, wpg_ref,
     wpp_ref) = refs[:18]
    (hout_ref, newa_ref, newc_ref, v_ref, win_bf_ref, wo_bf_ref, wpg_bf_ref,
     wpp_bf_ref) = refs[18 + n_carried:]

    @pl.when(pl.program_id(0) > 0)
    def _():
        for ref in (newa_ref, newc_ref, v_ref):
            ref[...] = jnp.zeros(ref.shape, F32)

    @pl.when(pl.program_id(0) == 0)
    def _():
        _mixer_row_body(h_ref, sa_ref, sc_ref, gmix_ref, win_ref, wca_ref, wscat_ref,
                        bsfull_ref, glnb_ref, blnb_ref, wcc_ref, bcc_ref, glnc_ref, blnc_ref,
                        gout_ref, wo_ref, wpg_ref, wpp_ref, hout_ref, newa_ref, newc_ref,
                        v_ref, win_bf_ref, wo_bf_ref, wpg_bf_ref, wpp_bf_ref)


def _mixer_row_body(h_ref, sa_ref, sc_ref, gmix_ref, win_ref, wca_ref, wscat_ref,
                    bsfull_ref, glnb_ref, blnb_ref, wcc_ref, bcc_ref, glnc_ref, blnc_ref,
                    gout_ref, wo_ref, wpg_ref, wpp_ref, hout_ref, newa_ref, newc_ref, v_ref,
                    win_bf_ref, wo_bf_ref, wpg_bf_ref, wpp_bf_ref):
    for src, dst in ((win_ref, win_bf_ref), (wo_ref, wo_bf_ref), (wpg_ref, wpg_bf_ref),
                     (wpp_ref, wpp_bf_ref)):
        dst[...] = src[...].astype(BF16)

    h = h_ref[...]
    z = _dot_full(_rms(h, gmix_ref[...]), win_ref[...])

    q = z[:, O1:O2] * z[:, O2:O3]
    conv_a = wca_ref[CONV_A - 1:CONV_A, :] * q
    for k in range(CONV_A - 1):
        conv_a = conv_a + wca_ref[k:k + 1, :] * sa_ref[k]
    y_a = z[:, :O1] * conv_a
    for k in range(CONV_A - 2):
        newa_ref[:, k, :] = sa_ref[k + 1]
    newa_ref[:, CONV_A - 2, :] = q

    zb = jax.nn.gelu(z[:, O3:O4])
    v = _ln(zb[:, D_B:], glnb_ref[...], blnb_ref[...])
    v_ref[...] = v
    w_diag0 = jnp.concatenate(
        [jnp.broadcast_to(wscat_ref[0:1, hd * CHUNK:hd * CHUNK + 1], (1, HEAD_DIM))
         for hd in range(N_HEADS_B)], axis=-1)
    y_b = zb[:, :D_B] * (w_diag0 * v + bsfull_ref[0:1, :])

    glu = z[:, O4:O4 + D_C] * jax.nn.sigmoid(z[:, O4 + D_C:])
    conv_c = wcc_ref[CONV_C - 1:CONV_C, :] * glu
    for k in range(CONV_C - 1):
        conv_c = conv_c + wcc_ref[k:k + 1, :] * sc_ref[k]
    y_c = jax.nn.silu(_ln(conv_c + bcc_ref[...], glnc_ref[...], blnc_ref[...]))
    for k in range(CONV_C - 2):
        newc_ref[:, k, :] = sc_ref[k + 1]
    newc_ref[:, CONV_C - 2, :] = glu

    hout_ref[...] = h + _dot_full(_normed_groups(y_a, y_b, y_c, gout_ref), wo_ref[...])


CONVERTED_BY_MIXER_ROW = ('w_in', 'w_o', 'w_ple_gate', 'w_ple_proj')


def _mixer_row(h, h_block, lw, states):
    depth, _, n_dec, _ = lw['state_a'][0].shape
    layer = lw['state_a'][1]
    weights = [lw[k] for k in ('state_a', 'state_c', 'g_mix', 'w_in_f32', 'w_conv_a', 'w_s_cat',
                               'b_s_full', 'g_ln_b', 'b_ln_b', 'w_conv_c', 'b_conv_c', 'g_ln_c',
                               'b_ln_c', 'g_out', 'w_o_f32', 'w_ple_gate_f32', 'w_ple_proj_f32')]
    bf16_shapes = [lw[k + '_f32'][0].shape[1:] for k in CONVERTED_BY_MIXER_ROW]
    state_shapes = [(n_dec, CONV_A - 1, D_A), (n_dec, CONV_C - 1, D_C), (n_dec, D_B)]
    carried = list(states or ())
    first = 1 + len(weights)
    block_of = (lambda g: g) if states is None else (lambda g: layer)
    return pl.pallas_call(
        functools.partial(_mixer_row_kernel, n_carried=len(carried)),
        grid=(depth if states is None else 1,),
        in_specs=[pl.BlockSpec((n_dec, D_MODEL), lambda g: (h_block, 0))]
        + [_layer_spec(*w) for w in weights]
        + [pl.BlockSpec(memory_space=pl.ANY) for _ in carried],
        out_specs=[_whole_out_spec((n_dec, D_MODEL))]
        + [pl.BlockSpec((None,) + s, lambda g, s=s: (block_of(g),) + (0,) * len(s))
           for s in state_shapes]
        + [_whole_out_spec(s) for s in bf16_shapes],
        out_shape=[jax.ShapeDtypeStruct((n_dec, D_MODEL), F32)]
        + [jax.ShapeDtypeStruct((depth,) + s, F32) for s in state_shapes]
        + [jax.ShapeDtypeStruct(s, BF16) for s in bf16_shapes],
        input_output_aliases={first + j: 1 + j for j in range(len(carried))},
        compiler_params=_tc_params(),
        name="mixer_row",
    )(h, *[w for w, _ in weights], *carried)


def _ple(h, p, gple_ref, wpg_ref, wpp_ref, dot):
    gate = jax.nn.sigmoid(dot(_rms(h, gple_ref[...]), wpg_ref[...]))
    return h + gate * dot(p, wpp_ref[...])


def _ffn_dense_kernel(h_ref, p_ref, gffn_ref, wg_ref, wu_ref, wd_ref, gple_ref, wpg_ref,
                      wpp_ref, ride_ref, out_ref, ride_out_ref):
    ride_out_ref[...] = ride_ref[...].astype(BF16)
    h = h_ref[...]
    xn = _rms(h, gffn_ref[...]).astype(BF16)
    a = jax.nn.silu(_dot(xn, wg_ref[...])) * _dot(xn, wu_ref[...])
    h = h + _dot(a, wd_ref[...])
    out_ref[...] = _ple(h, p_ref[...], gple_ref, wpg_ref, wpp_ref, _dot)


def _ffn_dense(h, lw, ride):
    n_tok = h.shape[0]
    tm = TOK_ROWS
    p, layer = lw['p_seq']
    weights = [lw[k] for k in ('g_ffn', 'w_ff_gate', 'w_ff_up', 'w_ff_down', 'g_ple',
                               'w_ple_gate', 'w_ple_proj')]
    ride_in_spec, ride_out_spec, ride_out_shape = _ride_specs(ride, n_tok // tm)
    return pl.pallas_call(
        _ffn_dense_kernel,
        grid=(n_tok // tm,),
        in_specs=[pl.BlockSpec((tm, D_MODEL), lambda g: (g, 0)),
                  pl.BlockSpec((None, tm, D_PLE), lambda g: (layer, g, 0))]
        + [_layer_spec(*w) for w in weights] + [ride_in_spec],
        out_specs=[pl.BlockSpec((tm, D_MODEL), lambda g: (g, 0)), ride_out_spec],
        out_shape=[jax.ShapeDtypeStruct(h.shape, F32), ride_out_shape],
        compiler_params=_tc_params(),
        name="ffn_dense",
    )(h, p, *[w for w, _ in weights], ride)


def _ffn_row_kernel(h_ref, p_ref, gffn_ref, wg_ref, wu_ref, wd_ref, gple_ref, wpg_ref,
                    wpp_ref, out_ref, wg_bf_ref, wu_bf_ref, wd_bf_ref, xn_ref, acc_ref):
    j = pl.program_id(0)

    @pl.when(j == 0)
    def _():
        h = h_ref[...]
        xn_ref[...] = _rms(h, gffn_ref[...])
        acc_ref[...] = h

    for src, dst in ((wg_ref, wg_bf_ref), (wu_ref, wu_bf_ref), (wd_ref, wd_bf_ref)):
        dst[...] = src[...].astype(BF16)

    xn = xn_ref[...]
    a = jax.nn.silu(_dot_full(xn, wg_ref[...])) * _dot_full(xn, wu_ref[...])
    acc_ref[...] += _dot_full(a, wd_ref[...])

    @pl.when(j == pl.num_programs(0) - 1)
    def _():
        out_ref[...] = _ple(acc_ref[...], p_ref[...], gple_ref, wpg_ref, wpp_ref, _dot_full)


def _ffn_row(h, lw):
    n_dec = h.shape[0]
    d_ff = lw['w_ff_gate_f32'][0].shape[-1]
    ff = lw['w_ff_gate_f32'][1]
    return pl.pallas_call(
        _ffn_row_kernel,
        grid=(d_ff // FF_COLS,),
        in_specs=[_const_spec(h.shape), _layer_spec(*lw['p_row']), _layer_spec(*lw['g_ffn']),
                  pl.BlockSpec((None, D_MODEL, FF_COLS), lambda j: (ff, 0, j)),
                  pl.BlockSpec((None, D_MODEL, FF_COLS), lambda j: (ff, 0, j)),
                  pl.BlockSpec((None, FF_COLS, D_MODEL), lambda j: (ff, j, 0)),
                  _layer_spec(*lw['g_ple']), _layer_spec(*lw['w_ple_gate_f32']),
                  _layer_spec(*lw['w_ple_proj_f32'])],
        out_specs=[_whole_out_spec(h.shape),
                   pl.BlockSpec((D_MODEL, FF_COLS), lambda j: (0, j)),
                   pl.BlockSpec((D_MODEL, FF_COLS), lambda j: (0, j)),
                   pl.BlockSpec((FF_COLS, D_MODEL), lambda j: (j, 0))],
        out_shape=[jax.ShapeDtypeStruct(h.shape, F32),
                   jax.ShapeDtypeStruct((D_MODEL, d_ff), BF16),
                   jax.ShapeDtypeStruct((D_MODEL, d_ff), BF16),
                   jax.ShapeDtypeStruct((d_ff, D_MODEL), BF16)],
        scratch_shapes=[pltpu.VMEM((n_dec, D_MODEL), F32), pltpu.VMEM((n_dec, D_MODEL), F32)],
        compiler_params=_tc_params(),
        name="ffn_row",
    )(h, *[lw[k][0] for k in ('p_row', 'g_ffn', 'w_ff_gate_f32', 'w_ff_up_f32',
                              'w_ff_down_f32', 'g_ple', 'w_ple_gate_f32', 'w_ple_proj_f32')])


def _router_kernel(h_ref, gffn_ref, wr_ref, mi_ref, mf_ref, cnt_ref, xp_ref, carry_ref, *,
                   n_tok):
    g = pl.program_id(0)
    tm = h_ref.shape[0]

    @pl.when(g == 0)
    def _():
        carry_ref[...] = jnp.zeros(carry_ref.shape, F32)

    valid_row = (g * tm + lax.broadcasted_iota(I32, (tm, 1), 0)) < n_tok
    xn = _rms(jnp.where(valid_row, h_ref[...], 0.0), gffn_ref[...])
    xp_ref[...] = _pack_bf16_halves(xn)
    logits = _dot_full(xn, wr_ref[...]).T[:N_EXPERTS]
    valid = (g * tm + lax.broadcasted_iota(I32, (1, tm), 1)) < n_tok
    e = lax.broadcasted_iota(I32, logits.shape, 0)
    m1 = jnp.max(logits, axis=0, keepdims=True)
    i1 = jnp.min(jnp.where(logits == m1, e, N_EXPERTS), axis=0, keepdims=True)
    rest = jnp.where(e == i1, -jnp.inf, logits)
    m2 = jnp.max(rest, axis=0, keepdims=True)
    i2 = jnp.min(jnp.where(rest == m2, e, N_EXPERTS), axis=0, keepdims=True)
    e2 = jnp.exp(m2 - m1)
    denom = 1.0 + e2
    w1 = 1.0 / denom
    w2 = e2 / denom

    oh1 = jnp.where((e == i1) & valid, 1.0, 0.0)
    oh2 = jnp.where((e == i2) & valid, 1.0, 0.0)
    member = oh1 + oh2
    r = lax.broadcasted_iota(I32, (tm, tm), 0)
    c = lax.broadcasted_iota(I32, (tm, tm), 1)
    earlier = jnp.where(r < c, 1.0, 0.0).astype(BF16)
    pos = _dot(member, earlier) + carry_ref[...]
    pos1 = jnp.sum(oh1 * pos, axis=0, keepdims=True).astype(I32)
    pos2 = jnp.sum(oh2 * pos, axis=0, keepdims=True).astype(I32)
    carry_ref[...] = carry_ref[...] + jnp.sum(member, axis=1, keepdims=True)
    cnt_ref[...] = carry_ref[...]

    mi_ref[...] = jnp.where(e == 0, i1, jnp.where(e == 1, i2,
                            jnp.where(e == 2, pos1, jnp.where(e == 3, pos2, 0))))
    gates = jnp.where(e == 0, w1, jnp.where(e == 1, w2, 0.0))
    lanes = wr_ref.shape[1]
    gates = jnp.concatenate([gates, jnp.zeros((lanes - N_EXPERTS, tm), F32)], axis=0)
    mf_ref[...] = gates.T[:, :N_EXPERTS]


def _router(h, tile0, n_tok, lw):
    tm = TOK_ROWS
    return pl.pallas_call(
        functools.partial(_router_kernel, n_tok=n_tok),
        grid=(pl.cdiv(n_tok, tm),),
        in_specs=[pl.BlockSpec((tm, D_MODEL), lambda g: (tile0 + g, 0)),
                  _layer_spec(*lw['g_ffn']), _layer_spec(*lw['w_router'])],
        out_specs=[pl.BlockSpec((N_EXPERTS, tm), lambda g: (0, g)),
                   pl.BlockSpec((tm, N_EXPERTS), lambda g: (g, 0)),
                   _whole_out_spec((N_EXPERTS, 1)),
                   pl.BlockSpec((tm, D_MODEL // 2), lambda g: (g, 0))],
        out_shape=[jax.ShapeDtypeStruct((N_EXPERTS, n_tok), I32),
                   jax.ShapeDtypeStruct((n_tok, N_EXPERTS), F32),
                   jax.ShapeDtypeStruct((N_EXPERTS, 1), F32),
                   jax.ShapeDtypeStruct((n_tok, D_MODEL // 2), jnp.uint32)],
        scratch_shapes=[pltpu.VMEM((N_EXPERTS, 1), F32)],
        compiler_params=_tc_params(),
        name="router",
    )(h, lw['g_ffn'][0], lw['w_router'][0])


def _sc_chunk(n_rows):
    per_worker = pl.cdiv(n_rows, SC_WORKERS)
    return pl.cdiv(per_worker, SC_WINDOW) * SC_WINDOW


def _sc_worker_base(n_rows, chunk):
    wid = lax.axis_index("s") * SC_CORES + lax.axis_index("c")
    return jnp.minimum(wid * chunk, n_rows - chunk)


def _sc_scatter_rows(x, row0, dest, n_out):
    n = dest.shape[0] // TOP_K
    width = x.shape[1]
    chunk = _sc_chunk(n)
    mesh = plsc.VectorSubcoreMesh(core_axis_name="c", subcore_axis_name="s")

    @functools.partial(
        pl.kernel, mesh=mesh,
        out_type=jax.ShapeDtypeStruct((n_out, width), x.dtype),
        scratch_types=[pltpu.VMEM((SC_WINDOW,), I32) for _ in range(TOP_K)]
        + [pltpu.VMEM((SC_WINDOW, width), x.dtype), pltpu.SemaphoreType.DMA],
        name="sc_scatter_rows",
    )
    def scatter(x_hbm, dest_hbm, out_hbm, idx0_v, idx1_v, rows_v, sem):
        base = _sc_worker_base(n, chunk)

        @pl.loop(0, chunk // SC_WINDOW)
        def _(j):
            off = pl.multiple_of(base + j * SC_WINDOW, 8)
            pltpu.sync_copy(dest_hbm.at[pl.ds(off, SC_WINDOW)], idx0_v)
            pltpu.sync_copy(dest_hbm.at[pl.ds(n + off, SC_WINDOW)], idx1_v)
            pltpu.sync_copy(x_hbm.at[pl.ds(row0 + off, SC_WINDOW)], rows_v)
            first = pltpu.async_copy(rows_v, out_hbm.at[idx0_v], sem)
            second = pltpu.async_copy(rows_v, out_hbm.at[idx1_v], sem)
            first.wait()
            second.wait()

    return scatter(x, dest)


def _sc_gather_rows(y, idx):
    n = idx.shape[0]
    width = y.shape[1]
    chunk = _sc_chunk(n)
    mesh = plsc.VectorSubcoreMesh(core_axis_name="c", subcore_axis_name="s")

    @functools.partial(
        pl.kernel, mesh=mesh,
        out_type=jax.ShapeDtypeStruct((n, width), y.dtype),
        scratch_types=[pltpu.VMEM((SC_WINDOW,), I32),
                       pltpu.VMEM((SC_WINDOW, width), y.dtype), pltpu.SemaphoreType.DMA],
        name="sc_gather_rows",
    )
    def gather(y_hbm, idx_hbm, out_hbm, idx_v, rows_v, sem):
        base = _sc_worker_base(n, chunk)

        @pl.loop(0, chunk // SC_WINDOW)
        def _(j):
            off = pl.multiple_of(base + j * SC_WINDOW, 8)
            pltpu.sync_copy(idx_hbm.at[pl.ds(off, SC_WINDOW)], idx_v)
            pltpu.async_copy(y_hbm.at[idx_v], rows_v, sem).wait()
            pltpu.sync_copy(rows_v, out_hbm.at[pl.ds(off, SC_WINDOW)])

    return gather(y, idx)


def _pack_bf16_halves(y):
    half = y.shape[1] // 2
    rounded = y.astype(BF16).astype(F32)
    hi = lax.bitcast_convert_type(rounded[:, :half], jnp.uint32)
    lo = lax.bitcast_convert_type(rounded[:, half:], jnp.uint32)
    return hi | (lo >> 16)


def _unpack_bf16_halves(packed):
    hi = lax.bitcast_convert_type(packed & jnp.uint32(0xFFFF0000), F32)
    lo = lax.bitcast_convert_type(packed << 16, F32)
    return jnp.concatenate([hi, lo], axis=1)


def _expert_kernel(tile_expert_ref, n_valid_ref, xs_ref, wg_ref, wu_ref, wd_ref, y_ref):
    @pl.when(pl.program_id(0) < n_valid_ref[0])
    def _():
        xn = _unpack_bf16_halves(xs_ref[...]).astype(BF16)
        a = jax.nn.silu(_dot(xn, wg_ref[...])) * _dot(xn, wu_ref[...])
        y_ref[...] = _pack_bf16_halves(_dot(a, wd_ref[...]))


def _experts(xs, tile_expert, n_valid, lw):
    n_slots = xs.shape[0]
    te = EXPERT_ROWS
    d_exp = lw['w_ex_gate'][0].shape[-1]
    moe = lw['w_ex_gate'][1]
    row_block = lambda g, tex, nv: (jnp.minimum(g, nv[0] - 1), 0)
    w_block = lambda g, tex, nv: (moe, tex[g], 0, 0)
    return pl.pallas_call(
        _expert_kernel,
        grid_spec=pltpu.PrefetchScalarGridSpec(
            num_scalar_prefetch=2,
            grid=(n_slots // te,),
            in_specs=[pl.BlockSpec((te, xs.shape[1]), row_block),
                      pl.BlockSpec((None, None, D_MODEL, d_exp), w_block),
                      pl.BlockSpec((None, None, D_MODEL, d_exp), w_block),
                      pl.BlockSpec((None, None, d_exp, D_MODEL), w_block)],
            out_specs=pl.BlockSpec((te, D_MODEL // 2), row_block)),
        out_shape=jax.ShapeDtypeStruct((n_slots, D_MODEL // 2), jnp.uint32),
        compiler_params=_tc_params(),
        name="experts",
    )(tile_expert, n_valid, xs, *[lw[k][0] for k in ('w_ex_gate', 'w_ex_up', 'w_ex_down')])


def _combine_math(h_ref, yg_ref, mf_ref, p, gple_ref, wpg_ref, wpp_ref, gfin_ref):
    gates = mf_ref[...]
    h = h_ref[...] + (gates[:, 0:1] * _unpack_bf16_halves(yg_ref[0])
                      + gates[:, 1:2] * _unpack_bf16_halves(yg_ref[1]))
    return _rms(_ple(h, p, gple_ref, wpg_ref, wpp_ref, _dot), gfin_ref[...])


def _combine_seq_kernel(h_ref, yg_ref, mf_ref, pp_ref, gple_ref, wpg_ref, wpp_ref, gfin_ref,
                        ybuf_ref, yp_ref):
    del ybuf_ref
    yp_ref[...] = _combine_math(h_ref, yg_ref, mf_ref, pp_ref[...], gple_ref, wpg_ref,
                                wpp_ref, gfin_ref)


def _combine_tail_kernel(h_ref, yg_ref, mf_ref, pp_ref, ps_ref, gple_ref, wpg_ref, wpp_ref,
                         gfin_ref, ybuf_ref, yp_ref, ys_ref, pbuf_ref):
    del ybuf_ref
    g = pl.program_id(0)
    last = pl.num_programs(0) - 1
    n_dec = ps_ref.shape[0]
    pbuf_ref[...] = pp_ref[...]

    @pl.when(g == last)
    def _():
        pbuf_ref[0:n_dec, :] = ps_ref[...]

    out = _combine_math(h_ref, yg_ref, mf_ref, pbuf_ref[...], gple_ref, wpg_ref, wpp_ref,
                        gfin_ref)

    @pl.when(g < last)
    def _():
        yp_ref[...] = out

    @pl.when(g == last)
    def _():
        ys_ref[...] = out[0:n_dec, :]


def _combine(h, tile0, yg, mf, y_seq, lw, with_rows):
    tm = COMBINE_ROWS
    assert tile0 * TOK_ROWS % tm == 0
    tile0 = tile0 * TOK_ROWS // tm
    n_tiles = pl.cdiv(yg.shape[1], tm)
    last_seq_tile = y_seq.shape[0] // tm - 1
    pp, layer = lw['p_seq']
    n_dec = lw['p_row'][0].shape[-2]
    weights = [lw[k] for k in (('p_row',) if with_rows else ())
               + ('g_ple', 'w_ple_gate', 'w_ple_proj', 'g_final')]
    seq_tile = lambda g: (jnp.minimum(tile0 + g, last_seq_tile), 0)
    n_in = 4 + len(weights)
    out = pl.pallas_call(
        _combine_tail_kernel if with_rows else _combine_seq_kernel,
        grid=(n_tiles,),
        in_specs=[pl.BlockSpec((tm, D_MODEL), lambda g: (tile0 + g, 0)),
                  pl.BlockSpec((TOP_K, tm, yg.shape[2]), lambda g: (0, g, 0)),
                  pl.BlockSpec((tm, N_EXPERTS), lambda g: (g, 0)),
                  pl.BlockSpec((None, tm, D_PLE), lambda g: (layer,) + seq_tile(g))]
        + [_layer_spec(*w) for w in weights]
        + [pl.BlockSpec(memory_space=pl.ANY)],
        out_specs=[pl.BlockSpec((tm, D_MODEL), seq_tile)]
        + ([_whole_out_spec((n_dec, D_MODEL))] if with_rows else []),
        out_shape=[jax.ShapeDtypeStruct(y_seq.shape, F32)]
        + ([jax.ShapeDtypeStruct((n_dec, D_MODEL), F32)] if with_rows else []),
        scratch_shapes=[pltpu.VMEM((tm, D_PLE), F32)] if with_rows else [],
        input_output_aliases={n_in: 0},
        compiler_params=_tc_params(),
        name="combine",
    )(h, yg, mf, pp, *[w for w, _ in weights], y_seq)
    return out if with_rows else (out[0], None)


def _moe_layer(h, y_seq, lw):
    n_tok = h.shape[0]
    tm = TOK_ROWS
    te = EXPERT_ROWS
    tiles = pl.cdiv(n_tok, tm)
    bounds = [tiles * c // MOE_CHUNKS for c in range(MOE_CHUNKS + 1)]
    y_rows = None
    for c in range(MOE_CHUNKS):
        tile0 = bounds[c]
        n = min(bounds[c + 1] * tm, n_tok) - tile0 * tm
        mi, mf, counts, xp = _router(h, tile0, n, lw)

        cnt = counts[:, 0].astype(I32)
        padded = (cnt + te - 1) // te * te
        ends = jnp.cumsum(padded)
        starts = ends - padded
        experts = jnp.arange(N_EXPERTS, dtype=I32)
        start_of = lambda e: jnp.sum(
            jnp.where(e[None, :] == experts[:, None], starts[:, None], 0), axis=0)
        dest = jnp.concatenate([start_of(mi[0]) + mi[2], start_of(mi[1]) + mi[3]])
        dest, h = lax.optimization_barrier((dest, h))
        n_tiles = pl.cdiv(TOP_K * n + N_EXPERTS * (te - 1), te)
        tile_start = jnp.arange(n_tiles, dtype=I32) * te
        last_used = jnp.max(jnp.where(padded > 0, experts, 0))
        tile_expert = jnp.minimum(
            jnp.sum(tile_start[:, None] >= ends[None, :], axis=-1).astype(I32), last_used)
        n_valid = (ends[-1:] // te).astype(I32)

        xs = _sc_scatter_rows(xp, 0, dest, n_tiles * te)
        y = _experts(xs, tile_expert, n_valid, lw)
        yg = _sc_gather_rows(y, dest).reshape(TOP_K, n, y.shape[1])
        y_seq, rows = _combine(h, tile0, yg, mf, y_seq, lw, with_rows=c == MOE_CHUNKS - 1)
        y_rows = rows if rows is not None else y_rows
    return y_seq, y_rows


def kernel(x_prompt, x_sample, state_conv_a, state_conv_c, p_prompt, p_sample, g_mix, w_in, w_conv_a, w_s, b_s, g_ln_b, b_ln_b, w_conv_c, b_conv_c, g_ln_c, b_ln_c, g_out, w_o, g_ffn, w_ff_gate, w_ff_up, w_ff_down, w_router, w_ex_gate, w_ex_up, w_ex_down, g_ple, w_ple_gate, w_ple_proj, g_final):
    depth = g_mix.shape[0]
    n_seq, seq, _ = x_prompt.shape
    n_dec = x_sample.shape[0]
    n_prompt = n_seq * seq
    assert depth == 2 and x_sample.shape[1] == 1
    assert seq % MIXER_ROWS == 0 and TOK_ROWS % n_dec == 0 and w_ff_gate.shape[-1] % FF_COLS == 0

    vectors = {'g_mix': g_mix, 'g_out': g_out, 'g_ffn': g_ffn, 'g_ple': g_ple,
               'b_conv_c': b_conv_c, 'g_ln_c': g_ln_c, 'b_ln_c': b_ln_c,
               'g_ln_b': g_ln_b, 'b_ln_b': b_ln_b}
    pieces, lanes, total = [], {}, 0
    for name, x in sorted(vectors.items(), key=lambda kv: -kv[1].shape[1]):
        n = x.shape[1]
        gap = -total % n
        pieces += [jnp.zeros((depth, gap), F32)] * (gap > 0) + [x]
        lanes[name] = (total + gap, n)
        total += gap + n
    packed = jnp.concatenate(pieces, axis=1).reshape(depth, 1, total)
    per_layer = {
        'w_in_f32': w_in, 'w_conv_a': w_conv_a,
        'w_s_cat': jnp.transpose(w_s, (0, 2, 1, 3)).reshape(depth, CHUNK, N_HEADS_B * CHUNK),
        'b_s_full': jnp.repeat(jnp.swapaxes(b_s, 1, 2), HEAD_DIM, axis=2),
        'w_conv_c': w_conv_c, 'w_o_f32': w_o,
        'w_ple_gate_f32': w_ple_gate, 'w_ple_proj_f32': w_ple_proj,
        'state_a': jnp.swapaxes(state_conv_a, 1, 2), 'state_c': jnp.swapaxes(state_conv_c, 1, 2),
        'p_seq': p_prompt.reshape(depth, n_prompt, D_PLE),
        'p_row': p_sample.reshape(depth, n_dec, D_PLE),
    }
    per_dense = {'w_ff_gate_f32': w_ff_gate, 'w_ff_up_f32': w_ff_up, 'w_ff_down_f32': w_ff_down}
    one_layer = lambda w: (w.reshape((1,) + w.shape), 0)
    per_moe = {'w_router': jnp.pad(w_router, ((0, 0), (0, 0), (0, LANES - N_EXPERTS)))}
    to_convert = [('w_ex_gate', w_ex_gate), ('w_ex_up', w_ex_up), ('w_ex_down', w_ex_down)]

    def with_ride(call):
        name, w = to_convert.pop(0)
        *outputs, w_bf16 = call(w.reshape(-1, w.shape[-1]))
        per_moe[name] = w_bf16.reshape(w.shape)
        return outputs

    hp = x_prompt.reshape(n_prompt, D_MODEL)
    hs = x_sample.reshape(n_dec, D_MODEL)
    outs = {'a_p': [], 'c_p': []}
    row_states = None
    for i in range(depth):
        is_expert_layer = i % 2 == 1
        lw = {k: (v, i) for k, v in per_layer.items()}
        lw.update({k: (packed, _Lanes(i, off, n)) for k, (off, n) in lanes.items()})
        lw.update({k: (v, i // 2) for k, v in (per_dense if not is_expert_layer else {}).items()})
        lw['g_final'] = (g_final.reshape(1, 1, -1), 0)

        hs, *row_states, w0, w1, w2, w3 = _mixer_row(hs, 0, lw, row_states)
        lw.update({k: one_layer(w) for k, w in zip(CONVERTED_BY_MIXER_ROW, (w0, w1, w2, w3))})
        mixer_in = hp
        hp, a_p, c_p = with_ride(functools.partial(
            _mixer_seq, hp, hs if is_expert_layer else None, lw, n_seq, seq))
        outs['a_p'].append(a_p)
        outs['c_p'].append(c_p)
        if not is_expert_layer:
            hs, *converted = _ffn_row(hs, lw)
            lw.update({k: one_layer(w) for k, w in
                       zip(('w_ff_gate', 'w_ff_up', 'w_ff_down'), converted)})
            hp, = with_ride(functools.partial(_ffn_dense, hp, lw))
        else:
            lw.update({k: (v, i // 2) for k, v in per_moe.items()})
            y_prompt, y_sample = _moe_layer(hp, mixer_in, lw)

    a_s, c_s, v_s = row_states
    return (y_prompt.reshape(x_prompt.shape), y_sample.reshape(x_sample.shape),
            jnp.stack(outs['a_p']), a_s, jnp.stack(outs['c_p']), c_s,
            v_s.reshape(depth, n_dec, 1, D_B))
```

```python
import collections
import functools

import jax
import jax.numpy as jnp
from jax import lax
from jax.experimental import pallas as pl
from jax.experimental.pallas import tpu as pltpu
from jax.experimental.pallas import tpu_sc as plsc

D_MODEL = 1024
HEAD_DIM = 64
D_A = 384
D_B = 256
D_C = 384
N_HEADS_B = D_B // HEAD_DIM
CONV_A = 3
CONV_C = 31
CHUNK = 128
D_IN = 3 * D_A + 2 * D_B + 2 * D_C
D_PLE = 256
N_EXPERTS = 8
TOP_K = 2
EPS = 1e-6

O1, O2, O3 = D_A, 2 * D_A, 3 * D_A
O4 = O3 + 2 * D_B

F32 = jnp.float32
BF16 = jnp.bfloat16
I32 = jnp.int32

VMEM_LIMIT_BYTES = 56 * 1024 * 1024
SUBLANES = 8
LANES = 128

TOK_ROWS = 512
MIXER_ROWS = 1024
COMBINE_ROWS = 1024
CONV_ROWS = 64
A_HALO = 8
C_HALO = 32
EXPERT_ROWS = 512
FF_COLS = 256
MOE_CHUNKS = 2

SC_CORES = 2
SC_WORKERS = 32
SC_WINDOW = 48


def _rms(x, g):
    return x * lax.rsqrt(jnp.mean(x * x, axis=-1, keepdims=True) + EPS) * g


def _ln(x, g, b):
    mu = jnp.mean(x, axis=-1, keepdims=True)
    xc = x - mu
    var = jnp.mean(xc * xc, axis=-1, keepdims=True)
    return xc * lax.rsqrt(var + EPS) * g + b


def _dot(a, b):
    return jnp.dot(a.astype(BF16), b, preferred_element_type=F32)


def _split(x):
    hi = x.astype(BF16)
    return hi, (x - hi.astype(F32)).astype(BF16)


def _dot_full(a, b):
    a_hi, a_lo = _split(a)
    b_hi, b_lo = _split(b)
    m = a.shape[0]
    by_hi = jnp.dot(jnp.concatenate([a_hi, a_lo], axis=0), b_hi, preferred_element_type=F32)
    return by_hi[:m] + by_hi[m:] + jnp.dot(a_hi, b_lo, preferred_element_type=F32)


def _const_spec(shape):
    return pl.BlockSpec(shape, lambda *_: (0,) * len(shape), pipeline_mode=pl.Buffered(1))


_Lanes = collections.namedtuple('_Lanes', 'layer offset size')


def _layer_spec(stacked, layer):
    if isinstance(layer, _Lanes):
        where = (layer.layer, 0, layer.offset // layer.size)
        return pl.BlockSpec((None, 1, layer.size), lambda *_: where,
                            pipeline_mode=pl.Buffered(1))
    idx = (layer,) if isinstance(layer, int) else tuple(layer)
    rest = stacked.shape[len(idx):]
    return pl.BlockSpec((None,) * len(idx) + rest, lambda *_: idx + (0,) * len(rest),
                        pipeline_mode=pl.Buffered(1))


def _whole_out_spec(shape):
    return pl.BlockSpec(shape, lambda *_: (0,) * len(shape))


def _tc_params():
    return pltpu.CompilerParams(dimension_semantics=("arbitrary",),
                                vmem_limit_bytes=VMEM_LIMIT_BYTES)


def _normed_groups(y_a, y_b, y_c, gout_ref):
    return jnp.concatenate([_rms(y_a, gout_ref[:, :D_A]),
                            _rms(y_b, gout_ref[:, D_A:D_A + D_B]),
                            _rms(y_c, gout_ref[:, D_A + D_B:])], axis=-1)


def _mixer_seq_kernel(hp_ref, hs_ref, gmix_ref, win_ref, wca_ref, wscat_ref, bsfull_ref,
                      glnb_ref, blnb_ref, wcc_ref, bcc_ref, glnc_ref, blnc_ref, gout_ref,
                      wo_ref, ride_ref, hout_ref, newa_ref, newc_ref, ride_out_ref,
                      qext_ref, gext_ref, gshift_ref, conv_ref, wcc8_ref, *, tiles_per_seq,
                      n_seq_tiles):
    g = pl.program_id(0)
    tm = hp_ref.shape[0]

    @pl.when(g == 0)
    def _():
        for k in range(CONV_C):
            wcc8_ref[k] = jnp.broadcast_to(wcc_ref[k:k + 1, :], (SUBLANES, D_C))

    @pl.when(g < n_seq_tiles)
    def _sequence_tile():
        ride_out_ref[...] = ride_ref[...].astype(BF16)

        @pl.when(g % tiles_per_seq == 0)
        def _():
            qext_ref[0:A_HALO, :] = jnp.zeros((A_HALO, D_A), F32)
            gext_ref[0:C_HALO, :] = jnp.zeros((C_HALO, D_C), F32)

        h = hp_ref[...]
        z = _dot(_rms(h, gmix_ref[...]), win_ref[...])

        qext_ref[A_HALO:A_HALO + tm, :] = z[:, O1:O2] * z[:, O2:O3]
        conv_a = jnp.zeros((tm, D_A), F32)
        for k in range(CONV_A):
            off = A_HALO - (CONV_A - 1) + k
            conv_a = conv_a + wca_ref[k:k + 1, :] * qext_ref[off:off + tm, :]
        y_a = z[:, :O1] * conv_a
        last_q = qext_ref[A_HALO + tm - (CONV_A - 1):A_HALO + tm, :]
        newa_ref[...] = last_q
        qext_ref[A_HALO - (CONV_A - 1):A_HALO, :] = last_q

        zb = jax.nn.gelu(z[:, O3:O4])
        v = _ln(zb[:, D_B:], glnb_ref[...], blnb_ref[...])
        row = lax.broadcasted_iota(I32, (CHUNK, N_HEADS_B * CHUNK), 0)
        col = lax.broadcasted_iota(I32, (CHUNK, N_HEADS_B * CHUNK), 1)
        w_tril = jnp.where((col % CHUNK) <= row, wscat_ref[...], 0.0).astype(BF16)
        lane_head = lax.broadcasted_iota(I32, (CHUNK, D_B), 1) // HEAD_DIM
        s_chunks = []
        for c in range(tm // CHUNK):
            vc = v[c * CHUNK:(c + 1) * CHUNK, :]
            vstack = jnp.concatenate(
                [jnp.where(lane_head == hd, vc, 0.0) for hd in range(N_HEADS_B)], axis=0)
            s_chunks.append(_dot(w_tril, vstack.astype(BF16)) + bsfull_ref[...])
        y_b = zb[:, :D_B] * jnp.concatenate(s_chunks, axis=0)

        gext_ref[C_HALO:C_HALO + tm, :] = (z[:, O4:O4 + D_C]
                                           * jax.nn.sigmoid(z[:, O4 + D_C:]))
        n_shift = gshift_ref.shape[1]
        for s in range(1, SUBLANES):
            gshift_ref[s - 1] = gext_ref[s:s + n_shift, :]
        base = C_HALO - (CONV_C - 1)
        for r0 in range(0, tm, CONV_ROWS):
            acc = jnp.zeros((CONV_ROWS // SUBLANES, SUBLANES, D_C), F32)
            for k in range(CONV_C):
                lo = (base + k) // SUBLANES * SUBLANES + r0
                s = (base + k) % SUBLANES
                window = (gext_ref[lo:lo + CONV_ROWS, :] if s == 0
                          else gshift_ref[s - 1, lo:lo + CONV_ROWS, :])
                acc = acc + wcc8_ref[k][None] * window.reshape(acc.shape)
            conv_ref[r0:r0 + CONV_ROWS, :] = acc.reshape(CONV_ROWS, D_C)
        y_c = jax.nn.silu(_ln(conv_ref[...] + bcc_ref[...], glnc_ref[...], blnc_ref[...]))
        last_g = gext_ref[C_HALO + tm - (CONV_C - 1):C_HALO + tm, :]
        newc_ref[...] = last_g
        gext_ref[C_HALO - (CONV_C - 1):C_HALO, :] = last_g

        hout_ref[...] = h + _dot(_normed_groups(y_a, y_b, y_c, gout_ref), wo_ref[...])

    @pl.when(g == n_seq_tiles)
    def _append_sample_rows():
        hout_ref[0:hs_ref.shape[0], :] = hs_ref[...]


def _ride_specs(ride, n_steps):
    rows = ride.shape[0] // n_steps
    assert rows * n_steps == ride.shape[0]
    slab = lambda g: (jnp.minimum(g, n_steps - 1), 0)
    return (pl.BlockSpec((rows, ride.shape[1]), slab), pl.BlockSpec((rows, ride.shape[1]), slab),
            jax.ShapeDtypeStruct(ride.shape, BF16))


def _mixer_seq(hp, hs, lw, n_seq, seq, ride):
    tm = MIXER_ROWS
    tiles_per_seq = seq // tm
    n_seq_tiles = n_seq * tiles_per_seq
    join = hs is not None
    if not join:
        hs = jnp.zeros((SUBLANES, D_MODEL), F32)
    n_out = n_seq * seq + (hs.shape[0] if join else 0)
    weights = [lw[k] for k in ('g_mix', 'w_in', 'w_conv_a', 'w_s_cat', 'b_s_full', 'g_ln_b',
                               'b_ln_b', 'w_conv_c', 'b_conv_c', 'g_ln_c', 'b_ln_c', 'g_out',
                               'w_o')]
    seq_of = lambda g: jnp.minimum(g // tiles_per_seq, n_seq - 1)
    n_shift = tm + C_HALO - SUBLANES
    ride_in_spec, ride_out_spec, ride_out_shape = _ride_specs(ride, n_seq_tiles)
    return pl.pallas_call(
        functools.partial(_mixer_seq_kernel, tiles_per_seq=tiles_per_seq,
                          n_seq_tiles=n_seq_tiles),
        grid=(n_seq_tiles + int(join),),
        in_specs=[pl.BlockSpec((tm, D_MODEL), lambda g: (jnp.minimum(g, n_seq_tiles - 1), 0)),
                  _const_spec(hs.shape)]
        + [_layer_spec(*w) for w in weights] + [ride_in_spec],
        out_specs=[pl.BlockSpec((tm, D_MODEL), lambda g: (g, 0)),
                   pl.BlockSpec((None, CONV_A - 1, D_A), lambda g: (seq_of(g), 0, 0)),
                   pl.BlockSpec((None, CONV_C - 1, D_C), lambda g: (seq_of(g), 0, 0)),
                   ride_out_spec],
        out_shape=[jax.ShapeDtypeStruct((n_out, D_MODEL), F32),
                   jax.ShapeDtypeStruct((n_seq, CONV_A - 1, D_A), F32),
                   jax.ShapeDtypeStruct((n_seq, CONV_C - 1, D_C), F32),
                   ride_out_shape],
        scratch_shapes=[pltpu.VMEM((A_HALO + tm, D_A), F32),
                        pltpu.VMEM((C_HALO + tm, D_C), F32),
                        pltpu.VMEM((SUBLANES - 1, n_shift, D_C), F32),
                        pltpu.VMEM((tm, D_C), F32),
                        pltpu.VMEM((CONV_C, SUBLANES, D_C), F32)],
        compiler_params=_tc_params(),
        name="mixer_seq",
    )(hp, hs, *[w for w, _ in weights], ride)


def _mixer_row_kernel(*refs, n_carried, layer):
    (h_ref, sa_ref, sc_hbm, gmix_ref, win_hbm, wca_ref, wscat_ref, bsfull_ref, glnb_ref,
     blnb_ref, wcc_ref, bcc_ref, glnc_ref, blnc_ref, gout_ref, wo_hbm, wpg_hbm,
     wpp_hbm) = refs[:18]
    (hout_ref, newa_ref, newc_ref, v_ref, win_bf_hbm, wo_bf_hbm, wpg_bf_hbm, wpp_bf_hbm,
     sc_ref, win_ref, wo_ref, wpg_ref, wpp_ref, win_bf_ref, wo_bf_ref, wpg_bf_ref,
     wpp_bf_ref, in_sem, out_sem) = refs[18 + n_carried:]

    @pl.when(pl.program_id(0) > 0)
    def _():
        for ref in (newa_ref, newc_ref, v_ref):
            ref[...] = jnp.zeros(ref.shape, F32)

    @pl.when(pl.program_id(0) == 0)
    def _():
        fetched = {}
        for i, (name, src, dst) in enumerate((('w_in', win_hbm, win_ref),
                                              ('state_c', sc_hbm, sc_ref),
                                              ('w_o', wo_hbm, wo_ref),
                                              ('w_ple_gate', wpg_hbm, wpg_ref),
                                              ('w_ple_proj', wpp_hbm, wpp_ref))):
            fetched[name] = pltpu.make_async_copy(src.at[layer], dst, in_sem.at[i])
            fetched[name].start()
        stored = []

        def ready(name):
            fetched[name].wait()
            if name == 'state_c':
                return
            i = CONVERTED_BY_MIXER_ROW.index(name)
            f32_ref, bf_ref, bf_hbm = ((win_ref, win_bf_ref, win_bf_hbm),
                                       (wo_ref, wo_bf_ref, wo_bf_hbm),
                                       (wpg_ref, wpg_bf_ref, wpg_bf_hbm),
                                       (wpp_ref, wpp_bf_ref, wpp_bf_hbm))[i]
            bf_ref[...] = f32_ref[...].astype(BF16)
            stored.append(pltpu.make_async_copy(bf_ref, bf_hbm, out_sem.at[i]))
            stored[-1].start()

        _mixer_row_body(h_ref, sa_ref, sc_ref, gmix_ref, win_ref, wca_ref, wscat_ref,
                        bsfull_ref, glnb_ref, blnb_ref, wcc_ref, bcc_ref, glnc_ref, blnc_ref,
                        gout_ref, wo_ref, hout_ref, newa_ref, newc_ref, v_ref, ready)
        ready('w_ple_gate')
        ready('w_ple_proj')
        for copy in stored:
            copy.wait()


CONVERTED_BY_MIXER_ROW = ('w_in', 'w_o', 'w_ple_gate', 'w_ple_proj')


def _mixer_row_body(h_ref, sa_ref, sc_ref, gmix_ref, win_ref, wca_ref, wscat_ref,
                    bsfull_ref, glnb_ref, blnb_ref, wcc_ref, bcc_ref, glnc_ref, blnc_ref,
                    gout_ref, wo_ref, hout_ref, newa_ref, newc_ref, v_ref, ready):
    h = h_ref[...]
    xn = _rms(h, gmix_ref[...])
    ready('w_in')
    z = _dot_full(xn, win_ref[...])

    q = z[:, O1:O2] * z[:, O2:O3]
    conv_a = wca_ref[CONV_A - 1:CONV_A, :] * q
    for k in range(CONV_A - 1):
        conv_a = conv_a + wca_ref[k:k + 1, :] * sa_ref[k]
    y_a = z[:, :O1] * conv_a
    for k in range(CONV_A - 2):
        newa_ref[:, k, :] = sa_ref[k + 1]
    newa_ref[:, CONV_A - 2, :] = q

    zb = jax.nn.gelu(z[:, O3:O4])
    v = _ln(zb[:, D_B:], glnb_ref[...], blnb_ref[...])
    v_ref[...] = v
    w_diag0 = jnp.concatenate(
        [jnp.broadcast_to(wscat_ref[0:1, hd * CHUNK:hd * CHUNK + 1], (1, HEAD_DIM))
         for hd in range(N_HEADS_B)], axis=-1)
    y_b = zb[:, :D_B] * (w_diag0 * v + bsfull_ref[0:1, :])

    glu = z[:, O4:O4 + D_C] * jax.nn.sigmoid(z[:, O4 + D_C:])
    conv_c = wcc_ref[CONV_C - 1:CONV_C, :] * glu
    ready('state_c')
    for k in range(CONV_C - 1):
        conv_c = conv_c + wcc_ref[k:k + 1, :] * sc_ref[k]
    y_c = jax.nn.silu(_ln(conv_c + bcc_ref[...], glnc_ref[...], blnc_ref[...]))
    for k in range(CONV_C - 2):
        newc_ref[:, k, :] = sc_ref[k + 1]
    newc_ref[:, CONV_C - 2, :] = glu

    y = _normed_groups(y_a, y_b, y_c, gout_ref)
    ready('w_o')
    hout_ref[...] = h + _dot_full(y, wo_ref[...])


def _mixer_row(h, h_block, lw, states):
    depth, _, n_dec, _ = lw['state_a'][0].shape
    layer = lw['state_a'][1]
    weights = [lw[k] for k in ('state_a', 'state_c', 'g_mix', 'w_in_f32', 'w_conv_a', 'w_s_cat',
                               'b_s_full', 'g_ln_b', 'b_ln_b', 'w_conv_c', 'b_conv_c', 'g_ln_c',
                               'b_ln_c', 'g_out', 'w_o_f32', 'w_ple_gate_f32', 'w_ple_proj_f32')]
    bf16_shapes = [lw[k + '_f32'][0].shape[1:] for k in CONVERTED_BY_MIXER_ROW]
    state_shapes = [(n_dec, CONV_A - 1, D_A), (n_dec, CONV_C - 1, D_C), (n_dec, D_B)]
    carried = list(states or ())
    first = 1 + len(weights)
    block_of = (lambda g: g) if states is None else (lambda g: layer)
    by_hand = ('state_c', 'w_in_f32', 'w_o_f32', 'w_ple_gate_f32', 'w_ple_proj_f32')
    landing = [lw[k][0].shape[1:] for k in ('state_c',) + tuple(
        k + '_f32' for k in CONVERTED_BY_MIXER_ROW)]
    in_hbm = pl.BlockSpec(memory_space=pl.ANY)
    names = ('state_a', 'state_c', 'g_mix', 'w_in_f32', 'w_conv_a', 'w_s_cat', 'b_s_full',
             'g_ln_b', 'b_ln_b', 'w_conv_c', 'b_conv_c', 'g_ln_c', 'b_ln_c', 'g_out', 'w_o_f32',
             'w_ple_gate_f32', 'w_ple_proj_f32')
    return pl.pallas_call(
        functools.partial(_mixer_row_kernel, n_carried=len(carried), layer=layer),
        grid=(depth if states is None else 1,),
        in_specs=[pl.BlockSpec((n_dec, D_MODEL), lambda g: (h_block, 0))]
        + [in_hbm if k in by_hand else _layer_spec(*lw[k]) for k in names]
        + [in_hbm for _ in carried],
        out_specs=[_whole_out_spec((n_dec, D_MODEL))]
        + [pl.BlockSpec((None,) + s, lambda g, s=s: (block_of(g),) + (0,) * len(s))
           for s in state_shapes]
        + [in_hbm for _ in bf16_shapes],
        out_shape=[jax.ShapeDtypeStruct((n_dec, D_MODEL), F32)]
        + [jax.ShapeDtypeStruct((depth,) + s, F32) for s in state_shapes]
        + [jax.ShapeDtypeStruct(s, BF16) for s in bf16_shapes],
        scratch_shapes=[pltpu.VMEM(s, F32) for s in landing]
        + [pltpu.VMEM(s, BF16) for s in bf16_shapes]
        + [pltpu.SemaphoreType.DMA((len(landing),)),
           pltpu.SemaphoreType.DMA((len(bf16_shapes),))],
        input_output_aliases={first + j: 1 + j for j in range(len(carried))},
        compiler_params=_tc_params(),
        name="mixer_row",
    )(h, *[w for w, _ in weights], *carried)


def _ple(h, p, gple_ref, wpg_ref, wpp_ref, dot):
    gate = jax.nn.sigmoid(dot(_rms(h, gple_ref[...]), wpg_ref[...]))
    return h + gate * dot(p, wpp_ref[...])


def _ffn_dense_kernel(h_ref, p_ref, gffn_ref, wg_ref, wu_ref, wd_ref, gple_ref, wpg_ref,
                      wpp_ref, ride_ref, out_ref, ride_out_ref):
    ride_out_ref[...] = ride_ref[...].astype(BF16)
    h = h_ref[...]
    xn = _rms(h, gffn_ref[...]).astype(BF16)
    a = jax.nn.silu(_dot(xn, wg_ref[...])) * _dot(xn, wu_ref[...])
    h = h + _dot(a, wd_ref[...])
    out_ref[...] = _ple(h, p_ref[...], gple_ref, wpg_ref, wpp_ref, _dot)


def _ffn_dense(h, lw, ride):
    n_tok = h.shape[0]
    tm = TOK_ROWS
    p, layer = lw['p_seq']
    weights = [lw[k] for k in ('g_ffn', 'w_ff_gate', 'w_ff_up', 'w_ff_down', 'g_ple',
                               'w_ple_gate', 'w_ple_proj')]
    ride_in_spec, ride_out_spec, ride_out_shape = _ride_specs(ride, n_tok // tm)
    return pl.pallas_call(
        _ffn_dense_kernel,
        grid=(n_tok // tm,),
        in_specs=[pl.BlockSpec((tm, D_MODEL), lambda g: (g, 0)),
                  pl.BlockSpec((None, tm, D_PLE), lambda g: (layer, g, 0))]
        + [_layer_spec(*w) for w in weights] + [ride_in_spec],
        out_specs=[pl.BlockSpec((tm, D_MODEL), lambda g: (g, 0)), ride_out_spec],
        out_shape=[jax.ShapeDtypeStruct(h.shape, F32), ride_out_shape],
        compiler_params=_tc_params(),
        name="ffn_dense",
    )(h, p, *[w for w, _ in weights], ride)


def _ffn_row_kernel(h_ref, p_ref, gffn_ref, wg_ref, wu_ref, wd_ref, gple_ref, wpg_ref,
                    wpp_ref, out_ref, wg_bf_ref, wu_bf_ref, wd_bf_ref, xn_ref, acc_ref):
    j = pl.program_id(0)

    @pl.when(j == 0)
    def _():
        h = h_ref[...]
        xn_ref[...] = _rms(h, gffn_ref[...])
        acc_ref[...] = h

    for src, dst in ((wg_ref, wg_bf_ref), (wu_ref, wu_bf_ref), (wd_ref, wd_bf_ref)):
        dst[...] = src[...].astype(BF16)

    xn = xn_ref[...]
    a = jax.nn.silu(_dot_full(xn, wg_ref[...])) * _dot_full(xn, wu_ref[...])
    acc_ref[...] += _dot_full(a, wd_ref[...])

    @pl.when(j == pl.num_programs(0) - 1)
    def _():
        out_ref[...] = _ple(acc_ref[...], p_ref[...], gple_ref, wpg_ref, wpp_ref, _dot_full)


def _ffn_row(h, lw):
    n_dec = h.shape[0]
    d_ff = lw['w_ff_gate_f32'][0].shape[-1]
    ff = lw['w_ff_gate_f32'][1]
    return pl.pallas_call(
        _ffn_row_kernel,
        grid=(d_ff // FF_COLS,),
        in_specs=[_const_spec(h.shape), _layer_spec(*lw['p_row']), _layer_spec(*lw['g_ffn']),
                  pl.BlockSpec((None, D_MODEL, FF_COLS), lambda j: (ff, 0, j)),
                  pl.BlockSpec((None, D_MODEL, FF_COLS), lambda j: (ff, 0, j)),
                  pl.BlockSpec((None, FF_COLS, D_MODEL), lambda j: (ff, j, 0)),
                  _layer_spec(*lw['g_ple']), _layer_spec(*lw['w_ple_gate_f32']),
                  _layer_spec(*lw['w_ple_proj_f32'])],
        out_specs=[_whole_out_spec(h.shape),
                   pl.BlockSpec((D_MODEL, FF_COLS), lambda j: (0, j)),
                   pl.BlockSpec((D_MODEL, FF_COLS), lambda j: (0, j)),
                   pl.BlockSpec((FF_COLS, D_MODEL), lambda j: (j, 0))],
        out_shape=[jax.ShapeDtypeStruct(h.shape, F32),
                   jax.ShapeDtypeStruct((D_MODEL, d_ff), BF16),
                   jax.ShapeDtypeStruct((D_MODEL, d_ff), BF16),
                   jax.ShapeDtypeStruct((d_ff, D_MODEL), BF16)],
        scratch_shapes=[pltpu.VMEM((n_dec, D_MODEL), F32), pltpu.VMEM((n_dec, D_MODEL), F32)],
        compiler_params=_tc_params(),
        name="ffn_row",
    )(h, *[lw[k][0] for k in ('p_row', 'g_ffn', 'w_ff_gate_f32', 'w_ff_up_f32',
                              'w_ff_down_f32', 'g_ple', 'w_ple_gate_f32', 'w_ple_proj_f32')])


def _router_kernel(h_ref, gffn_ref, wr_ref, mi_ref, mf_ref, cnt_ref, xp_ref, carry_ref, *,
                   n_tok):
    g = pl.program_id(0)
    tm = h_ref.shape[0]

    @pl.when(g == 0)
    def _():
        carry_ref[...] = jnp.zeros(carry_ref.shape, F32)

    valid_row = (g * tm + lax.broadcasted_iota(I32, (tm, 1), 0)) < n_tok
    xn = _rms(jnp.where(valid_row, h_ref[...], 0.0), gffn_ref[...])
    xp_ref[...] = _pack_bf16_halves(xn)
    logits = _dot_full(xn, wr_ref[...]).T[:N_EXPERTS]
    valid = (g * tm + lax.broadcasted_iota(I32, (1, tm), 1)) < n_tok
    e = lax.broadcasted_iota(I32, logits.shape, 0)
    m1 = jnp.max(logits, axis=0, keepdims=True)
    i1 = jnp.min(jnp.where(logits == m1, e, N_EXPERTS), axis=0, keepdims=True)
    rest = jnp.where(e == i1, -jnp.inf, logits)
    m2 = jnp.max(rest, axis=0, keepdims=True)
    i2 = jnp.min(jnp.where(rest == m2, e, N_EXPERTS), axis=0, keepdims=True)
    e2 = jnp.exp(m2 - m1)
    denom = 1.0 + e2
    w1 = 1.0 / denom
    w2 = e2 / denom

    oh1 = jnp.where((e == i1) & valid, 1.0, 0.0)
    oh2 = jnp.where((e == i2) & valid, 1.0, 0.0)
    member = oh1 + oh2
    r = lax.broadcasted_iota(I32, (tm, tm), 0)
    c = lax.broadcasted_iota(I32, (tm, tm), 1)
    earlier = jnp.where(r < c, 1.0, 0.0).astype(BF16)
    pos = _dot(member, earlier) + carry_ref[...]
    pos1 = jnp.sum(oh1 * pos, axis=0, keepdims=True).astype(I32)
    pos2 = jnp.sum(oh2 * pos, axis=0, keepdims=True).astype(I32)
    carry_ref[...] = carry_ref[...] + jnp.sum(member, axis=1, keepdims=True)
    cnt_ref[...] = carry_ref[...]

    mi_ref[...] = jnp.where(e == 0, i1, jnp.where(e == 1, i2,
                            jnp.where(e == 2, pos1, jnp.where(e == 3, pos2, 0))))
    gates = jnp.where(e == 0, w1, jnp.where(e == 1, w2, 0.0))
    lanes = wr_ref.shape[1]
    gates = jnp.concatenate([gates, jnp.zeros((lanes - N_EXPERTS, tm), F32)], axis=0)
    mf_ref[...] = gates.T[:, :N_EXPERTS]


def _router(h, tile0, n_tok, lw):
    tm = TOK_ROWS
    return pl.pallas_call(
        functools.partial(_router_kernel, n_tok=n_tok),
        grid=(pl.cdiv(n_tok, tm),),
        in_specs=[pl.BlockSpec((tm, D_MODEL), lambda g: (tile0 + g, 0)),
                  _layer_spec(*lw['g_ffn']), _layer_spec(*lw['w_router'])],
        out_specs=[pl.BlockSpec((N_EXPERTS, tm), lambda g: (0, g)),
                   pl.BlockSpec((tm, N_EXPERTS), lambda g: (g, 0)),
                   _whole_out_spec((N_EXPERTS, 1)),
                   pl.BlockSpec((tm, D_MODEL // 2), lambda g: (g, 0))],
        out_shape=[jax.ShapeDtypeStruct((N_EXPERTS, n_tok), I32),
                   jax.ShapeDtypeStruct((n_tok, N_EXPERTS), F32),
                   jax.ShapeDtypeStruct((N_EXPERTS, 1), F32),
                   jax.ShapeDtypeStruct((n_tok, D_MODEL // 2), jnp.uint32)],
        scratch_shapes=[pltpu.VMEM((N_EXPERTS, 1), F32)],
        compiler_params=_tc_params(),
        name="router",
    )(h, lw['g_ffn'][0], lw['w_router'][0])


def _sc_chunk(n_rows):
    per_worker = pl.cdiv(n_rows, SC_WORKERS)
    return pl.cdiv(per_worker, SC_WINDOW) * SC_WINDOW


def _sc_worker_base(n_rows, chunk):
    wid = lax.axis_index("s") * SC_CORES + lax.axis_index("c")
    return jnp.minimum(wid * chunk, n_rows - chunk)


def _sc_scatter_rows(x, row0, dest, n_out):
    n = dest.shape[0] // TOP_K
    width = x.shape[1]
    chunk = _sc_chunk(n)
    mesh = plsc.VectorSubcoreMesh(core_axis_name="c", subcore_axis_name="s")

    @functools.partial(
        pl.kernel, mesh=mesh,
        out_type=jax.ShapeDtypeStruct((n_out, width), x.dtype),
        scratch_types=[pltpu.VMEM((SC_WINDOW,), I32) for _ in range(TOP_K)]
        + [pltpu.VMEM((SC_WINDOW, width), x.dtype), pltpu.SemaphoreType.DMA],
        name="sc_scatter_rows",
    )
    def scatter(x_hbm, dest_hbm, out_hbm, idx0_v, idx1_v, rows_v, sem):
        base = _sc_worker_base(n, chunk)

        @pl.loop(0, chunk // SC_WINDOW)
        def _(j):
            off = pl.multiple_of(base + j * SC_WINDOW, 8)
            pltpu.sync_copy(dest_hbm.at[pl.ds(off, SC_WINDOW)], idx0_v)
            pltpu.sync_copy(dest_hbm.at[pl.ds(n + off, SC_WINDOW)], idx1_v)
            pltpu.sync_copy(x_hbm.at[pl.ds(row0 + off, SC_WINDOW)], rows_v)
            first = pltpu.async_copy(rows_v, out_hbm.at[idx0_v], sem)
            second = pltpu.async_copy(rows_v, out_hbm.at[idx1_v], sem)
            first.wait()
            second.wait()

    return scatter(x, dest)


def _sc_gather_rows(y, idx):
    n = idx.shape[0]
    width = y.shape[1]
    chunk = _sc_chunk(n)
    mesh = plsc.VectorSubcoreMesh(core_axis_name="c", subcore_axis_name="s")

    @functools.partial(
        pl.kernel, mesh=mesh,
        out_type=jax.ShapeDtypeStruct((n, width), y.dtype),
        scratch_types=[pltpu.VMEM((SC_WINDOW,), I32),
                       pltpu.VMEM((SC_WINDOW, width), y.dtype), pltpu.SemaphoreType.DMA],
        name="sc_gather_rows",
    )
    def gather(y_hbm, idx_hbm, out_hbm, idx_v, rows_v, sem):
        base = _sc_worker_base(n, chunk)

        @pl.loop(0, chunk // SC_WINDOW)
        def _(j):
            off = pl.multiple_of(base + j * SC_WINDOW, 8)
            pltpu.sync_copy(idx_hbm.at[pl.ds(off, SC_WINDOW)], idx_v)
            pltpu.async_copy(y_hbm.at[idx_v], rows_v, sem).wait()
            pltpu.sync_copy(rows_v, out_hbm.at[pl.ds(off, SC_WINDOW)])

    return gather(y, idx)


def _pack_bf16_halves(y):
    half = y.shape[1] // 2
    rounded = y.astype(BF16).astype(F32)
    hi = lax.bitcast_convert_type(rounded[:, :half], jnp.uint32)
    lo = lax.bitcast_convert_type(rounded[:, half:], jnp.uint32)
    return hi | (lo >> 16)


def _unpack_bf16_halves(packed):
    hi = lax.bitcast_convert_type(packed & jnp.uint32(0xFFFF0000), F32)
    lo = lax.bitcast_convert_type(packed << 16, F32)
    return jnp.concatenate([hi, lo], axis=1)


def _expert_kernel(tile_expert_ref, n_valid_ref, xs_ref, wg_ref, wu_ref, wd_ref, y_ref):
    @pl.when(pl.program_id(0) < n_valid_ref[0])
    def _():
        xn = _unpack_bf16_halves(xs_ref[...]).astype(BF16)
        a = jax.nn.silu(_dot(xn, wg_ref[...])) * _dot(xn, wu_ref[...])
        y_ref[...] = _pack_bf16_halves(_dot(a, wd_ref[...]))


def _experts(xs, tile_expert, n_valid, lw):
    n_slots = xs.shape[0]
    te = EXPERT_ROWS
    d_exp = lw['w_ex_gate'][0].shape[-1]
    moe = lw['w_ex_gate'][1]
    row_block = lambda g, tex, nv: (jnp.minimum(g, nv[0] - 1), 0)
    w_block = lambda g, tex, nv: (moe, tex[g], 0, 0)
    return pl.pallas_call(
        _expert_kernel,
        grid_spec=pltpu.PrefetchScalarGridSpec(
            num_scalar_prefetch=2,
            grid=(n_slots // te,),
            in_specs=[pl.BlockSpec((te, xs.shape[1]), row_block),
                      pl.BlockSpec((None, None, D_MODEL, d_exp), w_block),
                      pl.BlockSpec((None, None, D_MODEL, d_exp), w_block),
                      pl.BlockSpec((None, None, d_exp, D_MODEL), w_block)],
            out_specs=pl.BlockSpec((te, D_MODEL // 2), row_block)),
        out_shape=jax.ShapeDtypeStruct((n_slots, D_MODEL // 2), jnp.uint32),
        compiler_params=_tc_params(),
        name="experts",
    )(tile_expert, n_valid, xs, *[lw[k][0] for k in ('w_ex_gate', 'w_ex_up', 'w_ex_down')])


def _combine_math(h_ref, yg_ref, mf_ref, p, gple_ref, wpg_ref, wpp_ref, gfin_ref):
    gates = mf_ref[...]
    h = h_ref[...] + (gates[:, 0:1] * _unpack_bf16_halves(yg_ref[0])
                      + gates[:, 1:2] * _unpack_bf16_halves(yg_ref[1]))
    return _rms(_ple(h, p, gple_ref, wpg_ref, wpp_ref, _dot), gfin_ref[...])


def _combine_seq_kernel(h_ref, yg_ref, mf_ref, pp_ref, gple_ref, wpg_ref, wpp_ref, gfin_ref,
                        ybuf_ref, yp_ref):
    del ybuf_ref
    yp_ref[...] = _combine_math(h_ref, yg_ref, mf_ref, pp_ref[...], gple_ref, wpg_ref,
                                wpp_ref, gfin_ref)


def _combine_tail_kernel(h_ref, yg_ref, mf_ref, pp_ref, ps_ref, gple_ref, wpg_ref, wpp_ref,
                         gfin_ref, ybuf_ref, yp_ref, ys_ref, pbuf_ref):
    del ybuf_ref
    g = pl.program_id(0)
    last = pl.num_programs(0) - 1
    n_dec = ps_ref.shape[0]
    pbuf_ref[...] = pp_ref[...]

    @pl.when(g == last)
    def _():
        pbuf_ref[0:n_dec, :] = ps_ref[...]

    out = _combine_math(h_ref, yg_ref, mf_ref, pbuf_ref[...], gple_ref, wpg_ref, wpp_ref,
                        gfin_ref)

    @pl.when(g < last)
    def _():
        yp_ref[...] = out

    @pl.when(g == last)
    def _():
        ys_ref[...] = out[0:n_dec, :]


def _combine(h, tile0, yg, mf, y_seq, lw, with_rows):
    tm = COMBINE_ROWS
    assert tile0 * TOK_ROWS % tm == 0
    tile0 = tile0 * TOK_ROWS // tm
    n_tiles = pl.cdiv(yg.shape[1], tm)
    last_seq_tile = y_seq.shape[0] // tm - 1
    pp, layer = lw['p_seq']
    n_dec = lw['p_row'][0].shape[-2]
    weights = [lw[k] for k in (('p_row',) if with_rows else ())
               + ('g_ple', 'w_ple_gate', 'w_ple_proj', 'g_final')]
    seq_tile = lambda g: (jnp.minimum(tile0 + g, last_seq_tile), 0)
    n_in = 4 + len(weights)
    out = pl.pallas_call(
        _combine_tail_kernel if with_rows else _combine_seq_kernel,
        grid=(n_tiles,),
        in_specs=[pl.BlockSpec((tm, D_MODEL), lambda g: (tile0 + g, 0)),
                  pl.BlockSpec((TOP_K, tm, yg.shape[2]), lambda g: (0, g, 0)),
                  pl.BlockSpec((tm, N_EXPERTS), lambda g: (g, 0)),
                  pl.BlockSpec((None, tm, D_PLE), lambda g: (layer,) + seq_tile(g))]
        + [_layer_spec(*w) for w in weights]
        + [pl.BlockSpec(memory_space=pl.ANY)],
        out_specs=[pl.BlockSpec((tm, D_MODEL), seq_tile)]
        + ([_whole_out_spec((n_dec, D_MODEL))] if with_rows else []),
        out_shape=[jax.ShapeDtypeStruct(y_seq.shape, F32)]
        + ([jax.ShapeDtypeStruct((n_dec, D_MODEL), F32)] if with_rows else []),
        scratch_shapes=[pltpu.VMEM((tm, D_PLE), F32)] if with_rows else [],
        input_output_aliases={n_in: 0},
        compiler_params=_tc_params(),
        name="combine",
    )(h, yg, mf, pp, *[w for w, _ in weights], y_seq)
    return out if with_rows else (out[0], None)


def _moe_layer(h, y_seq, lw):
    n_tok = h.shape[0]
    tm = TOK_ROWS
    te = EXPERT_ROWS
    tiles = pl.cdiv(n_tok, tm)
    bounds = [tiles * c // MOE_CHUNKS for c in range(MOE_CHUNKS + 1)]
    y_rows = None
    for c in range(MOE_CHUNKS):
        tile0 = bounds[c]
        n = min(bounds[c + 1] * tm, n_tok) - tile0 * tm
        mi, mf, counts, xp = _router(h, tile0, n, lw)

        cnt = counts[:, 0].astype(I32)
        padded = (cnt + te - 1) // te * te
        ends = jnp.cumsum(padded)
        starts = ends - padded
        experts = jnp.arange(N_EXPERTS, dtype=I32)
        start_of = lambda e: jnp.sum(
            jnp.where(e[None, :] == experts[:, None], starts[:, None], 0), axis=0)
        dest = jnp.concatenate([start_of(mi[0]) + mi[2], start_of(mi[1]) + mi[3]])
        dest, h = lax.optimization_barrier((dest, h))
        n_tiles = pl.cdiv(TOP_K * n + N_EXPERTS * (te - 1), te)
        tile_start = jnp.arange(n_tiles, dtype=I32) * te
        last_used = jnp.max(jnp.where(padded > 0, experts, 0))
        tile_expert = jnp.minimum(
            jnp.sum(tile_start[:, None] >= ends[None, :], axis=-1).astype(I32), last_used)
        n_valid = (ends[-1:] // te).astype(I32)

        xs = _sc_scatter_rows(xp, 0, dest, n_tiles * te)
        y = _experts(xs, tile_expert, n_valid, lw)
        yg = _sc_gather_rows(y, dest).reshape(TOP_K, n, y.shape[1])
        y_seq, rows = _combine(h, tile0, yg, mf, y_seq, lw, with_rows=c == MOE_CHUNKS - 1)
        y_rows = rows if rows is not None else y_rows
    return y_seq, y_rows


def kernel(x_prompt, x_sample, state_conv_a, state_conv_c, p_prompt, p_sample, g_mix, w_in, w_conv_a, w_s, b_s, g_ln_b, b_ln_b, w_conv_c, b_conv_c, g_ln_c, b_ln_c, g_out, w_o, g_ffn, w_ff_gate, w_ff_up, w_ff_down, w_router, w_ex_gate, w_ex_up, w_ex_down, g_ple, w_ple_gate, w_ple_proj, g_final):
    depth = g_mix.shape[0]
    n_seq, seq, _ = x_prompt.shape
    n_dec = x_sample.shape[0]
    n_prompt = n_seq * seq
    assert depth == 2 and x_sample.shape[1] == 1
    assert seq % MIXER_ROWS == 0 and TOK_ROWS % n_dec == 0 and w_ff_gate.shape[-1] % FF_COLS == 0

    vectors = {'g_mix': g_mix, 'g_out': g_out, 'g_ffn': g_ffn, 'g_ple': g_ple,
               'b_conv_c': b_conv_c, 'g_ln_c': g_ln_c, 'b_ln_c': b_ln_c,
               'g_ln_b': g_ln_b, 'b_ln_b': b_ln_b}
    pieces, lanes, total = [], {}, 0
    for name, x in sorted(vectors.items(), key=lambda kv: -kv[1].shape[1]):
        n = x.shape[1]
        gap = -total % n
        pieces += [jnp.zeros((depth, gap), F32)] * (gap > 0) + [x]
        lanes[name] = (total + gap, n)
        total += gap + n
    packed = jnp.concatenate(pieces, axis=1).reshape(depth, 1, total)
    per_layer = {
        'w_in_f32': w_in, 'w_conv_a': w_conv_a,
        'w_s_cat': jnp.transpose(w_s, (0, 2, 1, 3)).reshape(depth, CHUNK, N_HEADS_B * CHUNK),
        'b_s_full': jnp.repeat(jnp.swapaxes(b_s, 1, 2), HEAD_DIM, axis=2),
        'w_conv_c': w_conv_c, 'w_o_f32': w_o,
        'w_ple_gate_f32': w_ple_gate, 'w_ple_proj_f32': w_ple_proj,
        'state_a': jnp.swapaxes(state_conv_a, 1, 2), 'state_c': jnp.swapaxes(state_conv_c, 1, 2),
        'p_seq': p_prompt.reshape(depth, n_prompt, D_PLE),
        'p_row': p_sample.reshape(depth, n_dec, D_PLE),
    }
    per_dense = {'w_ff_gate_f32': w_ff_gate, 'w_ff_up_f32': w_ff_up, 'w_ff_down_f32': w_ff_down}
    one_layer = lambda w: (w.reshape((1,) + w.shape), 0)
    per_moe = {'w_router': jnp.pad(w_router, ((0, 0), (0, 0), (0, LANES - N_EXPERTS)))}
    to_convert = [('w_ex_gate', w_ex_gate), ('w_ex_up', w_ex_up), ('w_ex_down', w_ex_down)]

    def with_ride(call):
        name, w = to_convert.pop(0)
        *outputs, w_bf16 = call(w.reshape(-1, w.shape[-1]))
        per_moe[name] = w_bf16.reshape(w.shape)
        return outputs

    hp = x_prompt.reshape(n_prompt, D_MODEL)
    hs = x_sample.reshape(n_dec, D_MODEL)
    outs = {'a_p': [], 'c_p': []}
    row_states = None
    for i in range(depth):
        is_expert_layer = i % 2 == 1
        lw = {k: (v, i) for k, v in per_layer.items()}
        lw.update({k: (packed, _Lanes(i, off, n)) for k, (off, n) in lanes.items()})
        lw.update({k: (v, i // 2) for k, v in (per_dense if not is_expert_layer else {}).items()})
        lw['g_final'] = (g_final.reshape(1, 1, -1), 0)

        hs, *row_states, w0, w1, w2, w3 = _mixer_row(hs, 0, lw, row_states)
        lw.update({k: one_layer(w) for k, w in zip(CONVERTED_BY_MIXER_ROW, (w0, w1, w2, w3))})
        mixer_in = hp
        hp, a_p, c_p = with_ride(functools.partial(
            _mixer_seq, hp, hs if is_expert_layer else None, lw, n_seq, seq))
        outs['a_p'].append(a_p)
        outs['c_p'].append(c_p)
        if not is_expert_layer:
            hs, *converted = _ffn_row(hs, lw)
            lw.update({k: one_layer(w) for k, w in
                       zip(('w_ff_gate', 'w_ff_up', 'w_ff_down'), converted)})
            hp, = with_ride(functools.partial(_ffn_dense, hp, lw))
        else:
            lw.update({k: (v, i // 2) for k, v in per_moe.items()})
            y_prompt, y_sample = _moe_layer(hp, mixer_in, lw)

    a_s, c_s, v_s = row_states
    return (y_prompt.reshape(x_prompt.shape), y_sample.reshape(x_sample.shape),
            jnp.stack(outs['a_p']), a_s, jnp.stack(outs['c_p']), c_s,
            v_s.reshape(depth, n_dec, 1, D_B))
```

```python
import collections
import functools

import jax
import jax.numpy as jnp
from jax import lax
from jax.experimental import pallas as pl
from jax.experimental.pallas import tpu as pltpu
from jax.experimental.pallas import tpu_sc as plsc

D_MODEL = 1024
HEAD_DIM = 64
D_A = 384
D_B = 256
D_C = 384
N_HEADS_B = D_B // HEAD_DIM
CONV_A = 3
CONV_C = 31
CHUNK = 128
D_IN = 3 * D_A + 2 * D_B + 2 * D_C
D_PLE = 256
N_EXPERTS = 8
TOP_K = 2
EPS = 1e-6

O1, O2, O3 = D_A, 2 * D_A, 3 * D_A
O4 = O3 + 2 * D_B

F32 = jnp.float32
BF16 = jnp.bfloat16
I32 = jnp.int32

VMEM_LIMIT_BYTES = 56 * 1024 * 1024
SUBLANES = 8
LANES = 128

TOK_ROWS = 512
MIXER_ROWS = 1024
COMBINE_ROWS = 1024
CONV_ROWS = 64
A_HALO = 8
C_HALO = 32
EXPERT_ROWS = 512
FF_COLS = 256
MOE_CHUNKS = 2

SC_CORES = 2
SC_WORKERS = 32
SC_WINDOW = 48


def _rms(x, g):
    return x * lax.rsqrt(jnp.mean(x * x, axis=-1, keepdims=True) + EPS) * g


def _ln(x, g, b):
    mu = jnp.mean(x, axis=-1, keepdims=True)
    xc = x - mu
    var = jnp.mean(xc * xc, axis=-1, keepdims=True)
    return xc * lax.rsqrt(var + EPS) * g + b


def _gelu_tanh(x):
    c = 2.0 * (2.0 / jnp.pi) ** 0.5
    return x * jax.nn.sigmoid(x * (c + (c * 0.044715) * (x * x)))


def _dot(a, b):
    return jnp.dot(a.astype(BF16), b, preferred_element_type=F32)


def _split(x):
    hi = x.astype(BF16)
    return hi, (x - hi.astype(F32)).astype(BF16)


def _dot_full(a, b):
    a_hi, a_lo = _split(a)
    b_hi, b_lo = _split(b)
    m = a.shape[0]
    by_hi = jnp.dot(jnp.concatenate([a_hi, a_lo], axis=0), b_hi, preferred_element_type=F32)
    return by_hi[:m] + by_hi[m:] + jnp.dot(a_hi, b_lo, preferred_element_type=F32)


def _const_spec(shape):
    return pl.BlockSpec(shape, lambda *_: (0,) * len(shape), pipeline_mode=pl.Buffered(1))


_Lanes = collections.namedtuple('_Lanes', 'layer offset size')


def _layer_spec(stacked, layer):
    if isinstance(layer, _Lanes):
        where = (layer.layer, 0, layer.offset // layer.size)
        return pl.BlockSpec((None, 1, layer.size), lambda *_: where,
                            pipeline_mode=pl.Buffered(1))
    idx = (layer,) if isinstance(layer, int) else tuple(layer)
    rest = stacked.shape[len(idx):]
    return pl.BlockSpec((None,) * len(idx) + rest, lambda *_: idx + (0,) * len(rest),
                        pipeline_mode=pl.Buffered(1))


def _whole_out_spec(shape):
    return pl.BlockSpec(shape, lambda *_: (0,) * len(shape))


def _tc_params():
    return pltpu.CompilerParams(dimension_semantics=("arbitrary",),
                                vmem_limit_bytes=VMEM_LIMIT_BYTES)


def _normed_groups(y_a, y_b, y_c, gout_ref):
    return jnp.concatenate([_rms(y_a, gout_ref[:, :D_A]),
                            _rms(y_b, gout_ref[:, D_A:D_A + D_B]),
                            _rms(y_c, gout_ref[:, D_A + D_B:])], axis=-1)


def _mixer_seq_kernel(hp_ref, hs_ref, gmix_ref, win_ref, wca_ref, wscat_ref, bsfull_ref,
                      glnb_ref, blnb_ref, wcc_ref, bcc_ref, glnc_ref, blnc_ref, gout_ref,
                      wo_ref, ride_ref, hout_ref, newa_ref, newc_ref, ride_out_ref,
                      qext_ref, gext_ref, gshift_ref, conv_ref, wcc8_ref, *, tiles_per_seq,
                      n_seq_tiles):
    g = pl.program_id(0)
    tm = hp_ref.shape[0]

    @pl.when(g == 0)
    def _():
        for k in range(CONV_C):
            wcc8_ref[k] = jnp.broadcast_to(wcc_ref[k:k + 1, :], (SUBLANES, D_C))

    @pl.when(g < n_seq_tiles)
    def _sequence_tile():
        ride_out_ref[...] = ride_ref[...].astype(BF16)

        @pl.when(g % tiles_per_seq == 0)
        def _():
            qext_ref[0:A_HALO, :] = jnp.zeros((A_HALO, D_A), F32)
            gext_ref[0:C_HALO, :] = jnp.zeros((C_HALO, D_C), F32)

        h = hp_ref[...]
        z = _dot(_rms(h, gmix_ref[...]), win_ref[...])

        qext_ref[A_HALO:A_HALO + tm, :] = z[:, O1:O2] * z[:, O2:O3]
        conv_a = jnp.zeros((tm, D_A), F32)
        for k in range(CONV_A):
            off = A_HALO - (CONV_A - 1) + k
            conv_a = conv_a + wca_ref[k:k + 1, :] * qext_ref[off:off + tm, :]
        y_a = z[:, :O1] * conv_a
        last_q = qext_ref[A_HALO + tm - (CONV_A - 1):A_HALO + tm, :]
        newa_ref[...] = last_q
        qext_ref[A_HALO - (CONV_A - 1):A_HALO, :] = last_q

        zb = _gelu_tanh(z[:, O3:O4])
        v = _ln(zb[:, D_B:], glnb_ref[...], blnb_ref[...])
        row = lax.broadcasted_iota(I32, (CHUNK, N_HEADS_B * CHUNK), 0)
        col = lax.broadcasted_iota(I32, (CHUNK, N_HEADS_B * CHUNK), 1)
        w_tril = jnp.where((col % CHUNK) <= row, wscat_ref[...], 0.0).astype(BF16)
        lane_head = lax.broadcasted_iota(I32, (CHUNK, D_B), 1) // HEAD_DIM
        s_chunks = []
        for c in range(tm // CHUNK):
            vc = v[c * CHUNK:(c + 1) * CHUNK, :]
            vstack = jnp.concatenate(
                [jnp.where(lane_head == hd, vc, 0.0) for hd in range(N_HEADS_B)], axis=0)
            s_chunks.append(_dot(w_tril, vstack.astype(BF16)) + bsfull_ref[...])
        y_b = zb[:, :D_B] * jnp.concatenate(s_chunks, axis=0)

        gext_ref[C_HALO:C_HALO + tm, :] = (z[:, O4:O4 + D_C]
                                           * jax.nn.sigmoid(z[:, O4 + D_C:]))
        n_shift = gshift_ref.shape[1]
        for s in range(1, SUBLANES):
            gshift_ref[s - 1] = gext_ref[s:s + n_shift, :]
        base = C_HALO - (CONV_C - 1)
        for r0 in range(0, tm, CONV_ROWS):
            acc = jnp.zeros((CONV_ROWS // SUBLANES, SUBLANES, D_C), F32)
            for k in range(CONV_C):
                lo = (base + k) // SUBLANES * SUBLANES + r0
                s = (base + k) % SUBLANES
                window = (gext_ref[lo:lo + CONV_ROWS, :] if s == 0
                          else gshift_ref[s - 1, lo:lo + CONV_ROWS, :])
                acc = acc + wcc8_ref[k][None] * window.reshape(acc.shape)
            conv_ref[r0:r0 + CONV_ROWS, :] = acc.reshape(CONV_ROWS, D_C)
        y_c = jax.nn.silu(_ln(conv_ref[...] + bcc_ref[...], glnc_ref[...], blnc_ref[...]))
        last_g = gext_ref[C_HALO + tm - (CONV_C - 1):C_HALO + tm, :]
        newc_ref[...] = last_g
        gext_ref[C_HALO - (CONV_C - 1):C_HALO, :] = last_g

        hout_ref[...] = h + _dot(_normed_groups(y_a, y_b, y_c, gout_ref), wo_ref[...])

    @pl.when(g == n_seq_tiles)
    def _append_sample_rows():
        hout_ref[0:hs_ref.shape[0], :] = hs_ref[...]


def _ride_specs(ride, n_steps):
    rows = ride.shape[0] // n_steps
    assert rows * n_steps == ride.shape[0]
    slab = lambda g: (jnp.minimum(g, n_steps - 1), 0)
    return (pl.BlockSpec((rows, ride.shape[1]), slab), pl.BlockSpec((rows, ride.shape[1]), slab),
            jax.ShapeDtypeStruct(ride.shape, BF16))


def _mixer_seq(hp, hs, lw, n_seq, seq, ride):
    tm = MIXER_ROWS
    tiles_per_seq = seq // tm
    n_seq_tiles = n_seq * tiles_per_seq
    join = hs is not None
    if not join:
        hs = jnp.zeros((SUBLANES, D_MODEL), F32)
    n_out = n_seq * seq + (hs.shape[0] if join else 0)
    weights = [lw[k] for k in ('g_mix', 'w_in', 'w_conv_a', 'w_s_cat', 'b_s_full', 'g_ln_b',
                               'b_ln_b', 'w_conv_c', 'b_conv_c', 'g_ln_c', 'b_ln_c', 'g_out',
                               'w_o')]
    seq_of = lambda g: jnp.minimum(g // tiles_per_seq, n_seq - 1)
    n_shift = tm + C_HALO - SUBLANES
    ride_in_spec, ride_out_spec, ride_out_shape = _ride_specs(ride, n_seq_tiles)
    return pl.pallas_call(
        functools.partial(_mixer_seq_kernel, tiles_per_seq=tiles_per_seq,
                          n_seq_tiles=n_seq_tiles),
        grid=(n_seq_tiles + int(join),),
        in_specs=[pl.BlockSpec((tm, D_MODEL), lambda g: (jnp.minimum(g, n_seq_tiles - 1), 0)),
                  _const_spec(hs.shape)]
        + [_layer_spec(*w) for w in weights] + [ride_in_spec],
        out_specs=[pl.BlockSpec((tm, D_MODEL), lambda g: (g, 0)),
                   pl.BlockSpec((None, CONV_A - 1, D_A), lambda g: (seq_of(g), 0, 0)),
                   pl.BlockSpec((None, CONV_C - 1, D_C), lambda g: (seq_of(g), 0, 0)),
                   ride_out_spec],
        out_shape=[jax.ShapeDtypeStruct((n_out, D_MODEL), F32),
                   jax.ShapeDtypeStruct((n_seq, CONV_A - 1, D_A), F32),
                   jax.ShapeDtypeStruct((n_seq, CONV_C - 1, D_C), F32),
                   ride_out_shape],
        scratch_shapes=[pltpu.VMEM((A_HALO + tm, D_A), F32),
                        pltpu.VMEM((C_HALO + tm, D_C), F32),
                        pltpu.VMEM((SUBLANES - 1, n_shift, D_C), F32),
                        pltpu.VMEM((tm, D_C), F32),
                        pltpu.VMEM((CONV_C, SUBLANES, D_C), F32)],
        compiler_params=_tc_params(),
        name="mixer_seq",
    )(hp, hs, *[w for w, _ in weights], ride)


def _mixer_row_kernel(*refs, n_carried, layer):
    (h_ref, sa_ref, sc_hbm, gmix_ref, win_hbm, wca_ref, wscat_ref, bsfull_ref, glnb_ref,
     blnb_ref, wcc_ref, bcc_ref, glnc_ref, blnc_ref, gout_ref, wo_hbm, wpg_hbm,
     wpp_hbm) = refs[:18]
    (hout_ref, newa_ref, newc_ref, v_ref, win_bf_hbm, wo_bf_hbm, wpg_bf_hbm, wpp_bf_hbm,
     sc_ref, win_ref, wo_ref, wpg_ref, wpp_ref, win_bf_ref, wo_bf_ref, wpg_bf_ref,
     wpp_bf_ref, in_sem, out_sem) = refs[18 + n_carried:]

    @pl.when(pl.program_id(0) > 0)
    def _():
        for ref in (newa_ref, newc_ref, v_ref):
            ref[...] = jnp.zeros(ref.shape, F32)

    @pl.when(pl.program_id(0) == 0)
    def _():
        fetched = {}
        for i, (name, src, dst) in enumerate((('w_in', win_hbm, win_ref),
                                              ('state_c', sc_hbm, sc_ref),
                                              ('w_o', wo_hbm, wo_ref),
                                              ('w_ple_gate', wpg_hbm, wpg_ref),
                                              ('w_ple_proj', wpp_hbm, wpp_ref))):
            fetched[name] = pltpu.make_async_copy(src.at[layer], dst, in_sem.at[i])
            fetched[name].start()
        stored = []

        def ready(name):
            fetched[name].wait()
            if name == 'state_c':
                return
            i = CONVERTED_BY_MIXER_ROW.index(name)
            f32_ref, bf_ref, bf_hbm = ((win_ref, win_bf_ref, win_bf_hbm),
                                       (wo_ref, wo_bf_ref, wo_bf_hbm),
                                       (wpg_ref, wpg_bf_ref, wpg_bf_hbm),
                                       (wpp_ref, wpp_bf_ref, wpp_bf_hbm))[i]
            bf_ref[...] = f32_ref[...].astype(BF16)
            stored.append(pltpu.make_async_copy(bf_ref, bf_hbm, out_sem.at[i]))
            stored[-1].start()

        _mixer_row_body(h_ref, sa_ref, sc_ref, gmix_ref, win_ref, wca_ref, wscat_ref,
                        bsfull_ref, glnb_ref, blnb_ref, wcc_ref, bcc_ref, glnc_ref, blnc_ref,
                        gout_ref, wo_ref, hout_ref, newa_ref, newc_ref, v_ref, ready)
        ready('w_ple_gate')
        ready('w_ple_proj')
        for copy in stored:
            copy.wait()


CONVERTED_BY_MIXER_ROW = ('w_in', 'w_o', 'w_ple_gate', 'w_ple_proj')


def _mixer_row_body(h_ref, sa_ref, sc_ref, gmix_ref, win_ref, wca_ref, wscat_ref,
                    bsfull_ref, glnb_ref, blnb_ref, wcc_ref, bcc_ref, glnc_ref, blnc_ref,
                    gout_ref, wo_ref, hout_ref, newa_ref, newc_ref, v_ref, ready):
    h = h_ref[...]
    xn = _rms(h, gmix_ref[...])
    ready('w_in')
    z = _dot_full(xn, win_ref[...])

    q = z[:, O1:O2] * z[:, O2:O3]
    conv_a = wca_ref[CONV_A - 1:CONV_A, :] * q
    for k in range(CONV_A - 1):
        conv_a = conv_a + wca_ref[k:k + 1, :] * sa_ref[k]
    y_a = z[:, :O1] * conv_a
    for k in range(CONV_A - 2):
        newa_ref[:, k, :] = sa_ref[k + 1]
    newa_ref[:, CONV_A - 2, :] = q

    zb = _gelu_tanh(z[:, O3:O4])
    v = _ln(zb[:, D_B:], glnb_ref[...], blnb_ref[...])
    v_ref[...] = v
    w_diag0 = jnp.concatenate(
        [jnp.broadcast_to(wscat_ref[0:1, hd * CHUNK:hd * CHUNK + 1], (1, HEAD_DIM))
         for hd in range(N_HEADS_B)], axis=-1)
    y_b = zb[:, :D_B] * (w_diag0 * v + bsfull_ref[0:1, :])

    glu = z[:, O4:O4 + D_C] * jax.nn.sigmoid(z[:, O4 + D_C:])
    conv_c = wcc_ref[CONV_C - 1:CONV_C, :] * glu
    ready('state_c')
    for k in range(CONV_C - 1):
        conv_c = conv_c + wcc_ref[k:k + 1, :] * sc_ref[k]
    y_c = jax.nn.silu(_ln(conv_c + bcc_ref[...], glnc_ref[...], blnc_ref[...]))
    for k in range(CONV_C - 2):
        newc_ref[:, k, :] = sc_ref[k + 1]
    newc_ref[:, CONV_C - 2, :] = glu

    y = _normed_groups(y_a, y_b, y_c, gout_ref)
    ready('w_o')
    hout_ref[...] = h + _dot_full(y, wo_ref[...])


def _mixer_row(h, h_block, lw, states):
    depth, _, n_dec, _ = lw['state_a'][0].shape
    layer = lw['state_a'][1]
    weights = [lw[k] for k in ('state_a', 'state_c', 'g_mix', 'w_in_f32', 'w_conv_a', 'w_s_cat',
                               'b_s_full', 'g_ln_b', 'b_ln_b', 'w_conv_c', 'b_conv_c', 'g_ln_c',
                               'b_ln_c', 'g_out', 'w_o_f32', 'w_ple_gate_f32', 'w_ple_proj_f32')]
    bf16_shapes = [lw[k + '_f32'][0].shape[1:] for k in CONVERTED_BY_MIXER_ROW]
    state_shapes = [(n_dec, CONV_A - 1, D_A), (n_dec, CONV_C - 1, D_C), (n_dec, D_B)]
    carried = list(states or ())
    first = 1 + len(weights)
    block_of = (lambda g: g) if states is None else (lambda g: layer)
    by_hand = ('state_c', 'w_in_f32', 'w_o_f32', 'w_ple_gate_f32', 'w_ple_proj_f32')
    landing = [lw[k][0].shape[1:] for k in ('state_c',) + tuple(
        k + '_f32' for k in CONVERTED_BY_MIXER_ROW)]
    in_hbm = pl.BlockSpec(memory_space=pl.ANY)
    names = ('state_a', 'state_c', 'g_mix', 'w_in_f32', 'w_conv_a', 'w_s_cat', 'b_s_full',
             'g_ln_b', 'b_ln_b', 'w_conv_c', 'b_conv_c', 'g_ln_c', 'b_ln_c', 'g_out', 'w_o_f32',
             'w_ple_gate_f32', 'w_ple_proj_f32')
    return pl.pallas_call(
        functools.partial(_mixer_row_kernel, n_carried=len(carried), layer=layer),
        grid=(depth if states is None else 1,),
        in_specs=[pl.BlockSpec((n_dec, D_MODEL), lambda g: (h_block, 0))]
        + [in_hbm if k in by_hand else _layer_spec(*lw[k]) for k in names]
        + [in_hbm for _ in carried],
        out_specs=[_whole_out_spec((n_dec, D_MODEL))]
        + [pl.BlockSpec((None,) + s, lambda g, s=s: (block_of(g),) + (0,) * len(s))
           for s in state_shapes]
        + [in_hbm for _ in bf16_shapes],
        out_shape=[jax.ShapeDtypeStruct((n_dec, D_MODEL), F32)]
        + [jax.ShapeDtypeStruct((depth,) + s, F32) for s in state_shapes]
        + [jax.ShapeDtypeStruct(s, BF16) for s in bf16_shapes],
        scratch_shapes=[pltpu.VMEM(s, F32) for s in landing]
        + [pltpu.VMEM(s, BF16) for s in bf16_shapes]
        + [pltpu.SemaphoreType.DMA((len(landing),)),
           pltpu.SemaphoreType.DMA((len(bf16_shapes),))],
        input_output_aliases={first + j: 1 + j for j in range(len(carried))},
        compiler_params=_tc_params(),
        name="mixer_row",
    )(h, *[w for w, _ in weights], *carried)


def _ple(h, p, gple_ref, wpg_ref, wpp_ref, dot):
    gate = jax.nn.sigmoid(dot(_rms(h, gple_ref[...]), wpg_ref[...]))
    return h + gate * dot(p, wpp_ref[...])


def _ffn_dense_kernel(h_ref, p_ref, gffn_ref, wg_ref, wu_ref, wd_ref, gple_ref, wpg_ref,
                      wpp_ref, ride_ref, out_ref, ride_out_ref):
    ride_out_ref[...] = ride_ref[...].astype(BF16)
    h = h_ref[...]
    xn = _rms(h, gffn_ref[...]).astype(BF16)
    a = jax.nn.silu(_dot(xn, wg_ref[...])) * _dot(xn, wu_ref[...])
    h = h + _dot(a, wd_ref[...])
    out_ref[...] = _ple(h, p_ref[...], gple_ref, wpg_ref, wpp_ref, _dot)


def _ffn_dense(h, lw, ride):
    n_tok = h.shape[0]
    tm = TOK_ROWS
    p, layer = lw['p_seq']
    weights = [lw[k] for k in ('g_ffn', 'w_ff_gate', 'w_ff_up', 'w_ff_down', 'g_ple',
                               'w_ple_gate', 'w_ple_proj')]
    ride_in_spec, ride_out_spec, ride_out_shape = _ride_specs(ride, n_tok // tm)
    return pl.pallas_call(
        _ffn_dense_kernel,
        grid=(n_tok // tm,),
        in_specs=[pl.BlockSpec((tm, D_MODEL), lambda g: (g, 0)),
                  pl.BlockSpec((None, tm, D_PLE), lambda g: (layer, g, 0))]
        + [_layer_spec(*w) for w in weights] + [ride_in_spec],
        out_specs=[pl.BlockSpec((tm, D_MODEL), lambda g: (g, 0)), ride_out_spec],
        out_shape=[jax.ShapeDtypeStruct(h.shape, F32), ride_out_shape],
        compiler_params=_tc_params(),
        name="ffn_dense",
    )(h, p, *[w for w, _ in weights], ride)


def _ffn_row_kernel(h_ref, p_ref, gffn_ref, wg_ref, wu_ref, wd_ref, gple_ref, wpg_ref,
                    wpp_ref, out_ref, wg_bf_ref, wu_bf_ref, wd_bf_ref, xn_ref, acc_ref):
    j = pl.program_id(0)

    @pl.when(j == 0)
    def _():
        h = h_ref[...]
        xn_ref[...] = _rms(h, gffn_ref[...])
        acc_ref[...] = h

    for src, dst in ((wg_ref, wg_bf_ref), (wu_ref, wu_bf_ref), (wd_ref, wd_bf_ref)):
        dst[...] = src[...].astype(BF16)

    xn = xn_ref[...]
    a = jax.nn.silu(_dot_full(xn, wg_ref[...])) * _dot_full(xn, wu_ref[...])
    acc_ref[...] += _dot_full(a, wd_ref[...])

    @pl.when(j == pl.num_programs(0) - 1)
    def _():
        out_ref[...] = _ple(acc_ref[...], p_ref[...], gple_ref, wpg_ref, wpp_ref, _dot_full)


def _ffn_row(h, lw):
    n_dec = h.shape[0]
    d_ff = lw['w_ff_gate_f32'][0].shape[-1]
    ff = lw['w_ff_gate_f32'][1]
    return pl.pallas_call(
        _ffn_row_kernel,
        grid=(d_ff // FF_COLS,),
        in_specs=[_const_spec(h.shape), _layer_spec(*lw['p_row']), _layer_spec(*lw['g_ffn']),
                  pl.BlockSpec((None, D_MODEL, FF_COLS), lambda j: (ff, 0, j)),
                  pl.BlockSpec((None, D_MODEL, FF_COLS), lambda j: (ff, 0, j)),
                  pl.BlockSpec((None, FF_COLS, D_MODEL), lambda j: (ff, j, 0)),
                  _layer_spec(*lw['g_ple']), _layer_spec(*lw['w_ple_gate_f32']),
                  _layer_spec(*lw['w_ple_proj_f32'])],
        out_specs=[_whole_out_spec(h.shape),
                   pl.BlockSpec((D_MODEL, FF_COLS), lambda j: (0, j)),
                   pl.BlockSpec((D_MODEL, FF_COLS), lambda j: (0, j)),
                   pl.BlockSpec((FF_COLS, D_MODEL), lambda j: (j, 0))],
        out_shape=[jax.ShapeDtypeStruct(h.shape, F32),
                   jax.ShapeDtypeStruct((D_MODEL, d_ff), BF16),
                   jax.ShapeDtypeStruct((D_MODEL, d_ff), BF16),
                   jax.ShapeDtypeStruct((d_ff, D_MODEL), BF16)],
        scratch_shapes=[pltpu.VMEM((n_dec, D_MODEL), F32), pltpu.VMEM((n_dec, D_MODEL), F32)],
        compiler_params=_tc_params(),
        name="ffn_row",
    )(h, *[lw[k][0] for k in ('p_row', 'g_ffn', 'w_ff_gate_f32', 'w_ff_up_f32',
                              'w_ff_down_f32', 'g_ple', 'w_ple_gate_f32', 'w_ple_proj_f32')])


def _router_kernel(h_ref, gffn_ref, wr_ref, mi_ref, mf_ref, cnt_ref, xp_ref, carry_ref, *,
                   n_tok):
    g = pl.program_id(0)
    tm = h_ref.shape[0]

    @pl.when(g == 0)
    def _():
        carry_ref[...] = jnp.zeros(carry_ref.shape, F32)

    valid_row = (g * tm + lax.broadcasted_iota(I32, (tm, 1), 0)) < n_tok
    xn = _rms(jnp.where(valid_row, h_ref[...], 0.0), gffn_ref[...])
    xp_ref[...] = _pack_bf16_halves(xn)
    logits = _dot_full(xn, wr_ref[...]).T[:N_EXPERTS]
    valid = (g * tm + lax.broadcasted_iota(I32, (1, tm), 1)) < n_tok
    e = lax.broadcasted_iota(I32, logits.shape, 0)
    m1 = jnp.max(logits, axis=0, keepdims=True)
    i1 = jnp.min(jnp.where(logits == m1, e, N_EXPERTS), axis=0, keepdims=True)
    rest = jnp.where(e == i1, -jnp.inf, logits)
    m2 = jnp.max(rest, axis=0, keepdims=True)
    i2 = jnp.min(jnp.where(rest == m2, e, N_EXPERTS), axis=0, keepdims=True)
    e2 = jnp.exp(m2 - m1)
    denom = 1.0 + e2
    w1 = 1.0 / denom
    w2 = e2 / denom

    oh1 = jnp.where((e == i1) & valid, 1.0, 0.0)
    oh2 = jnp.where((e == i2) & valid, 1.0, 0.0)
    member = oh1 + oh2
    r = lax.broadcasted_iota(I32, (tm, tm), 0)
    c = lax.broadcasted_iota(I32, (tm, tm), 1)
    earlier = jnp.where(r < c, 1.0, 0.0).astype(BF16)
    pos = _dot(member, earlier) + carry_ref[...]
    pos1 = jnp.sum(oh1 * pos, axis=0, keepdims=True).astype(I32)
    pos2 = jnp.sum(oh2 * pos, axis=0, keepdims=True).astype(I32)
    carry_ref[...] = carry_ref[...] + jnp.sum(member, axis=1, keepdims=True)
    cnt_ref[...] = carry_ref[...]

    mi_ref[...] = jnp.where(e == 0, i1, jnp.where(e == 1, i2,
                            jnp.where(e == 2, pos1, jnp.where(e == 3, pos2, 0))))
    gates = jnp.where(e == 0, w1, jnp.where(e == 1, w2, 0.0))
    lanes = wr_ref.shape[1]
    gates = jnp.concatenate([gates, jnp.zeros((lanes - N_EXPERTS, tm), F32)], axis=0)
    mf_ref[...] = gates.T[:, :N_EXPERTS]


def _router(h, tile0, n_tok, lw):
    tm = TOK_ROWS
    return pl.pallas_call(
        functools.partial(_router_kernel, n_tok=n_tok),
        grid=(pl.cdiv(n_tok, tm),),
        in_specs=[pl.BlockSpec((tm, D_MODEL), lambda g: (tile0 + g, 0)),
                  _layer_spec(*lw['g_ffn']), _layer_spec(*lw['w_router'])],
        out_specs=[pl.BlockSpec((N_EXPERTS, tm), lambda g: (0, g)),
                   pl.BlockSpec((tm, N_EXPERTS), lambda g: (g, 0)),
                   _whole_out_spec((N_EXPERTS, 1)),
                   pl.BlockSpec((tm, D_MODEL // 2), lambda g: (g, 0))],
        out_shape=[jax.ShapeDtypeStruct((N_EXPERTS, n_tok), I32),
                   jax.ShapeDtypeStruct((n_tok, N_EXPERTS), F32),
                   jax.ShapeDtypeStruct((N_EXPERTS, 1), F32),
                   jax.ShapeDtypeStruct((n_tok, D_MODEL // 2), jnp.uint32)],
        scratch_shapes=[pltpu.VMEM((N_EXPERTS, 1), F32)],
        compiler_params=_tc_params(),
        name="router",
    )(h, lw['g_ffn'][0], lw['w_router'][0])


def _sc_chunk(n_rows):
    per_worker = pl.cdiv(n_rows, SC_WORKERS)
    return pl.cdiv(per_worker, SC_WINDOW) * SC_WINDOW


def _sc_worker_base(n_rows, chunk):
    wid = lax.axis_index("s") * SC_CORES + lax.axis_index("c")
    return jnp.minimum(wid * chunk, n_rows - chunk)


def _sc_scatter_rows(x, row0, dest, n_out):
    n = dest.shape[0] // TOP_K
    width = x.shape[1]
    chunk = _sc_chunk(n)
    mesh = plsc.VectorSubcoreMesh(core_axis_name="c", subcore_axis_name="s")

    @functools.partial(
        pl.kernel, mesh=mesh,
        out_type=jax.ShapeDtypeStruct((n_out, width), x.dtype),
        scratch_types=[pltpu.VMEM((SC_WINDOW,), I32) for _ in range(TOP_K)]
        + [pltpu.VMEM((SC_WINDOW, width), x.dtype), pltpu.SemaphoreType.DMA],
        name="sc_scatter_rows",
    )
    def scatter(x_hbm, dest_hbm, out_hbm, idx0_v, idx1_v, rows_v, sem):
        base = _sc_worker_base(n, chunk)

        @pl.loop(0, chunk // SC_WINDOW)
        def _(j):
            off = pl.multiple_of(base + j * SC_WINDOW, 8)
            pltpu.sync_copy(dest_hbm.at[pl.ds(off, SC_WINDOW)], idx0_v)
            pltpu.sync_copy(dest_hbm.at[pl.ds(n + off, SC_WINDOW)], idx1_v)
            pltpu.sync_copy(x_hbm.at[pl.ds(row0 + off, SC_WINDOW)], rows_v)
            first = pltpu.async_copy(rows_v, out_hbm.at[idx0_v], sem)
            second = pltpu.async_copy(rows_v, out_hbm.at[idx1_v], sem)
            first.wait()
            second.wait()

    return scatter(x, dest)


def _sc_gather_rows(y, idx):
    n = idx.shape[0]
    width = y.shape[1]
    chunk = _sc_chunk(n)
    mesh = plsc.VectorSubcoreMesh(core_axis_name="c", subcore_axis_name="s")

    @functools.partial(
        pl.kernel, mesh=mesh,
        out_type=jax.ShapeDtypeStruct((n, width), y.dtype),
        scratch_types=[pltpu.VMEM((SC_WINDOW,), I32),
                       pltpu.VMEM((SC_WINDOW, width), y.dtype), pltpu.SemaphoreType.DMA],
        name="sc_gather_rows",
    )
    def gather(y_hbm, idx_hbm, out_hbm, idx_v, rows_v, sem):
        base = _sc_worker_base(n, chunk)

        @pl.loop(0, chunk // SC_WINDOW)
        def _(j):
            off = pl.multiple_of(base + j * SC_WINDOW, 8)
            pltpu.sync_copy(idx_hbm.at[pl.ds(off, SC_WINDOW)], idx_v)
            pltpu.async_copy(y_hbm.at[idx_v], rows_v, sem).wait()
            pltpu.sync_copy(rows_v, out_hbm.at[pl.ds(off, SC_WINDOW)])

    return gather(y, idx)


def _pack_bf16_halves(y):
    half = y.shape[1] // 2
    rounded = y.astype(BF16).astype(F32)
    hi = lax.bitcast_convert_type(rounded[:, :half], jnp.uint32)
    lo = lax.bitcast_convert_type(rounded[:, half:], jnp.uint32)
    return hi | (lo >> 16)


def _unpack_bf16_halves(packed):
    hi = lax.bitcast_convert_type(packed & jnp.uint32(0xFFFF0000), F32)
    lo = lax.bitcast_convert_type(packed << 16, F32)
    return jnp.concatenate([hi, lo], axis=1)


def _expert_kernel(tile_expert_ref, n_valid_ref, xs_ref, wg_ref, wu_ref, wd_ref, y_ref):
    @pl.when(pl.program_id(0) < n_valid_ref[0])
    def _():
        xn = _unpack_bf16_halves(xs_ref[...]).astype(BF16)
        a = jax.nn.silu(_dot(xn, wg_ref[...])) * _dot(xn, wu_ref[...])
        y_ref[...] = _pack_bf16_halves(_dot(a, wd_ref[...]))


def _experts(xs, tile_expert, n_valid, lw):
    n_slots = xs.shape[0]
    te = EXPERT_ROWS
    d_exp = lw['w_ex_gate'][0].shape[-1]
    moe = lw['w_ex_gate'][1]
    row_block = lambda g, tex, nv: (jnp.minimum(g, nv[0] - 1), 0)
    w_block = lambda g, tex, nv: (moe, tex[g], 0, 0)
    return pl.pallas_call(
        _expert_kernel,
        grid_spec=pltpu.PrefetchScalarGridSpec(
            num_scalar_prefetch=2,
            grid=(n_slots // te,),
            in_specs=[pl.BlockSpec((te, xs.shape[1]), row_block),
                      pl.BlockSpec((None, None, D_MODEL, d_exp), w_block),
                      pl.BlockSpec((None, None, D_MODEL, d_exp), w_block),
                      pl.BlockSpec((None, None, d_exp, D_MODEL), w_block)],
            out_specs=pl.BlockSpec((te, D_MODEL // 2), row_block)),
        out_shape=jax.ShapeDtypeStruct((n_slots, D_MODEL // 2), jnp.uint32),
        compiler_params=_tc_params(),
        name="experts",
    )(tile_expert, n_valid, xs, *[lw[k][0] for k in ('w_ex_gate', 'w_ex_up', 'w_ex_down')])


def _combine_math(h_ref, yg_ref, mf_ref, p, gple_ref, wpg_ref, wpp_ref, gfin_ref):
    gates = mf_ref[...]
    h = h_ref[...] + (gates[:, 0:1] * _unpack_bf16_halves(yg_ref[0])
                      + gates[:, 1:2] * _unpack_bf16_halves(yg_ref[1]))
    return _rms(_ple(h, p, gple_ref, wpg_ref, wpp_ref, _dot), gfin_ref[...])


def _combine_seq_kernel(h_ref, yg_ref, mf_ref, pp_ref, gple_ref, wpg_ref, wpp_ref, gfin_ref,
                        ybuf_ref, yp_ref):
    del ybuf_ref
    yp_ref[...] = _combine_math(h_ref, yg_ref, mf_ref, pp_ref[...], gple_ref, wpg_ref,
                                wpp_ref, gfin_ref)


def _combine_tail_kernel(h_ref, yg_ref, mf_ref, pp_ref, ps_ref, gple_ref, wpg_ref, wpp_ref,
                         gfin_ref, ybuf_ref, yp_ref, ys_ref, pbuf_ref):
    del ybuf_ref
    g = pl.program_id(0)
    last = pl.num_programs(0) - 1
    n_dec = ps_ref.shape[0]
    pbuf_ref[...] = pp_ref[...]

    @pl.when(g == last)
    def _():
        pbuf_ref[0:n_dec, :] = ps_ref[...]

    out = _combine_math(h_ref, yg_ref, mf_ref, pbuf_ref[...], gple_ref, wpg_ref, wpp_ref,
                        gfin_ref)

    @pl.when(g < last)
    def _():
        yp_ref[...] = out

    @pl.when(g == last)
    def _():
        ys_ref[...] = out[0:n_dec, :]


def _combine(h, tile0, yg, mf, y_seq, lw, with_rows):
    tm = COMBINE_ROWS
    assert tile0 * TOK_ROWS % tm == 0
    tile0 = tile0 * TOK_ROWS // tm
    n_tiles = pl.cdiv(yg.shape[1], tm)
    last_seq_tile = y_seq.shape[0] // tm - 1
    pp, layer = lw['p_seq']
    n_dec = lw['p_row'][0].shape[-2]
    weights = [lw[k] for k in (('p_row',) if with_rows else ())
               + ('g_ple', 'w_ple_gate', 'w_ple_proj', 'g_final')]
    seq_tile = lambda g: (jnp.minimum(tile0 + g, last_seq_tile), 0)
    n_in = 4 + len(weights)
    out = pl.pallas_call(
        _combine_tail_kernel if with_rows else _combine_seq_kernel,
        grid=(n_tiles,),
        in_specs=[pl.BlockSpec((tm, D_MODEL), lambda g: (tile0 + g, 0)),
                  pl.BlockSpec((TOP_K, tm, yg.shape[2]), lambda g: (0, g, 0)),
                  pl.BlockSpec((tm, N_EXPERTS), lambda g: (g, 0)),
                  pl.BlockSpec((None, tm, D_PLE), lambda g: (layer,) + seq_tile(g))]
        + [_layer_spec(*w) for w in weights]
        + [pl.BlockSpec(memory_space=pl.ANY)],
        out_specs=[pl.BlockSpec((tm, D_MODEL), seq_tile)]
        + ([_whole_out_spec((n_dec, D_MODEL))] if with_rows else []),
        out_shape=[jax.ShapeDtypeStruct(y_seq.shape, F32)]
        + ([jax.ShapeDtypeStruct((n_dec, D_MODEL), F32)] if with_rows else []),
        scratch_shapes=[pltpu.VMEM((tm, D_PLE), F32)] if with_rows else [],
        input_output_aliases={n_in: 0},
        compiler_params=_tc_params(),
        name="combine",
    )(h, yg, mf, pp, *[w for w, _ in weights], y_seq)
    return out if with_rows else (out[0], None)


def _moe_layer(h, y_seq, lw):
    n_tok = h.shape[0]
    tm = TOK_ROWS
    te = EXPERT_ROWS
    tiles = pl.cdiv(n_tok, tm)
    bounds = [tiles * c // MOE_CHUNKS for c in range(MOE_CHUNKS + 1)]
    y_rows = None
    for c in range(MOE_CHUNKS):
        tile0 = bounds[c]
        n = min(bounds[c + 1] * tm, n_tok) - tile0 * tm
        mi, mf, counts, xp = _router(h, tile0, n, lw)

        cnt = counts[:, 0].astype(I32)
        padded = (cnt + te - 1) // te * te
        ends = jnp.cumsum(padded)
        starts = ends - padded
        experts = jnp.arange(N_EXPERTS, dtype=I32)
        start_of = lambda e: jnp.sum(
            jnp.where(e[None, :] == experts[:, None], starts[:, None], 0), axis=0)
        dest = jnp.concatenate([start_of(mi[0]) + mi[2], start_of(mi[1]) + mi[3]])
        dest, h = lax.optimization_barrier((dest, h))
        n_tiles = pl.cdiv(TOP_K * n + N_EXPERTS * (te - 1), te)
        tile_start = jnp.arange(n_tiles, dtype=I32) * te
        last_used = jnp.max(jnp.where(padded > 0, experts, 0))
        tile_expert = jnp.minimum(
            jnp.sum(tile_start[:, None] >= ends[None, :], axis=-1).astype(I32), last_used)
        n_valid = (ends[-1:] // te).astype(I32)

        xs = _sc_scatter_rows(xp, 0, dest, n_tiles * te)
        y = _experts(xs, tile_expert, n_valid, lw)
        yg = _sc_gather_rows(y, dest).reshape(TOP_K, n, y.shape[1])
        y_seq, rows = _combine(h, tile0, yg, mf, y_seq, lw, with_rows=c == MOE_CHUNKS - 1)
        y_rows = rows if rows is not None else y_rows
    return y_seq, y_rows


def kernel(x_prompt, x_sample, state_conv_a, state_conv_c, p_prompt, p_sample, g_mix, w_in, w_conv_a, w_s, b_s, g_ln_b, b_ln_b, w_conv_c, b_conv_c, g_ln_c, b_ln_c, g_out, w_o, g_ffn, w_ff_gate, w_ff_up, w_ff_down, w_router, w_ex_gate, w_ex_up, w_ex_down, g_ple, w_ple_gate, w_ple_proj, g_final):
    depth = g_mix.shape[0]
    n_seq, seq, _ = x_prompt.shape
    n_dec = x_sample.shape[0]
    n_prompt = n_seq * seq
    assert depth == 2 and x_sample.shape[1] == 1
    assert seq % MIXER_ROWS == 0 and TOK_ROWS % n_dec == 0 and w_ff_gate.shape[-1] % FF_COLS == 0

    vectors = {'g_mix': g_mix, 'g_out': g_out, 'g_ffn': g_ffn, 'g_ple': g_ple,
               'b_conv_c': b_conv_c, 'g_ln_c': g_ln_c, 'b_ln_c': b_ln_c,
               'g_ln_b': g_ln_b, 'b_ln_b': b_ln_b}
    pieces, lanes, total = [], {}, 0
    for name, x in sorted(vectors.items(), key=lambda kv: -kv[1].shape[1]):
        n = x.shape[1]
        gap = -total % n
        pieces += [jnp.zeros((depth, gap), F32)] * (gap > 0) + [x]
        lanes[name] = (total + gap, n)
        total += gap + n
    packed = jnp.concatenate(pieces, axis=1).reshape(depth, 1, total)
    per_layer = {
        'w_in_f32': w_in, 'w_conv_a': w_conv_a,
        'w_s_cat': jnp.transpose(w_s, (0, 2, 1, 3)).reshape(depth, CHUNK, N_HEADS_B * CHUNK),
        'b_s_full': jnp.repeat(jnp.swapaxes(b_s, 1, 2), HEAD_DIM, axis=2),
        'w_conv_c': w_conv_c, 'w_o_f32': w_o,
        'w_ple_gate_f32': w_ple_gate, 'w_ple_proj_f32': w_ple_proj,
        'state_a': jnp.swapaxes(state_conv_a, 1, 2), 'state_c': jnp.swapaxes(state_conv_c, 1, 2),
        'p_seq': p_prompt.reshape(depth, n_prompt, D_PLE),
        'p_row': p_sample.reshape(depth, n_dec, D_PLE),
    }
    per_dense = {'w_ff_gate_f32': w_ff_gate, 'w_ff_up_f32': w_ff_up, 'w_ff_down_f32': w_ff_down}
    one_layer = lambda w: (w.reshape((1,) + w.shape), 0)
    per_moe = {'w_router': jnp.pad(w_router, ((0, 0), (0, 0), (0, LANES - N_EXPERTS)))}
    to_convert = [('w_ex_gate', w_ex_gate), ('w_ex_up', w_ex_up), ('w_ex_down', w_ex_down)]

    def with_ride(call):
        name, w = to_convert.pop(0)
        *outputs, w_bf16 = call(w.reshape(-1, w.shape[-1]))
        per_moe[name] = w_bf16.reshape(w.shape)
        return outputs

    hp = x_prompt.reshape(n_prompt, D_MODEL)
    hs = x_sample.reshape(n_dec, D_MODEL)
    outs = {'a_p': [], 'c_p': []}
    row_states = None
    for i in range(depth):
        is_expert_layer = i % 2 == 1
        lw = {k: (v, i) for k, v in per_layer.items()}
        lw.update({k: (packed, _Lanes(i, off, n)) for k, (off, n) in lanes.items()})
        lw.update({k: (v, i // 2) for k, v in (per_dense if not is_expert_layer else {}).items()})
        lw['g_final'] = (g_final.reshape(1, 1, -1), 0)

        hs, *row_states, w0, w1, w2, w3 = _mixer_row(hs, 0, lw, row_states)
        lw.update({k: one_layer(w) for k, w in zip(CONVERTED_BY_MIXER_ROW, (w0, w1, w2, w3))})
        mixer_in = hp
        hp, a_p, c_p = with_ride(functools.partial(
            _mixer_seq, hp, hs if is_expert_layer else None, lw, n_seq, seq))
        outs['a_p'].append(a_p)
        outs['c_p'].append(c_p)
        if not is_expert_layer:
            hs, *converted = _ffn_row(hs, lw)
            lw.update({k: one_layer(w) for k, w in
                       zip(('w_ff_gate', 'w_ff_up', 'w_ff_down'), converted)})
            hp, = with_ride(functools.partial(_ffn_dense, hp, lw))
        else:
            lw.update({k: (v, i // 2) for k, v in per_moe.items()})
            y_prompt, y_sample = _moe_layer(hp, mixer_in, lw)

    a_s, c_s, v_s = row_states
    return (y_prompt.reshape(x_prompt.shape), y_sample.reshape(x_sample.shape),
            jnp.stack(outs['a_p']), a_s, jnp.stack(outs['c_p']), c_s,
            v_s.reshape(depth, n_dec, 1, D_B))
```
